```python
import math
import jax, jax.numpy as jnp
from jax import lax
import numpy as np

D_MODEL = 2048
BATCH = 4
SEQ = 4096
DEPTH = 2

CHUNK = 64
MEM_LEN = 256
ROPE_THETA = 10000.0
NORM_EPS = 1e-6

RET_HEADS = 4
RET_QK_DIM = 256
RET_V_DIM = 512
SG_GROUPS = 4
SG_DIM = 256
SG_BLOCK = 128
DIFF_HEADS = 8
DIFF_HEAD_DIM = 128
Q_BLOCK = 128
XA_HEADS = 4
XA_HEAD_DIM = D_MODEL // XA_HEADS
D_FF = 4 * D_MODEL

N_EVEN = (DEPTH + 1) // 2
N_ODD = DEPTH // 2

RET_W = RET_HEADS * RET_QK_DIM
RET_VW = RET_HEADS * RET_V_DIM
SG_W = SG_GROUPS * SG_DIM
EVEN_IN = 2 * RET_W + 2 * RET_VW + 2 * SG_W
EVEN_OUT = RET_VW + SG_W
DIFF_QK_W = DIFF_HEADS * 2 * DIFF_HEAD_DIM
DIFF_V_W = DIFF_HEADS * 2 * DIFF_HEAD_DIM

kernel_name = 'chunk_causal_hybrid_retention_sgmlp_diffattn'


def rms_norm(x, gain):
    xf = x.astype(jnp.float32)
    y = xf * lax.rsqrt(jnp.mean(xf * xf, axis=-1, keepdims=True) + NORM_EPS)
    return (y * gain.astype(jnp.float32)).astype(x.dtype)


def rope_tables(positions, dim):
    inv = ROPE_THETA ** (-jnp.arange(0, dim, 2, dtype=jnp.float32) / dim)
    ang = positions.astype(jnp.float32)[..., None] * inv
    return jnp.cos(ang), jnp.sin(ang)


def apply_rope(x, cos, sin):
    half = x.shape[-1] // 2
    x1, x2 = x[..., :half], x[..., half:]
    c = cos.astype(x.dtype)
    s = sin.astype(x.dtype)
    return jnp.concatenate([x1 * c - x2 * s, x2 * c + x1 * s], axis=-1)


def split_heads(t, h):
    b, s, w = t.shape
    return t.reshape(b, s, h, w // h).transpose(0, 2, 1, 3)


def merge_heads(t):
    b, h, s, d = t.shape
    return t.transpose(0, 2, 1, 3).reshape(b, s, h * d)


def retention(q, k, v):
    out_dtype = v.dtype
    q = q.astype(jnp.float32)
    k = k.astype(jnp.float32)
    v = v.astype(jnp.float32)
    b, h, s, dk = q.shape
    dv = v.shape[-1]
    nc = s // CHUNK
    log_g = jnp.log(1.0 - 2.0 ** (-5.0 - jnp.arange(h, dtype=jnp.float32)))
    idx = jnp.arange(CHUNK, dtype=jnp.float32)
    intra = jnp.exp(log_g[:, None, None] * jnp.abs(idx[:, None] - idx[None, :]))
    q_decay = jnp.exp(log_g[:, None] * idx)[..., None]
    k_decay = jnp.exp(log_g[:, None] * (CHUNK - idx))[..., None]
    chunk_decay = jnp.exp(log_g * CHUNK)[:, None, None]

    def to_chunks(t):
        return t.reshape(b, h, nc, CHUNK, t.shape[-1]).transpose(2, 0, 1, 3, 4)

    def step(state, inp):
        qc, kc, vc = inp
        scores = jnp.einsum('bhid,bhjd->bhij', qc, kc) * intra
        out = (jnp.einsum('bhij,bhje->bhie', scores, vc)
               + jnp.einsum('bhid,bhde->bhie', qc * q_decay, state))
        state = state * chunk_decay + jnp.einsum('bhjd,bhje->bhde', kc * k_decay, vc)
        return state, out

    init = jnp.zeros((b, h, dk, dv), jnp.float32)
    _, outs = lax.scan(step, init, (to_chunks(q), to_chunks(k), to_chunks(v)))
    return outs.transpose(1, 2, 0, 3, 4).reshape(b, h, s, dv).astype(out_dtype)


def spatial_gate(u, v, w_s, b_s):
    b, s, g, dg = v.shape
    nb = s // SG_BLOCK
    vb = v.reshape(b, nb, SG_BLOCK, g, dg)
    pos_chunk = jnp.arange(SG_BLOCK) // CHUNK
    mask = pos_chunk[None, :] <= pos_chunk[:, None]
    w = jnp.where(mask[None], w_s, jnp.zeros_like(w_s))
    mixed = jnp.einsum('gij,bnjgc->bnigc', w, vb) + b_s.T[None, None, :, :, None]
    return u * mixed.reshape(b, s, g, dg)


def even_mixer(h, cos_r, sin_r, w_in, ret_gain, sg_norm_gain, sg_w, sg_b, w_out):
    b, s, _ = h.shape
    proj = h @ w_in
    cuts = [RET_W, 2 * RET_W, 2 * RET_W + RET_VW, 2 * RET_W + 2 * RET_VW, 2 * RET_W + 2 * RET_VW + SG_W]
    q, k, v, g, u, vs = jnp.split(proj, cuts, axis=-1)
    c = cos_r[:, None]
    sn = sin_r[:, None]
    q = apply_rope(split_heads(q, RET_HEADS), c, sn)
    k = apply_rope(split_heads(k, RET_HEADS), c, sn) * (RET_QK_DIM ** -0.5)
    ret = retention(q, k, split_heads(v, RET_HEADS))
    ret = rms_norm(ret.transpose(0, 2, 1, 3), ret_gain).reshape(b, s, RET_VW)
    out_a = jax.nn.silu(g) * ret
    u = jax.nn.gelu(u).reshape(b, s, SG_GROUPS, SG_DIM)
    vs = rms_norm(jax.nn.gelu(vs).reshape(b, s, SG_GROUPS, SG_DIM), sg_norm_gain)
    out_b = spatial_gate(u, vs, sg_w, sg_b).reshape(b, s, SG_W)
    return jnp.concatenate([out_a, out_b], axis=-1) @ w_out


def diff_attention(h, cos_d, sin_d, w_qkv, q_gain, k_gain, lam_q1, lam_k1, lam_q2, lam_k2,
                   sub_gain, w_o, lambda_init):
    b, s, _ = h.shape
    proj = h @ w_qkv
    q, k, v = jnp.split(proj, [DIFF_QK_W, 2 * DIFF_QK_W], axis=-1)
    q = q.reshape(b, s, DIFF_HEADS, 2, DIFF_HEAD_DIM).transpose(0, 2, 3, 1, 4)
    k = k.reshape(b, s, DIFF_HEADS, 2, DIFF_HEAD_DIM).transpose(0, 2, 3, 1, 4)
    v = split_heads(v, DIFF_HEADS)
    c = cos_d[:, None, None]
    sn = sin_d[:, None, None]
    q = apply_rope(rms_norm(q, q_gain), c, sn) * (DIFF_HEAD_DIM ** -0.5)
    k = apply_rope(rms_norm(k, k_gain), c, sn)
    lam = (jnp.exp(jnp.sum(lam_q1.astype(jnp.float32) * lam_k1.astype(jnp.float32)))
           - jnp.exp(jnp.sum(lam_q2.astype(jnp.float32) * lam_k2.astype(jnp.float32)))
           + lambda_init)
    nqb = s // Q_BLOCK
    qb = q.reshape(b, DIFF_HEADS, 2, nqb, Q_BLOCK, DIFF_HEAD_DIM).transpose(3, 0, 1, 2, 4, 5)
    key_chunk = jnp.arange(s) // CHUNK
    q_off = jnp.arange(nqb) * Q_BLOCK

    def block(args):
        qblk, off = args
        q_chunk = (off + jnp.arange(Q_BLOCK)) // CHUNK
        mask = key_chunk[None, :] <= q_chunk[:, None]
        sc = jnp.einsum('bhmqd,bhmkd->bhmqk', qblk, k).astype(jnp.float32)
        sc = jnp.where(mask, sc, jnp.float32(-1e30))
        p = jax.nn.softmax(sc, axis=-1)
        attn = p[:, :, 0] - lam * p[:, :, 1]
        return jnp.einsum('bhqk,bhke->bhqe', attn.astype(v.dtype), v)

    out = lax.map(block, (qb, q_off))
    out = out.transpose(1, 0, 3, 2, 4).reshape(b, s, DIFF_HEADS, 2 * DIFF_HEAD_DIM)
    out = rms_norm(out, sub_gain) * (1.0 - lambda_init)
    return out.reshape(b, s, DIFF_V_W) @ w_o


def memory_cross_attention(h, mem_n, w_q, w_kv, q_gain, k_gain, w_o):
    q = split_heads(h @ w_q, XA_HEADS)
    k, v = jnp.split(mem_n @ w_kv, 2, axis=-1)
    k = rms_norm(split_heads(k, XA_HEADS), k_gain)
    v = split_heads(v, XA_HEADS)
    q = rms_norm(q, q_gain) * (XA_HEAD_DIM ** -0.5)
    sc = jnp.einsum('bhqd,bhkd->bhqk', q, k).astype(jnp.float32)
    p = jax.nn.softmax(sc, axis=-1).astype(v.dtype)
    return merge_heads(jnp.einsum('bhqk,bhkd->bhqd', p, v)) @ w_o


def squared_relu_mlp(h, w1, w2):
    a = jax.nn.relu(h @ w1)
    return (a * a) @ w2


def setup_inputs(seed: int = 0) -> dict:
    key = jax.random.key(seed)
    ks = iter(jax.random.split(key, 48))

    def nrm(shape, scale):
        return jax.random.normal(next(ks), shape, jnp.float32) * scale

    def gain(shape):
        return 1.0 + 0.02 * jax.random.normal(next(ks), shape, jnp.float32)

    D = D_MODEL
    x = nrm((BATCH, SEQ, D), 1.0)
    mem = nrm((BATCH, MEM_LEN, D), 1.0)
    offset = jax.random.randint(next(ks), (BATCH, 1), 0, 64, dtype=jnp.int32) * CHUNK
    positions = (offset + jnp.arange(SEQ, dtype=jnp.int32)[None, :]).astype(jnp.int32)
    return {
        'x': x,
        'mem': mem,
        'positions': positions,
        'norm_mix': gain((DEPTH, D)),
        'norm_xa': gain((DEPTH, D)),
        'norm_mem': gain((DEPTH, D)),
        'norm_ffn': gain((DEPTH, D)),
        'ev_w_in': nrm((N_EVEN, D, EVEN_IN), D ** -0.5),
        'ev_ret_gain': gain((N_EVEN, RET_HEADS, RET_V_DIM)),
        'ev_sg_norm': gain((N_EVEN, SG_GROUPS, SG_DIM)),
        'ev_sg_w': nrm((N_EVEN, SG_GROUPS, SG_BLOCK, SG_BLOCK), SG_BLOCK ** -0.5),
        'ev_sg_b': 1.0 + nrm((N_EVEN, SG_GROUPS, SG_BLOCK), 0.01),
        'ev_w_out': nrm((N_EVEN, EVEN_OUT, D), EVEN_OUT ** -0.5),
        'od_w_qkv': nrm((N_ODD, D, 2 * DIFF_QK_W + DIFF_V_W), D ** -0.5),
        'od_q_gain': gain((N_ODD, DIFF_HEAD_DIM)),
        'od_k_gain': gain((N_ODD, DIFF_HEAD_DIM)),
        'od_lam_q1': nrm((N_ODD, DIFF_HEAD_DIM), 0.1),
        'od_lam_k1': nrm((N_ODD, DIFF_HEAD_DIM), 0.1),
        'od_lam_q2': nrm((N_ODD, DIFF_HEAD_DIM), 0.1),
        'od_lam_k2': nrm((N_ODD, DIFF_HEAD_DIM), 0.1),
        'od_sub_gain': gain((N_ODD, 2 * DIFF_HEAD_DIM)),
        'od_w_o': nrm((N_ODD, DIFF_V_W, D), DIFF_V_W ** -0.5),
        'xa_w_q': nrm((DEPTH, D, D), D ** -0.5),
        'xa_w_kv': nrm((DEPTH, D, 2 * D), D ** -0.5),
        'xa_q_gain': gain((DEPTH, XA_HEAD_DIM)),
        'xa_k_gain': gain((DEPTH, XA_HEAD_DIM)),
        'xa_w_o': nrm((DEPTH, D, D), D ** -0.5),
        'ffn_w1': nrm((DEPTH, D, D_FF), D ** -0.5),
        'ffn_w2': nrm((DEPTH, D_FF, D), D_FF ** -0.5),
    }


def reference(x, mem, positions, norm_mix, norm_xa, norm_mem, norm_ffn,
              ev_w_in, ev_ret_gain, ev_sg_norm, ev_sg_w, ev_sg_b, ev_w_out,
              od_w_qkv, od_q_gain, od_k_gain, od_lam_q1, od_lam_k1, od_lam_q2, od_lam_k2,
              od_sub_gain, od_w_o,
              xa_w_q, xa_w_kv, xa_q_gain, xa_k_gain, xa_w_o,
              ffn_w1, ffn_w2):
    cos_r, sin_r = rope_tables(positions, RET_QK_DIM)
    cos_d, sin_d = rope_tables(positions, DIFF_HEAD_DIM)
    for li in range(DEPTH):
        h = rms_norm(x, norm_mix[li])
        if li % 2 == 0:
            e = li // 2
            x = x + even_mixer(h, cos_r, sin_r, ev_w_in[e], ev_ret_gain[e], ev_sg_norm[e],
                               ev_sg_w[e], ev_sg_b[e], ev_w_out[e])
        else:
            o = li // 2
            lambda_init = 0.8 - 0.6 * math.exp(-0.3 * li)
            x = x + diff_attention(h, cos_d, sin_d, od_w_qkv[o], od_q_gain[o], od_k_gain[o],
                                   od_lam_q1[o], od_lam_k1[o], od_lam_q2[o], od_lam_k2[o],
                                   od_sub_gain[o], od_w_o[o], lambda_init)
        x = x + memory_cross_attention(rms_norm(x, norm_xa[li]), rms_norm(mem, norm_mem[li]),
                                       xa_w_q[li], xa_w_kv[li], xa_q_gain[li], xa_k_gain[li],
                                       xa_w_o[li])
        x = x + squared_relu_mlp(rms_norm(x, norm_ffn[li]), ffn_w1[li], ffn_w2[li])
    return x
```

```python
import functools
import math

import jax
import jax.numpy as jnp
from jax import lax
from jax.experimental import pallas as pl
from jax.experimental.pallas import tpu as pltpu

F32 = jnp.float32
BF16 = jnp.bfloat16

CHUNK = 64
CHUNK_SHIFT = CHUNK.bit_length() - 1
ROPE_THETA = 10000.0
NORM_EPS = 1e-6
RET_HEADS = 4
RET_QK_DIM = 256
RET_V_DIM = 512
SG_GROUPS = 4
SG_DIM = 256
SG_BLOCK = 128
DIFF_HEADS = 8
DIFF_HEAD_DIM = 128
XA_HEADS = 4
MASK_VALUE = -1e30

V7X_VMEM_BYTES = 64 * 1024 * 1024
VMEM_REQUEST_CAP = V7X_VMEM_BYTES - 8 * 1024 * 1024
LANES = 128


def _vmem_limit(block_bytes, scratch_bytes=0):
    need = 2 * block_bytes + scratch_bytes
    return int(min(VMEM_REQUEST_CAP, max(32 * 1024 * 1024, 2 * need)))


def _nbytes(shape, dtype):
    return math.prod(shape) * jnp.dtype(dtype).itemsize


def _rms(x, gain):
    ms = jnp.mean(x * x, axis=-1, keepdims=True)
    return x * lax.rsqrt(ms + NORM_EPS) * gain


def _gelu_tanh(x):
    c = math.sqrt(2.0 / math.pi)
    return 0.5 * x * (1.0 + jnp.tanh(c * (x + 0.044715 * (x * x * x))))


def _rope_kernel(pos_ref, inv_r_ref, inv_d_ref, cr_ref, sr_ref, cd_ref, sd_ref):
    pos = pos_ref[...].astype(F32)
    ang_r = pos * inv_r_ref[...]
    cr_ref[...] = jnp.cos(ang_r)
    sr_ref[...] = jnp.sin(ang_r)
    ang_d = pos * inv_d_ref[...]
    lane = lax.broadcasted_iota(jnp.int32, ang_d.shape, 1)
    sin_d = jnp.sin(ang_d)
    cd_ref[...] = jnp.cos(ang_d)
    sd_ref[...] = jnp.where(lane < DIFF_HEAD_DIM // 2, -sin_d, sin_d)


def _rope_tables(positions):
    b, s = positions.shape
    rows = 1024
    n = b * s
    inv_r = ROPE_THETA ** (-jnp.arange(0, RET_QK_DIM, 2, dtype=F32) / RET_QK_DIM)
    inv_d = ROPE_THETA ** (-jnp.arange(0, DIFF_HEAD_DIM, 2, dtype=F32) / DIFF_HEAD_DIM)
    inv_d = jnp.concatenate([inv_d, inv_d])
    out = jax.ShapeDtypeStruct((n, LANES), F32)
    tab_spec = pl.BlockSpec((rows, LANES), lambda i: (i, 0))
    vec_spec = pl.BlockSpec((1, LANES), lambda i: (0, 0))
    outs = pl.pallas_call(
        _rope_kernel,
        grid=(n // rows,),
        in_specs=[pl.BlockSpec((rows, 1), lambda i: (i, 0)), vec_spec, vec_spec],
        out_specs=[tab_spec] * 4,
        out_shape=[out] * 4,
        name="rope_tables",
    )(positions.reshape(n, 1), inv_r.reshape(1, LANES), inv_d.reshape(1, LANES))
    return [t.reshape(b, s, LANES) for t in outs]


def _norm_mm_kernel(x_ref, g_ref, w_ref, o_ref, xn_ref, *, act):
    @pl.when(pl.program_id(1) == 0)
    def _():
        xn_ref[...] = _rms(x_ref[...], g_ref[...]).astype(BF16)

    y = jnp.dot(xn_ref[...], w_ref[...], preferred_element_type=F32)
    if act == "relu2":
        y = jnp.maximum(y, 0.0)
        y = y * y
    o_ref[...] = y.astype(o_ref.dtype)


def _norm_mm(x, gain, w, *, act=None, tm=1024, tn=512, name):
    m, k = x.shape
    n = w.shape[1]
    tm = min(tm, m)
    assert m % tm == 0 and n % tn == 0
    blocks = (_nbytes((tm, k), F32) + _nbytes((k, tn), BF16) + _nbytes((tm, tn), BF16)
              + _nbytes((1, k), F32))
    return pl.pallas_call(
        functools.partial(_norm_mm_kernel, act=act),
        grid=(m // tm, n // tn),
        in_specs=[
            pl.BlockSpec((tm, k), lambda i, j: (i, 0)),
            pl.BlockSpec((1, k), lambda i, j: (0, 0)),
            pl.BlockSpec((k, tn), lambda i, j: (0, j)),
        ],
        out_specs=pl.BlockSpec((tm, tn), lambda i, j: (i, j)),
        out_shape=jax.ShapeDtypeStruct((m, n), BF16),
        scratch_shapes=[pltpu.VMEM((tm, k), BF16)],
        compiler_params=pltpu.CompilerParams(
            dimension_semantics=("parallel", "arbitrary"),
            vmem_limit_bytes=_vmem_limit(blocks, _nbytes((tm, k), BF16))),
        name=name,
    )(x, gain.reshape(1, k), w)


def _mm_res_kernel(a_ref, w_ref, r_ref, o_ref, acc_ref, *, nk):
    k = pl.program_id(2)

    @pl.when(k == 0)
    def _():
        acc_ref[...] = r_ref[...]

    acc_ref[...] += jnp.dot(a_ref[...], w_ref[...], preferred_element_type=F32)

    @pl.when(k == nk - 1)
    def _():
        o_ref[...] = acc_ref[...]


def _mm_res(a, w, res, *, tm=1024, tn=512, tk=2048, name):
    m, kdim = a.shape
    n = w.shape[1]
    tk = min(tk, kdim)
    assert m % tm == 0 and n % tn == 0 and kdim % tk == 0
    nk = kdim // tk
    blocks = (_nbytes((tm, tk), BF16) + _nbytes((tk, tn), BF16) + 2 * _nbytes((tm, tn), F32))
    return pl.pallas_call(
        functools.partial(_mm_res_kernel, nk=nk),
        grid=(m // tm, n // tn, nk),
        in_specs=[
            pl.BlockSpec((tm, tk), lambda i, j, k: (i, k)),
            pl.BlockSpec((tk, tn), lambda i, j, k: (k, j)),
            pl.BlockSpec((tm, tn), lambda i, j, k: (i, j)),
        ],
        out_specs=pl.BlockSpec((tm, tn), lambda i, j, k: (i, j)),
        out_shape=jax.ShapeDtypeStruct((m, n), F32),
        scratch_shapes=[pltpu.VMEM((tm, tn), F32)],
        compiler_params=pltpu.CompilerParams(
            dimension_semantics=("parallel", "parallel", "arbitrary"),
            vmem_limit_bytes=_vmem_limit(blocks, _nbytes((tm, tn), F32))),
        name=name,
    )(a, w, res)


def _mm2_res_kernel(a1_ref, a2_ref, w1_ref, w2_ref, r_ref, o_ref):
    acc = r_ref[...] + jnp.dot(a1_ref[...], w1_ref[...], preferred_element_type=F32)
    o_ref[...] = acc + jnp.dot(a2_ref[...], w2_ref[...], preferred_element_type=F32)


def _mm2_res(a1, a2, w, res, *, tm=1024, tn=512, name):
    m, k1 = a1.shape
    k2 = a2.shape[1]
    n = w.shape[1]
    assert w.shape[0] == k1 + k2 and k1 % k2 == 0
    assert m % tm == 0 and n % tn == 0
    blocks = (_nbytes((tm, k1 + k2), BF16) + _nbytes((k1 + k2, tn), BF16)
              + 2 * _nbytes((tm, tn), F32))
    return pl.pallas_call(
        _mm2_res_kernel,
        grid=(m // tm, n // tn),
        in_specs=[
            pl.BlockSpec((tm, k1), lambda i, j: (i, 0)),
            pl.BlockSpec((tm, k2), lambda i, j: (i, 0)),
            pl.BlockSpec((k1, tn), lambda i, j: (0, j)),
            pl.BlockSpec((k2, tn), lambda i, j: (k1 // k2, j)),
            pl.BlockSpec((tm, tn), lambda i, j: (i, j)),
        ],
        out_specs=pl.BlockSpec((tm, tn), lambda i, j: (i, j)),
        out_shape=jax.ShapeDtypeStruct((m, n), F32),
        compiler_params=pltpu.CompilerParams(
            dimension_semantics=("parallel", "parallel"),
            vmem_limit_bytes=_vmem_limit(blocks)),
        name=name,
    )(a1, a2, w, w, res)


def _ret_kernel(q_ref, k_ref, v_ref, g_ref, cos_ref, sin_ref, gain_ref, lg_ref, o_ref,
                state_ref, dmat_ref, qd_ref, kd_ref, *, blk):
    n = pl.program_id(2)
    lg = lg_ref[0][:, 0:1]

    @pl.when(n == 0)
    def _():
        state_ref[...] = jnp.zeros_like(state_ref)
        i = lax.broadcasted_iota(jnp.int32, (blk, blk), 0)
        j = lax.broadcasted_iota(jnp.int32, (blk, blk), 1)
        ci = i >> CHUNK_SHIFT
        cj = j >> CHUNK_SHIFT
        d = (i - j).astype(F32)
        expo = jnp.where(ci == cj, jnp.abs(d), d)
        dmat_ref[...] = jnp.where(ci >= cj, jnp.exp(lg * expo), 0.0)
        r = lax.broadcasted_iota(jnp.int32, (blk, RET_QK_DIM), 0).astype(F32)
        qd_ref[...] = jnp.exp(lg * r)
        kd_ref[...] = jnp.exp(lg * (blk - r))

    half = RET_QK_DIM // 2
    c = cos_ref[0]
    s = sin_ref[0]

    def rope(x):
        x1 = x[:, :half]
        x2 = x[:, half:]
        return jnp.concatenate([x1 * c - x2 * s, x2 * c + x1 * s], axis=-1)

    q = rope(q_ref[0].astype(F32))
    k = rope(k_ref[0].astype(F32)) * (RET_QK_DIM ** -0.5)
    v = v_ref[0]
    state = state_ref[...]

    scores = lax.dot_general(q.astype(BF16), k.astype(BF16), (((1,), (1,)), ((), ())),
                             preferred_element_type=F32) * dmat_ref[...]
    out = jnp.dot(scores.astype(BF16), v, preferred_element_type=F32)
    out = out + jnp.dot((q * qd_ref[...]).astype(BF16), state.astype(BF16),
                        preferred_element_type=F32)
    kv = lax.dot_general((k * kd_ref[...]).astype(BF16), v, (((0,), (0,)), ((), ())),
                         preferred_element_type=F32)
    state_ref[...] = state * jnp.exp(lg * blk) + kv

    y = _rms(out, gain_ref[0])
    g = g_ref[0].astype(F32)
    o_ref[0] = (g / (1.0 + jnp.exp(-g)) * y).astype(o_ref.dtype)


def _retention(proj, cos_r, sin_r, ret_gain, *, blk=256):
    b, s, _ = proj.shape
    h = RET_HEADS
    dk, dv = RET_QK_DIM, RET_V_DIM
    v_off = 2 * h * dk // dv
    g_off = v_off + h
    log_g = jnp.log(1.0 - 2.0 ** (-5.0 - jnp.arange(h, dtype=F32)))
    log_g = jnp.broadcast_to(log_g[:, None, None], (h, 1, LANES))
    blocks = (2 * _nbytes((blk, dk), BF16) + 3 * _nbytes((blk, dv), BF16)
              + 2 * _nbytes((blk, LANES), F32))
    scratch = (_nbytes((dk, dv), F32) + _nbytes((blk, blk), F32) + 2 * _nbytes((blk, dk), F32))
    return pl.pallas_call(
        functools.partial(_ret_kernel, blk=blk),
        grid=(b, h, s // blk),
        in_specs=[
            pl.BlockSpec((1, blk, dk), lambda bi, hi, n: (bi, n, hi)),
            pl.BlockSpec((1, blk, dk), lambda bi, hi, n: (bi, n, h + hi)),
            pl.BlockSpec((1, blk, dv), lambda bi, hi, n: (bi, n, v_off + hi)),
            pl.BlockSpec((1, blk, dv), lambda bi, hi, n: (bi, n, g_off + hi)),
            pl.BlockSpec((1, blk, LANES), lambda bi, hi, n: (bi, n, 0)),
            pl.BlockSpec((1, blk, LANES), lambda bi, hi, n: (bi, n, 0)),
            pl.BlockSpec((1, 1, dv), lambda bi, hi, n: (hi, 0, 0)),
            pl.BlockSpec((1, 1, LANES), lambda bi, hi, n: (hi, 0, 0)),
        ],
        out_specs=pl.BlockSpec((1, blk, dv), lambda bi, hi, n: (bi, n, hi)),
        out_shape=jax.ShapeDtypeStruct((b, s, h * dv), BF16),
        scratch_shapes=[
            pltpu.VMEM((dk, dv), F32),
            pltpu.VMEM((blk, blk), F32),
            pltpu.VMEM((blk, dk), F32),
            pltpu.VMEM((blk, dk), F32),
        ],
        compiler_params=pltpu.CompilerParams(
            dimension_semantics=("parallel", "parallel", "arbitrary"),
            vmem_limit_bytes=_vmem_limit(blocks, scratch)),
        name="retention",
    )(proj, proj, proj, proj, cos_r, sin_r, ret_gain.reshape(h, 1, dv), log_g)


def _sg_kernel(u_ref, v_ref, gain_ref, w_ref, b_ref, o_ref, *, rows):
    i = lax.broadcasted_iota(jnp.int32, (SG_BLOCK, SG_BLOCK), 0)
    j = lax.broadcasted_iota(jnp.int32, (SG_BLOCK, SG_BLOCK), 1)
    mask = (j >> CHUNK_SHIFT) <= (i >> CHUNK_SHIFT)
    for g in range(SG_GROUPS):
        cols = slice(g * SG_DIM, (g + 1) * SG_DIM)
        w = jnp.where(mask, w_ref[g], 0.0).astype(BF16)
        bias = b_ref[g]
        u = _gelu_tanh(u_ref[0, :, cols].astype(F32))
        v = _rms(_gelu_tanh(v_ref[0, :, cols].astype(F32)), gain_ref[:, cols]).astype(BF16)
        for n in range(rows // SG_BLOCK):
            r = slice(n * SG_BLOCK, (n + 1) * SG_BLOCK)
            mixed = jnp.dot(w, v[r], preferred_element_type=F32) + bias
            o_ref[0, r, cols] = (u[r] * mixed).astype(o_ref.dtype)


def _spatial_gate(proj, sg_norm, sg_w, sg_b, *, rows=512):
    b, s, width = proj.shape
    sgw = SG_GROUPS * SG_DIM
    u_blk = (width - 2 * sgw) // sgw
    blocks = 3 * _nbytes((rows, sgw), BF16) + _nbytes(sg_w.shape, F32)
    return pl.pallas_call(
        functools.partial(_sg_kernel, rows=rows),
        grid=(b, s // rows),
        in_specs=[
            pl.BlockSpec((1, rows, sgw), lambda bi, n: (bi, n, u_blk)),
            pl.BlockSpec((1, rows, sgw), lambda bi, n: (bi, n, u_blk + 1)),
            pl.BlockSpec((1, sgw), lambda bi, n: (0, 0)),
            pl.BlockSpec((SG_GROUPS, SG_BLOCK, SG_BLOCK), lambda bi, n: (0, 0, 0)),
            pl.BlockSpec((SG_GROUPS, SG_BLOCK, 1), lambda bi, n: (0, 0, 0)),
        ],
        out_specs=pl.BlockSpec((1, rows, sgw), lambda bi, n: (bi, n, 0)),
        out_shape=jax.ShapeDtypeStruct((b, s, sgw), BF16),
        compiler_params=pltpu.CompilerParams(
            dimension_semantics=("parallel", "parallel"),
            vmem_limit_bytes=_vmem_limit(blocks)),
        name="spatial_gate",
    )(proj, proj, sg_norm.reshape(1, sgw), sg_w, sg_b.reshape(SG_GROUPS, SG_BLOCK, 1))


def _diff_attn_kernel(q_ref, k_ref, v_ref, cos_ref, sin_ref, qg_ref, kg_ref, sg_ref, lam_ref,
                      o_ref, kn_ref, qn_ref, m_ref, l_ref, acc_ref, *, tq, lambda_init):
    qi = pl.program_id(2)
    seq = k_ref.shape[1]
    d = DIFF_HEAD_DIM

    def norm_rope(x, gain, c, s):
        y = _rms(x, gain)
        return y * c + pltpu.roll(y, d // 2, 1) * s

    @pl.when(qi == 0)
    def _():
        kg = kg_ref[...]

        def body(ci, carry):
            r0 = pl.multiple_of(ci * tq, tq)
            kc = k_ref[0, pl.ds(r0, tq), :].astype(F32)
            c = cos_ref[0, pl.ds(r0, tq), :]
            s = sin_ref[0, pl.ds(r0, tq), :]
            kn_ref[pl.ds(r0, tq), :] = jnp.concatenate(
                [norm_rope(kc[:, :d], kg, c, s), norm_rope(kc[:, d:], kg, c, s)],
                axis=-1).astype(BF16)
            return carry

        lax.fori_loop(0, seq // tq, body, 0)

    q0 = pl.multiple_of(qi * tq, tq)
    cq = cos_ref[0, pl.ds(q0, tq), :]
    sq = sin_ref[0, pl.ds(q0, tq), :]
    qf = q_ref[0].astype(F32)
    qg = qg_ref[...]
    scale = d ** -0.5
    qn_ref[...] = jnp.concatenate(
        [norm_rope(qf[:, :d], qg, cq, sq) * scale, norm_rope(qf[:, d:], qg, cq, sq) * scale],
        axis=-1).astype(BF16)
    m_ref[...] = jnp.full(m_ref.shape, MASK_VALUE, F32)
    l_ref[...] = jnp.zeros_like(l_ref)
    acc_ref[...] = jnp.zeros_like(acc_ref)

    row = lax.broadcasted_iota(jnp.int32, (tq, tq), 0)
    col = lax.broadcasted_iota(jnp.int32, (tq, tq), 1)
    diag_mask = (col >> CHUNK_SHIFT) <= (row >> CHUNK_SHIFT)

    def step(kv, masked):
        k0 = pl.multiple_of(kv * tq, tq)
        kt = kn_ref[pl.ds(k0, tq), :]
        vt = v_ref[0, pl.ds(k0, tq), :]
        for mi in range(2):
            lanes = slice(mi * d, (mi + 1) * d)
            s = lax.dot_general(qn_ref[:, lanes], kt[:, lanes], (((1,), (1,)), ((), ())),
                                preferred_element_type=F32)
            if masked:
                s = jnp.where(diag_mask, s, MASK_VALUE)
            m_prev = m_ref[mi]
            m_new = jnp.maximum(m_prev, jnp.max(s, axis=-1, keepdims=True))
            alpha = jnp.exp(m_prev - m_new)
            p = jnp.exp(s - m_new)
            l_ref[mi] = alpha * l_ref[mi] + jnp.sum(p, axis=-1, keepdims=True)
            acc_ref[mi] = alpha * acc_ref[mi] + jnp.dot(p.astype(BF16), vt,
                                                        preferred_element_type=F32)
            m_ref[mi] = m_new

    def body(kv, carry):
        step(kv, False)
        return carry

    lax.fori_loop(0, qi, body, 0)
    step(qi, True)

    lam_p = lam_ref[...]
    lam = (jnp.exp(jnp.sum(lam_p[0:1] * lam_p[1:2], axis=-1, keepdims=True))
           - jnp.exp(jnp.sum(lam_p[2:3] * lam_p[3:4], axis=-1, keepdims=True))
           + lambda_init)
    out = acc_ref[0] / l_ref[0] - lam * (acc_ref[1] / l_ref[1])
    o_ref[0] = (_rms(out, sg_ref[...]) * (1.0 - lambda_init)).astype(o_ref.dtype)


def _diff_attention(proj, cos_d, sin_d, q_gain, k_gain, sub_gain, lam_params, lambda_init,
                    *, tq=256):
    b, s, _ = proj.shape
    h = DIFF_HEADS
    hw = 2 * DIFF_HEAD_DIM
    blocks = (_nbytes((tq, hw), BF16) * 2 + 2 * _nbytes((s, hw), BF16)
              + 2 * _nbytes((s, LANES), F32))
    scratch = (_nbytes((s, hw), BF16) + _nbytes((tq, hw), BF16) + 2 * _nbytes((2, tq, LANES), F32)
               + _nbytes((2, tq, hw), F32))
    return pl.pallas_call(
        functools.partial(_diff_attn_kernel, tq=tq, lambda_init=lambda_init),
        grid=(b, h, s // tq),
        in_specs=[
            pl.BlockSpec((1, tq, hw), lambda bi, hi, qi: (bi, qi, hi)),
            pl.BlockSpec((1, s, hw), lambda bi, hi, qi: (bi, 0, h + hi)),
            pl.BlockSpec((1, s, hw), lambda bi, hi, qi: (bi, 0, 2 * h + hi)),
            pl.BlockSpec((1, s, LANES), lambda bi, hi, qi: (bi, 0, 0)),
            pl.BlockSpec((1, s, LANES), lambda bi, hi, qi: (bi, 0, 0)),
            pl.BlockSpec((1, DIFF_HEAD_DIM), lambda bi, hi, qi: (0, 0)),
            pl.BlockSpec((1, DIFF_HEAD_DIM), lambda bi, hi, qi: (0, 0)),
            pl.BlockSpec((1, hw), lambda bi, hi, qi: (0, 0)),
            pl.BlockSpec((4, DIFF_HEAD_DIM), lambda bi, hi, qi: (0, 0)),
        ],
        out_specs=pl.BlockSpec((1, tq, hw), lambda bi, hi, qi: (bi, qi, hi)),
        out_shape=jax.ShapeDtypeStruct((b, s, h * hw), BF16),
        scratch_shapes=[
            pltpu.VMEM((s, hw), BF16),
            pltpu.VMEM((tq, hw), BF16),
            pltpu.VMEM((2, tq, 1), F32),
            pltpu.VMEM((2, tq, 1), F32),
            pltpu.VMEM((2, tq, hw), F32),
        ],
        compiler_params=pltpu.CompilerParams(
            dimension_semantics=("parallel", "parallel", "arbitrary"),
            vmem_limit_bytes=_vmem_limit(blocks, scratch)),
        name="diff_attention",
    )(proj, proj, proj, cos_d, sin_d, q_gain.reshape(1, -1), k_gain.reshape(1, -1),
      sub_gain.reshape(1, -1), lam_params)


def _xa_kernel(q_ref, kv_ref, qg_ref, kg_ref, o_ref, kn_ref, *, heads):
    dm = q_ref.shape[2]
    hd = dm // heads

    @pl.when(pl.program_id(1) == 0)
    def _():
        for h in range(heads):
            cols = slice(h * hd, (h + 1) * hd)
            kn_ref[:, cols] = _rms(kv_ref[0, :, cols].astype(F32), kg_ref[...]).astype(BF16)

    scale = hd ** -0.5
    for h in range(heads):
        cols = slice(h * hd, (h + 1) * hd)
        qn = (_rms(q_ref[0, :, cols].astype(F32), qg_ref[...]) * scale).astype(BF16)
        s = lax.dot_general(qn, kn_ref[:, cols], (((1,), (1,)), ((), ())),
                            preferred_element_type=F32)
        e = jnp.exp(s - jnp.max(s, axis=-1, keepdims=True))
        p = e / jnp.sum(e, axis=-1, keepdims=True)
        v = kv_ref[0, :, dm + h * hd: dm + (h + 1) * hd]
        o_ref[0, :, cols] = jnp.dot(p.astype(BF16), v,
                                    preferred_element_type=F32).astype(o_ref.dtype)


def _cross_attention(q, kv, q_gain, k_gain, *, tq=512):
    b, s, dm = q.shape
    mlen = kv.shape[1]
    hd = dm // XA_HEADS
    blocks = 2 * _nbytes((tq, dm), BF16) + _nbytes((mlen, 2 * dm), BF16)
    return pl.pallas_call(
        functools.partial(_xa_kernel, heads=XA_HEADS),
        grid=(b, s // tq),
        in_specs=[
            pl.BlockSpec((1, tq, dm), lambda bi, qi: (bi, qi, 0)),
            pl.BlockSpec((1, mlen, 2 * dm), lambda bi, qi: (bi, 0, 0)),
            pl.BlockSpec((1, hd), lambda bi, qi: (0, 0)),
            pl.BlockSpec((1, hd), lambda bi, qi: (0, 0)),
        ],
        out_specs=pl.BlockSpec((1, tq, dm), lambda bi, qi: (bi, qi, 0)),
        out_shape=jax.ShapeDtypeStruct((b, s, dm), BF16),
        scratch_shapes=[pltpu.VMEM((mlen, dm), BF16)],
        compiler_params=pltpu.CompilerParams(
            dimension_semantics=("parallel", "arbitrary"),
            vmem_limit_bytes=_vmem_limit(blocks, _nbytes((mlen, dm), BF16))),
        name="cross_attention",
    )(q, kv, q_gain.reshape(1, hd), k_gain.reshape(1, hd))


def kernel(x, mem, positions, norm_mix, norm_xa, norm_mem, norm_ffn, ev_w_in, ev_ret_gain,
           ev_sg_norm, ev_sg_w, ev_sg_b, ev_w_out, od_w_qkv, od_q_gain, od_k_gain, od_lam_q1,
           od_lam_k1, od_lam_q2, od_lam_k2, od_sub_gain, od_w_o, xa_w_q, xa_w_kv, xa_q_gain,
           xa_k_gain, xa_w_o, ffn_w1, ffn_w2):
    b, s, dm = x.shape
    mlen = mem.shape[1]
    depth = norm_mix.shape[0]
    m = b * s
    bf = lambda w: w.astype(BF16)

    cos_r, sin_r, cos_d, sin_d = _rope_tables(positions)
    xf = x.reshape(m, dm)
    mem_f = mem.reshape(b * mlen, dm)

    for li in range(depth):
        if li % 2 == 0:
            e = li // 2
            proj = _norm_mm(xf, norm_mix[li], bf(ev_w_in[e]), name="even_in_proj")
            proj = proj.reshape(b, s, -1)
            out_a = _retention(proj, cos_r, sin_r, ev_ret_gain[e])
            out_b = _spatial_gate(proj, ev_sg_norm[e], ev_sg_w[e], ev_sg_b[e])
            xf = _mm2_res(out_a.reshape(m, -1), out_b.reshape(m, -1), bf(ev_w_out[e]), xf,
                          name="even_out_proj")
        else:
            o = li // 2
            lambda_init = 0.8 - 0.6 * math.exp(-0.3 * li)
            proj = _norm_mm(xf, norm_mix[li], bf(od_w_qkv[o]), name="diff_qkv_proj")
            lam_params = jnp.stack([od_lam_q1[o], od_lam_k1[o], od_lam_q2[o], od_lam_k2[o]])
            att = _diff_attention(proj.reshape(b, s, -1), cos_d, sin_d, od_q_gain[o],
                                  od_k_gain[o], od_sub_gain[o], lam_params, lambda_init)
            xf = _mm_res(att.reshape(m, -1), bf(od_w_o[o]), xf, name="diff_out_proj")

        q = _norm_mm(xf, norm_xa[li], bf(xa_w_q[li]), name="xa_q_proj")
        kv = _norm_mm(mem_f, norm_mem[li], bf(xa_w_kv[li]), name="xa_kv_proj")
        att = _cross_attention(q.reshape(b, s, dm), kv.reshape(b, mlen, 2 * dm),
                               xa_q_gain[li], xa_k_gain[li])
        xf = _mm_res(att.reshape(m, dm), bf(xa_w_o[li]), xf, name="xa_out_proj")

        hid = _norm_mm(xf, norm_ffn[li], bf(ffn_w1[li]), act="relu2", name="ffn_up")
        xf = _mm_res(hid, bf(ffn_w2[li]), xf, name="ffn_down")

    return xf.reshape(b, s, dm)
```

```python
import functools
import math

import jax
import jax.numpy as jnp
from jax import lax
from jax.experimental import pallas as pl
from jax.experimental.pallas import tpu as pltpu

F32 = jnp.float32
BF16 = jnp.bfloat16

CHUNK = 64
CHUNK_SHIFT = CHUNK.bit_length() - 1
ROPE_THETA = 10000.0
NORM_EPS = 1e-6
RET_HEADS = 4
RET_QK_DIM = 256
RET_V_DIM = 512
SG_GROUPS = 4
SG_DIM = 256
SG_BLOCK = 128
DIFF_HEADS = 8
DIFF_HEAD_DIM = 128
XA_HEADS = 4
MASK_VALUE = -1e30

V7X_VMEM_BYTES = 64 * 1024 * 1024
VMEM_REQUEST_CAP = V7X_VMEM_BYTES - 8 * 1024 * 1024
LANES = 128


def _vmem_limit(block_bytes, scratch_bytes=0):
    need = 2 * block_bytes + scratch_bytes
    return int(min(VMEM_REQUEST_CAP, max(32 * 1024 * 1024, 2 * need)))


def _nbytes(shape, dtype):
    return math.prod(shape) * jnp.dtype(dtype).itemsize


def _rms(x, gain):
    ms = jnp.mean(x * x, axis=-1, keepdims=True)
    return x * lax.rsqrt(ms + NORM_EPS) * gain


def _gelu_tanh(x):
    c = math.sqrt(2.0 / math.pi)
    return 0.5 * x * (1.0 + jnp.tanh(c * (x + 0.044715 * (x * x * x))))


def _rope_kernel(pos_ref, inv_r_ref, inv_d_ref, cr_ref, sr_ref, cd_ref, sd_ref):
    pos = pos_ref[...].astype(F32)
    ang_r = pos * inv_r_ref[...]
    cr_ref[...] = jnp.cos(ang_r)
    sr_ref[...] = jnp.sin(ang_r)
    ang_d = pos * inv_d_ref[...]
    lane = lax.broadcasted_iota(jnp.int32, ang_d.shape, 1)
    sin_d = jnp.sin(ang_d)
    cd_ref[...] = jnp.cos(ang_d)
    sd_ref[...] = jnp.where(lane < DIFF_HEAD_DIM // 2, -sin_d, sin_d)


def _rope_tables(positions):
    b, s = positions.shape
    rows = 1024
    n = b * s
    inv_r = ROPE_THETA ** (-jnp.arange(0, RET_QK_DIM, 2, dtype=F32) / RET_QK_DIM)
    inv_d = ROPE_THETA ** (-jnp.arange(0, DIFF_HEAD_DIM, 2, dtype=F32) / DIFF_HEAD_DIM)
    inv_d = jnp.concatenate([inv_d, inv_d])
    out = jax.ShapeDtypeStruct((n, LANES), F32)
    tab_spec = pl.BlockSpec((rows, LANES), lambda i: (i, 0))
    vec_spec = pl.BlockSpec((1, LANES), lambda i: (0, 0))
    outs = pl.pallas_call(
        _rope_kernel,
        grid=(n // rows,),
        in_specs=[pl.BlockSpec((rows, 1), lambda i: (i, 0)), vec_spec, vec_spec],
        out_specs=[tab_spec] * 4,
        out_shape=[out] * 4,
        name="rope_tables",
    )(positions.reshape(n, 1), inv_r.reshape(1, LANES), inv_d.reshape(1, LANES))
    return [t.reshape(b, s, LANES) for t in outs]


def _norm_mm_kernel(x_ref, g_ref, w_ref, o_ref, xn_ref, *, act):
    @pl.when(pl.program_id(1) == 0)
    def _():
        xn_ref[...] = _rms(x_ref[...], g_ref[...]).astype(BF16)

    y = jnp.dot(xn_ref[...], w_ref[...], preferred_element_type=F32)
    if act == "relu2":
        y = jnp.maximum(y, 0.0)
        y = y * y
    o_ref[...] = y.astype(o_ref.dtype)


def _norm_mm(x, gain, w, *, act=None, tm=1024, tn=1024, name):
    m, k = x.shape
    n = w.shape[1]
    tm = min(tm, m)
    assert m % tm == 0 and n % tn == 0
    blocks = (_nbytes((tm, k), F32) + _nbytes((k, tn), BF16) + _nbytes((tm, tn), BF16)
              + _nbytes((1, k), F32))
    return pl.pallas_call(
        functools.partial(_norm_mm_kernel, act=act),
        grid=(m // tm, n // tn),
        in_specs=[
            pl.BlockSpec((tm, k), lambda i, j: (i, 0)),
            pl.BlockSpec((1, k), lambda i, j: (0, 0)),
            pl.BlockSpec((k, tn), lambda i, j: (0, j)),
        ],
        out_specs=pl.BlockSpec((tm, tn), lambda i, j: (i, j)),
        out_shape=jax.ShapeDtypeStruct((m, n), BF16),
        scratch_shapes=[pltpu.VMEM((tm, k), BF16)],
        compiler_params=pltpu.CompilerParams(
            dimension_semantics=("parallel", "arbitrary"),
            vmem_limit_bytes=_vmem_limit(blocks, _nbytes((tm, k), BF16))),
        name=name,
    )(x, gain.reshape(1, k), w)


def _mm_res_kernel(a_ref, w_ref, r_ref, o_ref, acc_ref, *, nk):
    k = pl.program_id(2)

    @pl.when(k == 0)
    def _():
        acc_ref[...] = r_ref[...]

    acc_ref[...] += jnp.dot(a_ref[...], w_ref[...], preferred_element_type=F32)

    @pl.when(k == nk - 1)
    def _():
        o_ref[...] = acc_ref[...]


def _mm_res(a, w, res, *, tm=1024, tn=1024, tk=2048, name):
    m, kdim = a.shape
    n = w.shape[1]
    tk = min(tk, kdim)
    assert m % tm == 0 and n % tn == 0 and kdim % tk == 0
    nk = kdim // tk
    blocks = (_nbytes((tm, tk), BF16) + _nbytes((tk, tn), BF16) + 2 * _nbytes((tm, tn), F32))
    return pl.pallas_call(
        functools.partial(_mm_res_kernel, nk=nk),
        grid=(m // tm, n // tn, nk),
        in_specs=[
            pl.BlockSpec((tm, tk), lambda i, j, k: (i, k)),
            pl.BlockSpec((tk, tn), lambda i, j, k: (k, j)),
            pl.BlockSpec((tm, tn), lambda i, j, k: (i, j)),
        ],
        out_specs=pl.BlockSpec((tm, tn), lambda i, j, k: (i, j)),
        out_shape=jax.ShapeDtypeStruct((m, n), F32),
        scratch_shapes=[pltpu.VMEM((tm, tn), F32)],
        compiler_params=pltpu.CompilerParams(
            dimension_semantics=("parallel", "parallel", "arbitrary"),
            vmem_limit_bytes=_vmem_limit(blocks, _nbytes((tm, tn), F32))),
        name=name,
    )(a, w, res)


def _mm2_res_kernel(a1_ref, a2_ref, w1_ref, w2_ref, r_ref, o_ref):
    acc = r_ref[...] + jnp.dot(a1_ref[...], w1_ref[...], preferred_element_type=F32)
    o_ref[...] = acc + jnp.dot(a2_ref[...], w2_ref[...], preferred_element_type=F32)


def _mm2_res(a1, a2, w, res, *, tm=1024, tn=1024, name):
    m, k1 = a1.shape
    k2 = a2.shape[1]
    n = w.shape[1]
    assert w.shape[0] == k1 + k2 and k1 % k2 == 0
    assert m % tm == 0 and n % tn == 0
    blocks = (_nbytes((tm, k1 + k2), BF16) + _nbytes((k1 + k2, tn), BF16)
              + 2 * _nbytes((tm, tn), F32))
    return pl.pallas_call(
        _mm2_res_kernel,
        grid=(m // tm, n // tn),
        in_specs=[
            pl.BlockSpec((tm, k1), lambda i, j: (i, 0)),
            pl.BlockSpec((tm, k2), lambda i, j: (i, 0)),
            pl.BlockSpec((k1, tn), lambda i, j: (0, j)),
            pl.BlockSpec((k2, tn), lambda i, j: (k1 // k2, j)),
            pl.BlockSpec((tm, tn), lambda i, j: (i, j)),
        ],
        out_specs=pl.BlockSpec((tm, tn), lambda i, j: (i, j)),
        out_shape=jax.ShapeDtypeStruct((m, n), F32),
        compiler_params=pltpu.CompilerParams(
            dimension_semantics=("parallel", "parallel"),
            vmem_limit_bytes=_vmem_limit(blocks)),
        name=name,
    )(a1, a2, w, w, res)


def _ret_kernel(q_ref, k_ref, v_ref, g_ref, cos_ref, sin_ref, gain_ref, lg_ref, o_ref,
                state_ref, dmat_ref, qd_ref, kd_ref, *, blk):
    n = pl.program_id(2)
    lg = lg_ref[0][:, 0:1]

    @pl.when(n == 0)
    def _():
        state_ref[...] = jnp.zeros_like(state_ref)
        i = lax.broadcasted_iota(jnp.int32, (blk, blk), 0)
        j = lax.broadcasted_iota(jnp.int32, (blk, blk), 1)
        ci = i >> CHUNK_SHIFT
        cj = j >> CHUNK_SHIFT
        d = (i - j).astype(F32)
        expo = jnp.where(ci == cj, jnp.abs(d), d)
        dmat_ref[...] = jnp.where(ci >= cj, jnp.exp(lg * expo), 0.0)
        r = lax.broadcasted_iota(jnp.int32, (blk, RET_QK_DIM), 0).astype(F32)
        qd_ref[...] = jnp.exp(lg * r)
        kd_ref[...] = jnp.exp(lg * (blk - r))

    half = RET_QK_DIM // 2
    c = cos_ref[0]
    s = sin_ref[0]

    def rope(x):
        x1 = x[:, :half]
        x2 = x[:, half:]
        return jnp.concatenate([x1 * c - x2 * s, x2 * c + x1 * s], axis=-1)

    q = rope(q_ref[0].astype(F32))
    k = rope(k_ref[0].astype(F32)) * (RET_QK_DIM ** -0.5)
    v = v_ref[0]
    state = state_ref[...]

    scores = lax.dot_general(q.astype(BF16), k.astype(BF16), (((1,), (1,)), ((), ())),
                             preferred_element_type=F32) * dmat_ref[...]
    out = jnp.dot(scores.astype(BF16), v, preferred_element_type=F32)
    out = out + jnp.dot((q * qd_ref[...]).astype(BF16), state.astype(BF16),
                        preferred_element_type=F32)
    kv = lax.dot_general((k * kd_ref[...]).astype(BF16), v, (((0,), (0,)), ((), ())),
                         preferred_element_type=F32)
    state_ref[...] = state * jnp.exp(lg * blk) + kv

    y = _rms(out, gain_ref[0])
    g = g_ref[0].astype(F32)
    o_ref[0] = (g / (1.0 + jnp.exp(-g)) * y).astype(o_ref.dtype)


def _retention(proj, cos_r, sin_r, ret_gain, *, blk=256):
    b, s, _ = proj.shape
    h = RET_HEADS
    dk, dv = RET_QK_DIM, RET_V_DIM
    v_off = 2 * h * dk // dv
    g_off = v_off + h
    log_g = jnp.log(1.0 - 2.0 ** (-5.0 - jnp.arange(h, dtype=F32)))
    log_g = jnp.broadcast_to(log_g[:, None, None], (h, 1, LANES))
    blocks = (2 * _nbytes((blk, dk), BF16) + 3 * _nbytes((blk, dv), BF16)
              + 2 * _nbytes((blk, LANES), F32))
    scratch = (_nbytes((dk, dv), F32) + _nbytes((blk, blk), F32) + 2 * _nbytes((blk, dk), F32))
    return pl.pallas_call(
        functools.partial(_ret_kernel, blk=blk),
        grid=(b, h, s // blk),
        in_specs=[
            pl.BlockSpec((1, blk, dk), lambda bi, hi, n: (bi, n, hi)),
            pl.BlockSpec((1, blk, dk), lambda bi, hi, n: (bi, n, h + hi)),
            pl.BlockSpec((1, blk, dv), lambda bi, hi, n: (bi, n, v_off + hi)),
            pl.BlockSpec((1, blk, dv), lambda bi, hi, n: (bi, n, g_off + hi)),
            pl.BlockSpec((1, blk, LANES), lambda bi, hi, n: (bi, n, 0)),
            pl.BlockSpec((1, blk, LANES), lambda bi, hi, n: (bi, n, 0)),
            pl.BlockSpec((1, 1, dv), lambda bi, hi, n: (hi, 0, 0)),
            pl.BlockSpec((1, 1, LANES), lambda bi, hi, n: (hi, 0, 0)),
        ],
        out_specs=pl.BlockSpec((1, blk, dv), lambda bi, hi, n: (bi, n, hi)),
        out_shape=jax.ShapeDtypeStruct((b, s, h * dv), BF16),
        scratch_shapes=[
            pltpu.VMEM((dk, dv), F32),
            pltpu.VMEM((blk, blk), F32),
            pltpu.VMEM((blk, dk), F32),
            pltpu.VMEM((blk, dk), F32),
        ],
        compiler_params=pltpu.CompilerParams(
            dimension_semantics=("parallel", "parallel", "arbitrary"),
            vmem_limit_bytes=_vmem_limit(blocks, scratch)),
        name="retention",
    )(proj, proj, proj, proj, cos_r, sin_r, ret_gain.reshape(h, 1, dv), log_g)


def _sg_kernel(u_ref, v_ref, gain_ref, w_ref, b_ref, o_ref, *, rows):
    i = lax.broadcasted_iota(jnp.int32, (SG_BLOCK, SG_BLOCK), 0)
    j = lax.broadcasted_iota(jnp.int32, (SG_BLOCK, SG_BLOCK), 1)
    mask = (j >> CHUNK_SHIFT) <= (i >> CHUNK_SHIFT)
    for g in range(SG_GROUPS):
        cols = slice(g * SG_DIM, (g + 1) * SG_DIM)
        w = jnp.where(mask, w_ref[g], 0.0).astype(BF16)
        bias = b_ref[g]
        u = _gelu_tanh(u_ref[0, :, cols].astype(F32))
        v = _rms(_gelu_tanh(v_ref[0, :, cols].astype(F32)), gain_ref[:, cols]).astype(BF16)
        for n in range(rows // SG_BLOCK):
            r = slice(n * SG_BLOCK, (n + 1) * SG_BLOCK)
            mixed = jnp.dot(w, v[r], preferred_element_type=F32) + bias
            o_ref[0, r, cols] = (u[r] * mixed).astype(o_ref.dtype)


def _spatial_gate(proj, sg_norm, sg_w, sg_b, *, rows=512):
    b, s, width = proj.shape
    sgw = SG_GROUPS * SG_DIM
    u_blk = (width - 2 * sgw) // sgw
    blocks = 3 * _nbytes((rows, sgw), BF16) + _nbytes(sg_w.shape, F32)
    return pl.pallas_call(
        functools.partial(_sg_kernel, rows=rows),
        grid=(b, s // rows),
        in_specs=[
            pl.BlockSpec((1, rows, sgw), lambda bi, n: (bi, n, u_blk)),
            pl.BlockSpec((1, rows, sgw), lambda bi, n: (bi, n, u_blk + 1)),
            pl.BlockSpec((1, sgw), lambda bi, n: (0, 0)),
            pl.BlockSpec((SG_GROUPS, SG_BLOCK, SG_BLOCK), lambda bi, n: (0, 0, 0)),
            pl.BlockSpec((SG_GROUPS, SG_BLOCK, 1), lambda bi, n: (0, 0, 0)),
        ],
        out_specs=pl.BlockSpec((1, rows, sgw), lambda bi, n: (bi, n, 0)),
        out_shape=jax.ShapeDtypeStruct((b, s, sgw), BF16),
        compiler_params=pltpu.CompilerParams(
            dimension_semantics=("parallel", "parallel"),
            vmem_limit_bytes=_vmem_limit(blocks)),
        name="spatial_gate",
    )(proj, proj, sg_norm.reshape(1, sgw), sg_w, sg_b.reshape(SG_GROUPS, SG_BLOCK, 1))


def _diff_attn_kernel(q_ref, k_ref, v_ref, cos_ref, sin_ref, qg_ref, kg_ref, sg_ref, lam_ref,
                      o_ref, kn_ref, qn_ref, s_ref, m_ref, l_ref, acc_ref, *, tq, lambda_init):
    qi = pl.program_id(2)
    seq = k_ref.shape[1]
    d = DIFF_HEAD_DIM
    prep = min(tq, 256)

    def norm_rope(x, gain, c, s):
        y = _rms(x, gain)
        return y * c + pltpu.roll(y, d // 2, 1) * s

    @pl.when(qi == 0)
    def _():
        kg = kg_ref[...]

        def body(ci, carry):
            r0 = pl.multiple_of(ci * prep, prep)
            kc = k_ref[0, pl.ds(r0, prep), :].astype(F32)
            c = cos_ref[0, pl.ds(r0, prep), :]
            s = sin_ref[0, pl.ds(r0, prep), :]
            kn_ref[pl.ds(r0, prep), :] = jnp.concatenate(
                [norm_rope(kc[:, :d], kg, c, s), norm_rope(kc[:, d:], kg, c, s)],
                axis=-1).astype(BF16)
            return carry

        lax.fori_loop(0, seq // prep, body, 0)

    q0 = pl.multiple_of(qi * tq, tq)
    cq = cos_ref[0, pl.ds(q0, tq), :]
    sq = sin_ref[0, pl.ds(q0, tq), :]
    qf = q_ref[0].astype(F32)
    qg = qg_ref[...]
    scale = d ** -0.5 * math.log2(math.e)
    qn_ref[...] = jnp.concatenate(
        [norm_rope(qf[:, :d], qg, cq, sq) * scale, norm_rope(qf[:, d:], qg, cq, sq) * scale],
        axis=-1).astype(BF16)
    m_ref[...] = jnp.full(m_ref.shape, MASK_VALUE, F32)
    l_ref[...] = jnp.zeros_like(l_ref)
    acc_ref[...] = jnp.zeros_like(acc_ref)

    n_lane_tiles = tq // LANES

    def scores(kv, masked):
        k0 = pl.multiple_of(kv * tq, tq)
        kt = kn_ref[pl.ds(k0, tq), :]
        for mi in range(2):
            lanes = slice(mi * d, (mi + 1) * d)
            s = lax.dot_general(qn_ref[:, lanes], kt[:, lanes], (((1,), (1,)), ((), ())),
                                preferred_element_type=F32)
            if masked:
                row = lax.broadcasted_iota(jnp.int32, (tq, tq), 0)
                col = lax.broadcasted_iota(jnp.int32, (tq, tq), 1)
                s = jnp.where((col >> CHUNK_SHIFT) <= (row >> CHUNK_SHIFT), s, MASK_VALUE)
            s_ref[mi, kv] = s
            m = m_ref[mi]
            for c in range(n_lane_tiles):
                m = jnp.maximum(m, s[:, c * LANES:(c + 1) * LANES])
            m_ref[mi] = m

    def scores_body(kv, carry):
        scores(kv, False)
        return carry

    lax.fori_loop(0, qi, scores_body, 0)
    scores(qi, True)

    for mi in range(2):
        m_row = jnp.max(m_ref[mi], axis=-1, keepdims=True)
        m_ref[mi] = jnp.broadcast_to(m_row, (tq, LANES))

    def weighted_values(kv, carry):
        k0 = pl.multiple_of(kv * tq, tq)
        vt = v_ref[0, pl.ds(k0, tq), :]
        for mi in range(2):
            m = m_ref[mi]
            l = l_ref[mi]
            ps = []
            for c in range(n_lane_tiles):
                p = jnp.exp2(s_ref[mi, kv, :, c * LANES:(c + 1) * LANES] - m)
                l = l + p
                ps.append(p.astype(BF16))
            l_ref[mi] = l
            acc_ref[mi] += jnp.dot(jnp.concatenate(ps, axis=-1), vt,
                                   preferred_element_type=F32)
        return carry

    lax.fori_loop(0, qi + 1, weighted_values, 0)

    lam_p = lam_ref[...]
    lam = (jnp.exp(jnp.sum(lam_p[0:1] * lam_p[1:2], axis=-1, keepdims=True))
           - jnp.exp(jnp.sum(lam_p[2:3] * lam_p[3:4], axis=-1, keepdims=True))
           + lambda_init)
    l0 = jnp.sum(l_ref[0], axis=-1, keepdims=True)
    l1 = jnp.sum(l_ref[1], axis=-1, keepdims=True)
    out = acc_ref[0] / l0 - lam * (acc_ref[1] / l1)
    o_ref[0] = (_rms(out, sg_ref[...]) * (1.0 - lambda_init)).astype(o_ref.dtype)


def _diff_attention(proj, cos_d, sin_d, q_gain, k_gain, sub_gain, lam_params, lambda_init,
                    *, tq=512):
    b, s, _ = proj.shape
    h = DIFF_HEADS
    hw = 2 * DIFF_HEAD_DIM
    blocks = (_nbytes((tq, hw), BF16) * 2 + 2 * _nbytes((s, hw), BF16)
              + 2 * _nbytes((s, LANES), F32))
    scratch = (_nbytes((s, hw), BF16) + _nbytes((tq, hw), BF16) + _nbytes((2, tq, s), F32)
               + 2 * _nbytes((2, tq, LANES), F32) + _nbytes((2, tq, hw), F32))
    return pl.pallas_call(
        functools.partial(_diff_attn_kernel, tq=tq, lambda_init=lambda_init),
        grid=(b, h, s // tq),
        in_specs=[
            pl.BlockSpec((1, tq, hw), lambda bi, hi, qi: (bi, qi, hi)),
            pl.BlockSpec((1, s, hw), lambda bi, hi, qi: (bi, 0, h + hi)),
            pl.BlockSpec((1, s, hw), lambda bi, hi, qi: (bi, 0, 2 * h + hi)),
            pl.BlockSpec((1, s, LANES), lambda bi, hi, qi: (bi, 0, 0)),
            pl.BlockSpec((1, s, LANES), lambda bi, hi, qi: (bi, 0, 0)),
            pl.BlockSpec((1, DIFF_HEAD_DIM), lambda bi, hi, qi: (0, 0)),
            pl.BlockSpec((1, DIFF_HEAD_DIM), lambda bi, hi, qi: (0, 0)),
            pl.BlockSpec((1, hw), lambda bi, hi, qi: (0, 0)),
            pl.BlockSpec((4, DIFF_HEAD_DIM), lambda bi, hi, qi: (0, 0)),
        ],
        out_specs=pl.BlockSpec((1, tq, hw), lambda bi, hi, qi: (bi, qi, hi)),
        out_shape=jax.ShapeDtypeStruct((b, s, h * hw), BF16),
        scratch_shapes=[
            pltpu.VMEM((s, hw), BF16),
            pltpu.VMEM((tq, hw), BF16),
            pltpu.VMEM((2, s // tq, tq, tq), F32),
            pltpu.VMEM((2, tq, LANES), F32),
            pltpu.VMEM((2, tq, LANES), F32),
            pltpu.VMEM((2, tq, hw), F32),
        ],
        compiler_params=pltpu.CompilerParams(
            dimension_semantics=("parallel", "parallel", "arbitrary"),
            vmem_limit_bytes=_vmem_limit(blocks, scratch)),
        name="diff_attention",
    )(proj, proj, proj, cos_d, sin_d, q_gain.reshape(1, -1), k_gain.reshape(1, -1),
      sub_gain.reshape(1, -1), lam_params)


def _xa_kernel(q_ref, kv_ref, qg_ref, kg_ref, o_ref, kn_ref, *, heads):
    dm = q_ref.shape[2]
    hd = dm // heads

    @pl.when(pl.program_id(1) == 0)
    def _():
        for h in range(heads):
            cols = slice(h * hd, (h + 1) * hd)
            kn_ref[:, cols] = _rms(kv_ref[0, :, cols].astype(F32), kg_ref[...]).astype(BF16)

    scale = hd ** -0.5
    for h in range(heads):
        cols = slice(h * hd, (h + 1) * hd)
        qn = (_rms(q_ref[0, :, cols].astype(F32), qg_ref[...]) * scale).astype(BF16)
        s = lax.dot_general(qn, kn_ref[:, cols], (((1,), (1,)), ((), ())),
                            preferred_element_type=F32)
        e = jnp.exp(s - jnp.max(s, axis=-1, keepdims=True))
        p = e / jnp.sum(e, axis=-1, keepdims=True)
        v = kv_ref[0, :, dm + h * hd: dm + (h + 1) * hd]
        o_ref[0, :, cols] = jnp.dot(p.astype(BF16), v,
                                    preferred_element_type=F32).astype(o_ref.dtype)


def _cross_attention(q, kv, q_gain, k_gain, *, tq=512):
    b, s, dm = q.shape
    mlen = kv.shape[1]
    hd = dm // XA_HEADS
    blocks = 2 * _nbytes((tq, dm), BF16) + _nbytes((mlen, 2 * dm), BF16)
    return pl.pallas_call(
        functools.partial(_xa_kernel, heads=XA_HEADS),
        grid=(b, s // tq),
        in_specs=[
            pl.BlockSpec((1, tq, dm), lambda bi, qi: (bi, qi, 0)),
            pl.BlockSpec((1, mlen, 2 * dm), lambda bi, qi: (bi, 0, 0)),
            pl.BlockSpec((1, hd), lambda bi, qi: (0, 0)),
            pl.BlockSpec((1, hd), lambda bi, qi: (0, 0)),
        ],
        out_specs=pl.BlockSpec((1, tq, dm), lambda bi, qi: (bi, qi, 0)),
        out_shape=jax.ShapeDtypeStruct((b, s, dm), BF16),
        scratch_shapes=[pltpu.VMEM((mlen, dm), BF16)],
        compiler_params=pltpu.CompilerParams(
            dimension_semantics=("parallel", "arbitrary"),
            vmem_limit_bytes=_vmem_limit(blocks, _nbytes((mlen, dm), BF16))),
        name="cross_attention",
    )(q, kv, q_gain.reshape(1, hd), k_gain.reshape(1, hd))


def kernel(x, mem, positions, norm_mix, norm_xa, norm_mem, norm_ffn, ev_w_in, ev_ret_gain,
           ev_sg_norm, ev_sg_w, ev_sg_b, ev_w_out, od_w_qkv, od_q_gain, od_k_gain, od_lam_q1,
           od_lam_k1, od_lam_q2, od_lam_k2, od_sub_gain, od_w_o, xa_w_q, xa_w_kv, xa_q_gain,
           xa_k_gain, xa_w_o, ffn_w1, ffn_w2):
    b, s, dm = x.shape
    mlen = mem.shape[1]
    depth = norm_mix.shape[0]
    m = b * s
    bf = lambda w: w.astype(BF16)

    cos_r, sin_r, cos_d, sin_d = _rope_tables(positions)
    xf = x.reshape(m, dm)
    mem_f = mem.reshape(b * mlen, dm)

    for li in range(depth):
        if li % 2 == 0:
            e = li // 2
            proj = _norm_mm(xf, norm_mix[li], bf(ev_w_in[e]), name="even_in_proj")
            proj = proj.reshape(b, s, -1)
            out_a = _retention(proj, cos_r, sin_r, ev_ret_gain[e])
            out_b = _spatial_gate(proj, ev_sg_norm[e], ev_sg_w[e], ev_sg_b[e])
            xf = _mm2_res(out_a.reshape(m, -1), out_b.reshape(m, -1), bf(ev_w_out[e]), xf,
                          name="even_out_proj")
        else:
            o = li // 2
            lambda_init = 0.8 - 0.6 * math.exp(-0.3 * li)
            proj = _norm_mm(xf, norm_mix[li], bf(od_w_qkv[o]), name="diff_qkv_proj")
            lam_params = jnp.stack([od_lam_q1[o], od_lam_k1[o], od_lam_q2[o], od_lam_k2[o]])
            att = _diff_attention(proj.reshape(b, s, -1), cos_d, sin_d, od_q_gain[o],
                                  od_k_gain[o], od_sub_gain[o], lam_params, lambda_init)
            xf = _mm_res(att.reshape(m, -1), bf(od_w_o[o]), xf, name="diff_out_proj")

        q = _norm_mm(xf, norm_xa[li], bf(xa_w_q[li]), name="xa_q_proj")
        kv = _norm_mm(mem_f, norm_mem[li], bf(xa_w_kv[li]), name="xa_kv_proj")
        att = _cross_attention(q.reshape(b, s, dm), kv.reshape(b, mlen, 2 * dm),
                               xa_q_gain[li], xa_k_gain[li])
        xf = _mm_res(att.reshape(m, dm), bf(xa_w_o[li]), xf, name="xa_out_proj")

        hid = _norm_mm(xf, norm_ffn[li], bf(ffn_w1[li]), act="relu2", name="ffn_up")
        xf = _mm_res(hid, bf(ffn_w2[li]), xf, name="ffn_down")

    return xf.reshape(b, s, dm)
```

```python
import functools
import math

import jax
import jax.numpy as jnp
from jax import lax
from jax.experimental import pallas as pl
from jax.experimental.pallas import tpu as pltpu

F32 = jnp.float32
BF16 = jnp.bfloat16

CHUNK = 64
CHUNK_SHIFT = CHUNK.bit_length() - 1
ROPE_THETA = 10000.0
NORM_EPS = 1e-6
RET_HEADS = 4
RET_QK_DIM = 256
RET_V_DIM = 512
SG_GROUPS = 4
SG_DIM = 256
SG_BLOCK = 128
DIFF_HEADS = 8
DIFF_HEAD_DIM = 128
XA_HEADS = 4
MASK_VALUE = -1e30

V7X_VMEM_BYTES = 64 * 1024 * 1024
VMEM_REQUEST_CAP = V7X_VMEM_BYTES - 8 * 1024 * 1024
LANES = 128


def _vmem_limit(block_bytes, scratch_bytes=0):
    need = 2 * block_bytes + scratch_bytes
    return int(min(VMEM_REQUEST_CAP, max(32 * 1024 * 1024, 2 * need)))


def _nbytes(shape, dtype):
    return math.prod(shape) * jnp.dtype(dtype).itemsize


def _rms(x, gain):
    ms = jnp.mean(x * x, axis=-1, keepdims=True)
    return x * lax.rsqrt(ms + NORM_EPS) * gain


def _gelu_tanh(x):
    c = math.sqrt(2.0 / math.pi)
    return 0.5 * x * (1.0 + jnp.tanh(c * (x + 0.044715 * (x * x * x))))


def _rope_kernel(pos_ref, inv_r_ref, inv_d_ref, cr_ref, sr_ref, cd_ref, sd_ref):
    pos = pos_ref[...].astype(F32)
    ang_r = pos * inv_r_ref[...]
    cr_ref[...] = jnp.cos(ang_r)
    sr_ref[...] = jnp.sin(ang_r)
    ang_d = pos * inv_d_ref[...]
    lane = lax.broadcasted_iota(jnp.int32, ang_d.shape, 1)
    sin_d = jnp.sin(ang_d)
    cd_ref[...] = jnp.cos(ang_d)
    sd_ref[...] = jnp.where(lane < DIFF_HEAD_DIM // 2, -sin_d, sin_d)


def _rope_tables(positions):
    b, s = positions.shape
    rows = 1024
    n = b * s
    inv_r = ROPE_THETA ** (-jnp.arange(0, RET_QK_DIM, 2, dtype=F32) / RET_QK_DIM)
    inv_d = ROPE_THETA ** (-jnp.arange(0, DIFF_HEAD_DIM, 2, dtype=F32) / DIFF_HEAD_DIM)
    inv_d = jnp.concatenate([inv_d, inv_d])
    out = jax.ShapeDtypeStruct((n, LANES), F32)
    tab_spec = pl.BlockSpec((rows, LANES), lambda i: (i, 0))
    vec_spec = pl.BlockSpec((1, LANES), lambda i: (0, 0))
    outs = pl.pallas_call(
        _rope_kernel,
        grid=(n // rows,),
        in_specs=[pl.BlockSpec((rows, 1), lambda i: (i, 0)), vec_spec, vec_spec],
        out_specs=[tab_spec] * 4,
        out_shape=[out] * 4,
        name="rope_tables",
    )(positions.reshape(n, 1), inv_r.reshape(1, LANES), inv_d.reshape(1, LANES))
    return [t.reshape(b, s, LANES) for t in outs]


def _norm_mm_kernel(x_ref, g_ref, w_ref, o_ref, xn_ref, *, act):
    @pl.when(pl.program_id(1) == 0)
    def _():
        xn_ref[...] = _rms(x_ref[...], g_ref[...]).astype(BF16)

    y = jnp.dot(xn_ref[...], w_ref[...], preferred_element_type=F32)
    if act == "relu2":
        y = jnp.maximum(y, 0.0)
        y = y * y
    o_ref[...] = y.astype(o_ref.dtype)


def _norm_mm(x, gain, w, layer, *, act=None, tm=1024, tn=1024, name):
    m, k = x.shape
    n = w.shape[2]
    tm = min(tm, m)
    assert m % tm == 0 and n % tn == 0
    blocks = (_nbytes((tm, k), F32) + _nbytes((k, tn), BF16) + _nbytes((tm, tn), BF16)
              + _nbytes((1, k), F32))
    return pl.pallas_call(
        functools.partial(_norm_mm_kernel, act=act),
        grid=(m // tm, n // tn),
        in_specs=[
            pl.BlockSpec((tm, k), lambda i, j: (i, 0)),
            pl.BlockSpec((1, k), lambda i, j: (0, 0)),
            pl.BlockSpec((None, k, tn), lambda i, j: (layer, 0, j)),
        ],
        out_specs=pl.BlockSpec((tm, tn), lambda i, j: (i, j)),
        out_shape=jax.ShapeDtypeStruct((m, n), BF16),
        scratch_shapes=[pltpu.VMEM((tm, k), BF16)],
        compiler_params=pltpu.CompilerParams(
            dimension_semantics=("parallel", "arbitrary"),
            vmem_limit_bytes=_vmem_limit(blocks, _nbytes((tm, k), BF16))),
        name=name,
    )(x, gain.reshape(1, k), w)


def _mm_res_kernel(a_ref, w_ref, r_ref, o_ref, acc_ref, *, nk):
    k = pl.program_id(2)

    @pl.when(k == 0)
    def _():
        acc_ref[...] = r_ref[...]

    acc_ref[...] += jnp.dot(a_ref[...], w_ref[...], preferred_element_type=F32)

    @pl.when(k == nk - 1)
    def _():
        o_ref[...] = acc_ref[...]


def _mm_res_full_kernel(a_ref, w_ref, r_ref, o_ref):
    o_ref[...] = r_ref[...] + jnp.dot(a_ref[...], w_ref[...], preferred_element_type=F32)


def _mm_res_full(a, w, layer, res, *, tm=512, name):
    m, kdim = a.shape
    n = w.shape[2]
    assert m % tm == 0
    blocks = (_nbytes((tm, kdim), BF16) + _nbytes((kdim, n), BF16) + 2 * _nbytes((tm, n), F32))
    return pl.pallas_call(
        _mm_res_full_kernel,
        grid=(m // tm,),
        in_specs=[
            pl.BlockSpec((tm, kdim), lambda i: (i, 0)),
            pl.BlockSpec((None, kdim, n), lambda i: (layer, 0, 0)),
            pl.BlockSpec((tm, n), lambda i: (i, 0)),
        ],
        out_specs=pl.BlockSpec((tm, n), lambda i: (i, 0)),
        out_shape=jax.ShapeDtypeStruct((m, n), F32),
        compiler_params=pltpu.CompilerParams(
            dimension_semantics=("parallel",),
            vmem_limit_bytes=_vmem_limit(blocks)),
        name=name,
    )(a, w, res)


def _mm_res(a, w, layer, res, *, tm=1024, tn=1024, tk=2048, name):
    m, kdim = a.shape
    n = w.shape[2]
    tk = min(tk, kdim)
    assert m % tm == 0 and n % tn == 0 and kdim % tk == 0
    nk = kdim // tk
    blocks = (_nbytes((tm, tk), BF16) + _nbytes((tk, tn), BF16) + 2 * _nbytes((tm, tn), F32))
    return pl.pallas_call(
        functools.partial(_mm_res_kernel, nk=nk),
        grid=(m // tm, n // tn, nk),
        in_specs=[
            pl.BlockSpec((tm, tk), lambda i, j, k: (i, k)),
            pl.BlockSpec((None, tk, tn), lambda i, j, k: (layer, k, j)),
            pl.BlockSpec((tm, tn), lambda i, j, k: (i, j)),
        ],
        out_specs=pl.BlockSpec((tm, tn), lambda i, j, k: (i, j)),
        out_shape=jax.ShapeDtypeStruct((m, n), F32),
        scratch_shapes=[pltpu.VMEM((tm, tn), F32)],
        compiler_params=pltpu.CompilerParams(
            dimension_semantics=("parallel", "parallel", "arbitrary"),
            vmem_limit_bytes=_vmem_limit(blocks, _nbytes((tm, tn), F32))),
        name=name,
    )(a, w, res)


def _mm2_res_kernel(a1_ref, a2_ref, w1_ref, w2_ref, r_ref, o_ref):
    acc = r_ref[...] + jnp.dot(a1_ref[...], w1_ref[...], preferred_element_type=F32)
    o_ref[...] = acc + jnp.dot(a2_ref[...], w2_ref[...], preferred_element_type=F32)


def _mm2_res(a1, a2, w, layer, res, *, tm=1024, tn=1024, name):
    m, k1 = a1.shape
    k2 = a2.shape[1]
    n = w.shape[2]
    assert w.shape[1] == k1 + k2 and k1 % k2 == 0
    assert m % tm == 0 and n % tn == 0
    blocks = (_nbytes((tm, k1 + k2), BF16) + _nbytes((k1 + k2, tn), BF16)
              + 2 * _nbytes((tm, tn), F32))
    return pl.pallas_call(
        _mm2_res_kernel,
        grid=(m // tm, n // tn),
        in_specs=[
            pl.BlockSpec((tm, k1), lambda i, j: (i, 0)),
            pl.BlockSpec((tm, k2), lambda i, j: (i, 0)),
            pl.BlockSpec((None, k1, tn), lambda i, j: (layer, 0, j)),
            pl.BlockSpec((None, k2, tn), lambda i, j: (layer, k1 // k2, j)),
            pl.BlockSpec((tm, tn), lambda i, j: (i, j)),
        ],
        out_specs=pl.BlockSpec((tm, tn), lambda i, j: (i, j)),
        out_shape=jax.ShapeDtypeStruct((m, n), F32),
        compiler_params=pltpu.CompilerParams(
            dimension_semantics=("parallel", "parallel"),
            vmem_limit_bytes=_vmem_limit(blocks)),
        name=name,
    )(a1, a2, w, w, res)


def _ret_kernel(q_ref, k_ref, v_ref, g_ref, cos_ref, sin_ref, gain_ref, lg_ref, o_ref,
                state_ref, dmat_ref, qd_ref, kd_ref, *, blk):
    n = pl.program_id(2)
    lg = lg_ref[0][:, 0:1]

    @pl.when(n == 0)
    def _():
        state_ref[...] = jnp.zeros_like(state_ref)
        i = lax.broadcasted_iota(jnp.int32, (blk, blk), 0)
        j = lax.broadcasted_iota(jnp.int32, (blk, blk), 1)
        ci = i >> CHUNK_SHIFT
        cj = j >> CHUNK_SHIFT
        d = (i - j).astype(F32)
        expo = jnp.where(ci == cj, jnp.abs(d), d)
        dmat_ref[...] = jnp.where(ci >= cj, jnp.exp(lg * expo), 0.0)
        r = lax.broadcasted_iota(jnp.int32, (blk, RET_QK_DIM), 0).astype(F32)
        qd_ref[...] = jnp.exp(lg * r)
        kd_ref[...] = jnp.exp(lg * (blk - r))

    half = RET_QK_DIM // 2
    c = cos_ref[0]
    s = sin_ref[0]

    def rope(x):
        x1 = x[:, :half]
        x2 = x[:, half:]
        return jnp.concatenate([x1 * c - x2 * s, x2 * c + x1 * s], axis=-1)

    q = rope(q_ref[0].astype(F32))
    k = rope(k_ref[0].astype(F32)) * (RET_QK_DIM ** -0.5)
    v = v_ref[0]
    state = state_ref[...]

    scores = lax.dot_general(q.astype(BF16), k.astype(BF16), (((1,), (1,)), ((), ())),
                             preferred_element_type=F32) * dmat_ref[...]
    out = jnp.dot(scores.astype(BF16), v, preferred_element_type=F32)
    out = out + jnp.dot((q * qd_ref[...]).astype(BF16), state.astype(BF16),
                        preferred_element_type=F32)
    kv = lax.dot_general((k * kd_ref[...]).astype(BF16), v, (((0,), (0,)), ((), ())),
                         preferred_element_type=F32)
    state_ref[...] = state * jnp.exp(lg * blk) + kv

    y = _rms(out, gain_ref[0])
    g = g_ref[0].astype(F32)
    o_ref[0] = (g / (1.0 + jnp.exp(-g)) * y).astype(o_ref.dtype)


def _retention(proj, cos_r, sin_r, ret_gain, *, blk=512):
    b, s, _ = proj.shape
    h = RET_HEADS
    dk, dv = RET_QK_DIM, RET_V_DIM
    v_off = 2 * h * dk // dv
    g_off = v_off + h
    log_g = jnp.log(1.0 - 2.0 ** (-5.0 - jnp.arange(h, dtype=F32)))
    log_g = jnp.broadcast_to(log_g[:, None, None], (h, 1, LANES))
    blocks = (2 * _nbytes((blk, dk), BF16) + 3 * _nbytes((blk, dv), BF16)
              + 2 * _nbytes((blk, LANES), F32))
    scratch = (_nbytes((dk, dv), F32) + _nbytes((blk, blk), F32) + 2 * _nbytes((blk, dk), F32))
    return pl.pallas_call(
        functools.partial(_ret_kernel, blk=blk),
        grid=(b, h, s // blk),
        in_specs=[
            pl.BlockSpec((1, blk, dk), lambda bi, hi, n: (bi, n, hi)),
            pl.BlockSpec((1, blk, dk), lambda bi, hi, n: (bi, n, h + hi)),
            pl.BlockSpec((1, blk, dv), lambda bi, hi, n: (bi, n, v_off + hi)),
            pl.BlockSpec((1, blk, dv), lambda bi, hi, n: (bi, n, g_off + hi)),
            pl.BlockSpec((1, blk, LANES), lambda bi, hi, n: (bi, n, 0)),
            pl.BlockSpec((1, blk, LANES), lambda bi, hi, n: (bi, n, 0)),
            pl.BlockSpec((1, 1, dv), lambda bi, hi, n: (hi, 0, 0)),
            pl.BlockSpec((1, 1, LANES), lambda bi, hi, n: (hi, 0, 0)),
        ],
        out_specs=pl.BlockSpec((1, blk, dv), lambda bi, hi, n: (bi, n, hi)),
        out_shape=jax.ShapeDtypeStruct((b, s, h * dv), BF16),
        scratch_shapes=[
            pltpu.VMEM((dk, dv), F32),
            pltpu.VMEM((blk, blk), F32),
            pltpu.VMEM((blk, dk), F32),
            pltpu.VMEM((blk, dk), F32),
        ],
        compiler_params=pltpu.CompilerParams(
            dimension_semantics=("parallel", "parallel", "arbitrary"),
            vmem_limit_bytes=_vmem_limit(blocks, scratch)),
        name="retention",
    )(proj, proj, proj, proj, cos_r, sin_r, ret_gain.reshape(h, 1, dv), log_g)


def _sg_kernel(u_ref, v_ref, gain_ref, w_ref, b_ref, o_ref, *, rows):
    i = lax.broadcasted_iota(jnp.int32, (SG_BLOCK, SG_BLOCK), 0)
    j = lax.broadcasted_iota(jnp.int32, (SG_BLOCK, SG_BLOCK), 1)
    mask = (j >> CHUNK_SHIFT) <= (i >> CHUNK_SHIFT)
    for g in range(SG_GROUPS):
        cols = slice(g * SG_DIM, (g + 1) * SG_DIM)
        w = jnp.where(mask, w_ref[g], 0.0).astype(BF16)
        bias = b_ref[g]
        u = _gelu_tanh(u_ref[0, :, cols].astype(F32))
        v = _rms(_gelu_tanh(v_ref[0, :, cols].astype(F32)), gain_ref[:, cols]).astype(BF16)
        for n in range(rows // SG_BLOCK):
            r = slice(n * SG_BLOCK, (n + 1) * SG_BLOCK)
            mixed = jnp.dot(w, v[r], preferred_element_type=F32) + bias
            o_ref[0, r, cols] = (u[r] * mixed).astype(o_ref.dtype)


def _spatial_gate(proj, sg_norm, sg_w, sg_b, *, rows=512):
    b, s, width = proj.shape
    sgw = SG_GROUPS * SG_DIM
    u_blk = (width - 2 * sgw) // sgw
    blocks = 3 * _nbytes((rows, sgw), BF16) + _nbytes(sg_w.shape, F32)
    return pl.pallas_call(
        functools.partial(_sg_kernel, rows=rows),
        grid=(b, s // rows),
        in_specs=[
            pl.BlockSpec((1, rows, sgw), lambda bi, n: (bi, n, u_blk)),
            pl.BlockSpec((1, rows, sgw), lambda bi, n: (bi, n, u_blk + 1)),
            pl.BlockSpec((1, sgw), lambda bi, n: (0, 0)),
            pl.BlockSpec((SG_GROUPS, SG_BLOCK, SG_BLOCK), lambda bi, n: (0, 0, 0)),
            pl.BlockSpec((SG_GROUPS, SG_BLOCK, 1), lambda bi, n: (0, 0, 0)),
        ],
        out_specs=pl.BlockSpec((1, rows, sgw), lambda bi, n: (bi, n, 0)),
        out_shape=jax.ShapeDtypeStruct((b, s, sgw), BF16),
        compiler_params=pltpu.CompilerParams(
            dimension_semantics=("parallel", "parallel"),
            vmem_limit_bytes=_vmem_limit(blocks)),
        name="spatial_gate",
    )(proj, proj, sg_norm.reshape(1, sgw), sg_w, sg_b.reshape(SG_GROUPS, SG_BLOCK, 1))


def _diff_attn_kernel(q0_ref, k0_ref, cos0_ref, sin0_ref, q1_ref, k1_ref, cos1_ref, sin1_ref,
                      v_ref, qg_ref, kg_ref, sg_ref, lam_ref, o_ref,
                      qp_ref, kn_ref, sa_ref, sb_ref, m_ref, l_ref, acc_ref,
                      *, tq, nq, lambda_init):
    qi = pl.program_id(2)
    d = DIFF_HEAD_DIM
    n_lane_tiles = tq // LANES
    prep_rows = min(tq, 256)
    q_scale = d ** -0.5 * math.log2(math.e)

    def prepare(q_ref, k_ref, cos_ref, sin_ref, tile):
        ii = lax.broadcasted_iota(jnp.int32, (d, d), 0)
        jj = lax.broadcasted_iota(jnp.int32, (d, d), 1)
        swap_halves = jnp.where((ii ^ (d // 2)) == jj, 1.0, 0.0).astype(BF16)
        ones = jnp.ones((d, d), BF16)

        def norm_rope(x, gain_cos, gain_sin):
            xf = x.astype(F32)
            sq = xf * xf
            hi = sq.astype(BF16)
            lo = (sq - hi.astype(F32)).astype(BF16)
            ssq = (jnp.dot(hi, ones, preferred_element_type=F32)
                   + jnp.dot(lo, ones, preferred_element_type=F32))
            rot = jnp.dot(x, swap_halves, preferred_element_type=F32)
            return lax.rsqrt(ssq * (1.0 / d) + NORM_EPS) * (xf * gain_cos + rot * gain_sin)

        def rolled(g):
            return pltpu.roll(jnp.broadcast_to(g, (8, d)), d // 2, 1)[0:1]

        qg = qg_ref[...] * q_scale
        kg = kg_ref[...]
        qg_rot = rolled(qg)
        kg_rot = rolled(kg)
        base = pl.multiple_of(tile * tq, tq)
        for r0 in range(0, tq, prep_rows):
            rows = slice(r0, r0 + prep_rows)
            c = cos_ref[0, rows, :]
            s = sin_ref[0, rows, :]
            qc, qs = qg * c, qg_rot * s
            kc, ks = kg * c, kg_rot * s
            qp_ref[rows, :] = jnp.concatenate(
                [norm_rope(q_ref[0, rows, :d], qc, qs), norm_rope(q_ref[0, rows, d:], qc, qs)],
                axis=-1).astype(BF16)
            kn_ref[pl.ds(base + r0, prep_rows), :] = jnp.concatenate(
                [norm_rope(k_ref[0, rows, :d], kc, ks), norm_rope(k_ref[0, rows, d:], kc, ks)],
                axis=-1).astype(BF16)

    def scores(s_ref, slot, kv, masked):
        k0 = pl.multiple_of(kv * tq, tq)
        kt = kn_ref[pl.ds(k0, tq), :]
        for mi in range(2):
            lanes = slice(mi * d, (mi + 1) * d)
            s = lax.dot_general(qp_ref[:, lanes], kt[:, lanes], (((1,), (1,)), ((), ())),
                                preferred_element_type=F32)
            if masked:
                row = lax.broadcasted_iota(jnp.int32, (tq, tq), 0)
                col = lax.broadcasted_iota(jnp.int32, (tq, tq), 1)
                s = jnp.where((col >> CHUNK_SHIFT) <= (row >> CHUNK_SHIFT), s, MASK_VALUE)
            s_ref[mi, kv] = s
            m = m_ref[slot, mi]
            for c in range(n_lane_tiles):
                m = jnp.maximum(m, s[:, c * LANES:(c + 1) * LANES])
            m_ref[slot, mi] = m

    def finish_max(slot):
        for mi in range(2):
            m_row = jnp.max(m_ref[slot, mi], axis=-1, keepdims=True)
            m_ref[slot, mi] = jnp.broadcast_to(m_row, (tq, LANES))

    def weighted_values(s_ref, slot, kv):
        k0 = pl.multiple_of(kv * tq, tq)
        vt = v_ref[0, pl.ds(k0, tq), :]
        for mi in range(2):
            m = m_ref[slot, mi]
            l = l_ref[mi]
            ps = []
            for c in range(n_lane_tiles):
                p = jnp.exp2(s_ref[mi, kv, :, c * LANES:(c + 1) * LANES] - m)
                l = l + p
                ps.append(p.astype(BF16))
            l_ref[mi] = l
            acc_ref[mi] += jnp.dot(jnp.concatenate(ps, axis=-1), vt,
                                   preferred_element_type=F32)

    def run(cur_ref, nxt_ref, slot):
        nslot = 1 - slot
        l_ref[...] = jnp.zeros_like(l_ref)
        acc_ref[...] = jnp.zeros_like(acc_ref)

        @pl.when(qi < nq - 1)
        def _():
            prepare(q1_ref, k1_ref, cos1_ref, sin1_ref, qi + 1)
            m_ref[nslot] = jnp.full(m_ref.shape[1:], MASK_VALUE, F32)

            def body(kv, carry):
                weighted_values(cur_ref, slot, kv)
                scores(nxt_ref, nslot, kv, False)
                return carry

            lax.fori_loop(0, qi + 1, body, 0)
            scores(nxt_ref, nslot, qi + 1, True)
            finish_max(nslot)

        @pl.when(qi == nq - 1)
        def _():
            def body(kv, carry):
                weighted_values(cur_ref, slot, kv)
                return carry

            lax.fori_loop(0, qi + 1, body, 0)

    @pl.when(qi == 0)
    def _():
        prepare(q0_ref, k0_ref, cos0_ref, sin0_ref, 0)
        m_ref[0] = jnp.full(m_ref.shape[1:], MASK_VALUE, F32)
        scores(sa_ref, 0, 0, True)
        finish_max(0)

    @pl.when((qi & 1) == 0)
    def _():
        run(sa_ref, sb_ref, 0)

    @pl.when((qi & 1) == 1)
    def _():
        run(sb_ref, sa_ref, 1)

    lam_p = lam_ref[...]
    lam = (jnp.exp(jnp.sum(lam_p[0:1] * lam_p[1:2], axis=-1, keepdims=True))
           - jnp.exp(jnp.sum(lam_p[2:3] * lam_p[3:4], axis=-1, keepdims=True))
           + lambda_init)
    l0 = jnp.sum(l_ref[0], axis=-1, keepdims=True)
    l1 = jnp.sum(l_ref[1], axis=-1, keepdims=True)
    out = acc_ref[0] / l0 - lam * (acc_ref[1] / l1)
    o_ref[0] = (_rms(out, sg_ref[...]) * (1.0 - lambda_init)).astype(o_ref.dtype)


def _diff_attention(proj, cos_d, sin_d, q_gain, k_gain, sub_gain, lam_params, lambda_init,
                    *, tq=512):
    b, s, _ = proj.shape
    h = DIFF_HEADS
    d = DIFF_HEAD_DIM
    hw = 2 * d
    nq = s // tq
    blocks = (5 * _nbytes((tq, hw), BF16) + 4 * _nbytes((tq, d), F32) + _nbytes((s, hw), BF16))
    scratch = (_nbytes((tq, hw), BF16) + _nbytes((s, hw), BF16) + 2 * _nbytes((2, tq, s), F32)
               + _nbytes((2, 2, tq, LANES), F32) + _nbytes((2, tq, LANES), F32)
               + _nbytes((2, tq, hw), F32))
    first = lambda col: (lambda bi, hi, qi: (bi, 0, col(hi)))
    nxt = lambda col: (lambda bi, hi, qi: (bi, jnp.minimum(qi + 1, nq - 1), col(hi)))
    q_col = lambda hi: hi
    k_col = lambda hi: h + hi
    rope_col = lambda hi: 0
    return pl.pallas_call(
        functools.partial(_diff_attn_kernel, tq=tq, nq=nq, lambda_init=lambda_init),
        grid=(b, h, nq),
        in_specs=[
            pl.BlockSpec((1, tq, hw), first(q_col)),
            pl.BlockSpec((1, tq, hw), first(k_col)),
            pl.BlockSpec((1, tq, d), first(rope_col)),
            pl.BlockSpec((1, tq, d), first(rope_col)),
            pl.BlockSpec((1, tq, hw), nxt(q_col)),
            pl.BlockSpec((1, tq, hw), nxt(k_col)),
            pl.BlockSpec((1, tq, d), nxt(rope_col)),
            pl.BlockSpec((1, tq, d), nxt(rope_col)),
            pl.BlockSpec((1, s, hw), lambda bi, hi, qi: (bi, 0, 2 * h + hi)),
            pl.BlockSpec((1, d), lambda bi, hi, qi: (0, 0)),
            pl.BlockSpec((1, d), lambda bi, hi, qi: (0, 0)),
            pl.BlockSpec((1, hw), lambda bi, hi, qi: (0, 0)),
            pl.BlockSpec((4, d), lambda bi, hi, qi: (0, 0)),
        ],
        out_specs=pl.BlockSpec((1, tq, hw), lambda bi, hi, qi: (bi, qi, hi)),
        out_shape=jax.ShapeDtypeStruct((b, s, h * hw), BF16),
        scratch_shapes=[
            pltpu.VMEM((tq, hw), BF16),
            pltpu.VMEM((s, hw), BF16),
            pltpu.VMEM((2, nq, tq, tq), F32),
            pltpu.VMEM((2, nq, tq, tq), F32),
            pltpu.VMEM((2, 2, tq, LANES), F32),
            pltpu.VMEM((2, tq, LANES), F32),
            pltpu.VMEM((2, tq, hw), F32),
        ],
        compiler_params=pltpu.CompilerParams(
            dimension_semantics=("parallel", "parallel", "arbitrary"),
            vmem_limit_bytes=_vmem_limit(blocks, scratch)),
        name="diff_attention",
    )(proj, proj, cos_d, sin_d, proj, proj, cos_d, sin_d, proj,
      q_gain.reshape(1, d), k_gain.reshape(1, d), sub_gain.reshape(1, hw), lam_params)


def _xa_kernel(q_ref, kv_ref, qg_ref, kg_ref, o_ref, kn_ref, *, heads):
    dm = q_ref.shape[2]
    hd = dm // heads

    @pl.when(pl.program_id(1) == 0)
    def _():
        for h in range(heads):
            cols = slice(h * hd, (h + 1) * hd)
            kn_ref[:, cols] = _rms(kv_ref[0, :, cols].astype(F32), kg_ref[...]).astype(BF16)

    scale = hd ** -0.5
    for h in range(heads):
        cols = slice(h * hd, (h + 1) * hd)
        qn = (_rms(q_ref[0, :, cols].astype(F32), qg_ref[...]) * scale).astype(BF16)
        s = lax.dot_general(qn, kn_ref[:, cols], (((1,), (1,)), ((), ())),
                            preferred_element_type=F32)
        e = jnp.exp(s - jnp.max(s, axis=-1, keepdims=True))
        p = e / jnp.sum(e, axis=-1, keepdims=True)
        v = kv_ref[0, :, dm + h * hd: dm + (h + 1) * hd]
        o_ref[0, :, cols] = jnp.dot(p.astype(BF16), v,
                                    preferred_element_type=F32).astype(o_ref.dtype)


def _cross_attention(q, kv, q_gain, k_gain, *, tq=512):
    b, s, dm = q.shape
    mlen = kv.shape[1]
    hd = dm // XA_HEADS
    blocks = 2 * _nbytes((tq, dm), BF16) + _nbytes((mlen, 2 * dm), BF16)
    return pl.pallas_call(
        functools.partial(_xa_kernel, heads=XA_HEADS),
        grid=(b, s // tq),
        in_specs=[
            pl.BlockSpec((1, tq, dm), lambda bi, qi: (bi, qi, 0)),
            pl.BlockSpec((1, mlen, 2 * dm), lambda bi, qi: (bi, 0, 0)),
            pl.BlockSpec((1, hd), lambda bi, qi: (0, 0)),
            pl.BlockSpec((1, hd), lambda bi, qi: (0, 0)),
        ],
        out_specs=pl.BlockSpec((1, tq, dm), lambda bi, qi: (bi, qi, 0)),
        out_shape=jax.ShapeDtypeStruct((b, s, dm), BF16),
        scratch_shapes=[pltpu.VMEM((mlen, dm), BF16)],
        compiler_params=pltpu.CompilerParams(
            dimension_semantics=("parallel", "arbitrary"),
            vmem_limit_bytes=_vmem_limit(blocks, _nbytes((mlen, dm), BF16))),
        name="cross_attention",
    )(q, kv, q_gain.reshape(1, hd), k_gain.reshape(1, hd))


def kernel(x, mem, positions, norm_mix, norm_xa, norm_mem, norm_ffn, ev_w_in, ev_ret_gain,
           ev_sg_norm, ev_sg_w, ev_sg_b, ev_w_out, od_w_qkv, od_q_gain, od_k_gain, od_lam_q1,
           od_lam_k1, od_lam_q2, od_lam_k2, od_sub_gain, od_w_o, xa_w_q, xa_w_kv, xa_q_gain,
           xa_k_gain, xa_w_o, ffn_w1, ffn_w2):
    b, s, dm = x.shape
    mlen = mem.shape[1]
    depth = norm_mix.shape[0]
    m = b * s
    ev_w_in, ev_w_out, od_w_qkv, od_w_o, xa_w_q, xa_w_kv, xa_w_o, ffn_w1, ffn_w2 = (
        w.astype(BF16) for w in (ev_w_in, ev_w_out, od_w_qkv, od_w_o, xa_w_q, xa_w_kv, xa_w_o,
                                 ffn_w1, ffn_w2))

    cos_r, sin_r, cos_d, sin_d = _rope_tables(positions)
    xf = x.reshape(m, dm)
    mem_f = mem.reshape(b * mlen, dm)

    for li in range(depth):
        if li % 2 == 0:
            e = li // 2
            proj = _norm_mm(xf, norm_mix[li], ev_w_in, e, name="even_in_proj")
            proj = proj.reshape(b, s, -1)
            out_a = _retention(proj, cos_r, sin_r, ev_ret_gain[e])
            out_b = _spatial_gate(proj, ev_sg_norm[e], ev_sg_w[e], ev_sg_b[e])
            xf = _mm2_res(out_a.reshape(m, -1), out_b.reshape(m, -1), ev_w_out, e, xf,
                          name="even_out_proj")
        else:
            o = li // 2
            lambda_init = 0.8 - 0.6 * math.exp(-0.3 * li)
            proj = _norm_mm(xf, norm_mix[li], od_w_qkv, o, name="diff_qkv_proj")
            lam_params = jnp.stack([od_lam_q1[o], od_lam_k1[o], od_lam_q2[o], od_lam_k2[o]])
            att = _diff_attention(proj.reshape(b, s, -1), cos_d, sin_d, od_q_gain[o],
                                  od_k_gain[o], od_sub_gain[o], lam_params, lambda_init)
            xf = _mm_res_full(att.reshape(m, -1), od_w_o, o, xf, name="diff_out_proj")

        q = _norm_mm(xf, norm_xa[li], xa_w_q, li, name="xa_q_proj")
        kv = _norm_mm(mem_f, norm_mem[li], xa_w_kv, li, name="xa_kv_proj")
        att = _cross_attention(q.reshape(b, s, dm), kv.reshape(b, mlen, 2 * dm),
                               xa_q_gain[li], xa_k_gain[li])
        xf = _mm_res_full(att.reshape(m, dm), xa_w_o, li, xf, name="xa_out_proj")

        hid = _norm_mm(xf, norm_ffn[li], ffn_w1, li, act="relu2", name="ffn_up")
        xf = _mm_res(hid, ffn_w2, li, xf, name="ffn_down")

    return xf.reshape(b, s, dm)
```

```python
import functools
import math

import jax
import jax.numpy as jnp
from jax import lax
from jax.experimental import pallas as pl
from jax.experimental.pallas import tpu as pltpu

F32 = jnp.float32
BF16 = jnp.bfloat16

CHUNK = 64
CHUNK_SHIFT = CHUNK.bit_length() - 1
ROPE_THETA = 10000.0
NORM_EPS = 1e-6
RET_HEADS = 4
RET_QK_DIM = 256
RET_V_DIM = 512
SG_GROUPS = 4
SG_DIM = 256
SG_BLOCK = 128
DIFF_HEADS = 8
DIFF_HEAD_DIM = 128
XA_HEADS = 4
MASK_VALUE = -1e30

V7X_VMEM_BYTES = 64 * 1024 * 1024
VMEM_REQUEST_CAP = V7X_VMEM_BYTES - 8 * 1024 * 1024
LANES = 128


def _vmem_limit(block_bytes, scratch_bytes=0):
    need = 2 * block_bytes + scratch_bytes
    return int(min(VMEM_REQUEST_CAP, max(32 * 1024 * 1024, 2 * need)))


def _nbytes(shape, dtype):
    return math.prod(shape) * jnp.dtype(dtype).itemsize


def _rms(x, gain):
    ms = jnp.mean(x * x, axis=-1, keepdims=True)
    return x * lax.rsqrt(ms + NORM_EPS) * gain


def _gelu_tanh(x):
    c = math.sqrt(2.0 / math.pi)
    return 0.5 * x * (1.0 + jnp.tanh(c * (x + 0.044715 * (x * x * x))))


def _rope_kernel(pos_ref, inv_r_ref, inv_d_ref, cr_ref, sr_ref, cd_ref, sd_ref):
    pos = pos_ref[...].astype(F32)
    ang_r = pos * inv_r_ref[...]
    cr_ref[...] = jnp.cos(ang_r)
    sr_ref[...] = jnp.sin(ang_r)
    ang_d = pos * inv_d_ref[...]
    lane = lax.broadcasted_iota(jnp.int32, ang_d.shape, 1)
    sin_d = jnp.sin(ang_d)
    cd_ref[...] = jnp.cos(ang_d)
    sd_ref[...] = jnp.where(lane < DIFF_HEAD_DIM // 2, -sin_d, sin_d)


def _rope_tables(positions):
    b, s = positions.shape
    rows = 1024
    n = b * s
    inv_r = ROPE_THETA ** (-jnp.arange(0, RET_QK_DIM, 2, dtype=F32) / RET_QK_DIM)
    inv_d = ROPE_THETA ** (-jnp.arange(0, DIFF_HEAD_DIM, 2, dtype=F32) / DIFF_HEAD_DIM)
    inv_d = jnp.concatenate([inv_d, inv_d])
    out = jax.ShapeDtypeStruct((n, LANES), F32)
    tab_spec = pl.BlockSpec((rows, LANES), lambda i: (i, 0))
    vec_spec = pl.BlockSpec((1, LANES), lambda i: (0, 0))
    outs = pl.pallas_call(
        _rope_kernel,
        grid=(n // rows,),
        in_specs=[pl.BlockSpec((rows, 1), lambda i: (i, 0)), vec_spec, vec_spec],
        out_specs=[tab_spec] * 4,
        out_shape=[out] * 4,
        name="rope_tables",
    )(positions.reshape(n, 1), inv_r.reshape(1, LANES), inv_d.reshape(1, LANES))
    return [t.reshape(b, s, LANES) for t in outs]


def _norm_mm_kernel(x_ref, g_ref, w_ref, o_ref, xg_ref, r_ref, *, act):
    def finish(y):
        r = r_ref[...]
        for c in range(y.shape[1] // LANES):
            lanes = slice(c * LANES, (c + 1) * LANES)
            z = y[:, lanes] * r
            if act == "relu2":
                z = jnp.maximum(z, 0.0)
                z = z * z
            o_ref[:, lanes] = z.astype(o_ref.dtype)

    @pl.when(pl.program_id(1) == 0)
    def _():
        x = x_ref[...]
        xg = (x * g_ref[...]).astype(BF16)
        xg_ref[...] = xg
        ms = jnp.mean(x * x, axis=-1, keepdims=True)
        r_ref[...] = jnp.broadcast_to(lax.rsqrt(ms + NORM_EPS), r_ref.shape)
        finish(jnp.dot(xg, w_ref[...], preferred_element_type=F32))

    @pl.when(pl.program_id(1) != 0)
    def _():
        finish(jnp.dot(xg_ref[...], w_ref[...], preferred_element_type=F32))


def _norm_mm(x, gain, w, layer, *, act=None, tm=1024, tn=1024, name):
    m, k = x.shape
    n = w.shape[2]
    tm = min(tm, m)
    assert m % tm == 0 and n % tn == 0
    blocks = (_nbytes((tm, k), F32) + _nbytes((k, tn), BF16) + _nbytes((tm, tn), BF16)
              + _nbytes((1, k), F32))
    return pl.pallas_call(
        functools.partial(_norm_mm_kernel, act=act),
        grid=(m // tm, n // tn),
        in_specs=[
            pl.BlockSpec((tm, k), lambda i, j: (i, 0)),
            pl.BlockSpec((1, k), lambda i, j: (0, 0)),
            pl.BlockSpec((None, k, tn), lambda i, j: (layer, 0, j)),
        ],
        out_specs=pl.BlockSpec((tm, tn), lambda i, j: (i, j)),
        out_shape=jax.ShapeDtypeStruct((m, n), BF16),
        scratch_shapes=[pltpu.VMEM((tm, k), BF16), pltpu.VMEM((tm, LANES), F32)],
        compiler_params=pltpu.CompilerParams(
            dimension_semantics=("parallel", "arbitrary"),
            vmem_limit_bytes=_vmem_limit(blocks, _nbytes((tm, k), BF16))),
        name=name,
    )(x, gain.reshape(1, k), w)


def _mm_res_kernel(a_ref, w_ref, r_ref, o_ref, acc_ref, *, nk):
    k = pl.program_id(2)

    @pl.when(k == 0)
    def _():
        acc_ref[...] = r_ref[...]

    acc_ref[...] += jnp.dot(a_ref[...], w_ref[...], preferred_element_type=F32)

    @pl.when(k == nk - 1)
    def _():
        o_ref[...] = acc_ref[...]


def _mm_res_full_kernel(a_ref, w_ref, r_ref, o_ref):
    o_ref[...] = r_ref[...] + jnp.dot(a_ref[...], w_ref[...], preferred_element_type=F32)


def _mm_res_full(a, w, layer, res, *, tm=512, name):
    m, kdim = a.shape
    n = w.shape[2]
    assert m % tm == 0
    blocks = (_nbytes((tm, kdim), BF16) + _nbytes((kdim, n), BF16) + 2 * _nbytes((tm, n), F32))
    return pl.pallas_call(
        _mm_res_full_kernel,
        grid=(m // tm,),
        in_specs=[
            pl.BlockSpec((tm, kdim), lambda i: (i, 0)),
            pl.BlockSpec((None, kdim, n), lambda i: (layer, 0, 0)),
            pl.BlockSpec((tm, n), lambda i: (i, 0)),
        ],
        out_specs=pl.BlockSpec((tm, n), lambda i: (i, 0)),
        out_shape=jax.ShapeDtypeStruct((m, n), F32),
        compiler_params=pltpu.CompilerParams(
            dimension_semantics=("parallel",),
            vmem_limit_bytes=_vmem_limit(blocks)),
        name=name,
    )(a, w, res)


def _mm_res(a, w, layer, res, *, tm=1024, tn=1024, tk=2048, name):
    m, kdim = a.shape
    n = w.shape[2]
    tk = min(tk, kdim)
    assert m % tm == 0 and n % tn == 0 and kdim % tk == 0
    nk = kdim // tk
    blocks = (_nbytes((tm, tk), BF16) + _nbytes((tk, tn), BF16) + 2 * _nbytes((tm, tn), F32))
    return pl.pallas_call(
        functools.partial(_mm_res_kernel, nk=nk),
        grid=(m // tm, n // tn, nk),
        in_specs=[
            pl.BlockSpec((tm, tk), lambda i, j, k: (i, k)),
            pl.BlockSpec((None, tk, tn), lambda i, j, k: (layer, k, j)),
            pl.BlockSpec((tm, tn), lambda i, j, k: (i, j)),
        ],
        out_specs=pl.BlockSpec((tm, tn), lambda i, j, k: (i, j)),
        out_shape=jax.ShapeDtypeStruct((m, n), F32),
        scratch_shapes=[pltpu.VMEM((tm, tn), F32)],
        compiler_params=pltpu.CompilerParams(
            dimension_semantics=("parallel", "parallel", "arbitrary"),
            vmem_limit_bytes=_vmem_limit(blocks, _nbytes((tm, tn), F32))),
        name=name,
    )(a, w, res)


def _mm2_res_kernel(a1_ref, a2_ref, w1_ref, w2_ref, r_ref, o_ref):
    acc = r_ref[...] + jnp.dot(a1_ref[...], w1_ref[...], preferred_element_type=F32)
    o_ref[...] = acc + jnp.dot(a2_ref[...], w2_ref[...], preferred_element_type=F32)


def _mm2_res(a1, a2, w, layer, res, *, tm=1024, tn=1024, name):
    m, k1 = a1.shape
    k2 = a2.shape[1]
    n = w.shape[2]
    assert w.shape[1] == k1 + k2 and k1 % k2 == 0
    assert m % tm == 0 and n % tn == 0
    blocks = (_nbytes((tm, k1 + k2), BF16) + _nbytes((k1 + k2, tn), BF16)
              + 2 * _nbytes((tm, tn), F32))
    return pl.pallas_call(
        _mm2_res_kernel,
        grid=(m // tm, n // tn),
        in_specs=[
            pl.BlockSpec((tm, k1), lambda i, j: (i, 0)),
            pl.BlockSpec((tm, k2), lambda i, j: (i, 0)),
            pl.BlockSpec((None, k1, tn), lambda i, j: (layer, 0, j)),
            pl.BlockSpec((None, k2, tn), lambda i, j: (layer, k1 // k2, j)),
            pl.BlockSpec((tm, tn), lambda i, j: (i, j)),
        ],
        out_specs=pl.BlockSpec((tm, tn), lambda i, j: (i, j)),
        out_shape=jax.ShapeDtypeStruct((m, n), F32),
        compiler_params=pltpu.CompilerParams(
            dimension_semantics=("parallel", "parallel"),
            vmem_limit_bytes=_vmem_limit(blocks)),
        name=name,
    )(a1, a2, w, w, res)


def _ret_kernel(q_ref, k_ref, v_ref, g_ref, cos_ref, sin_ref, gain_ref, lg_ref, o_ref,
                state_ref, dmat_ref, qd_ref, kd_ref, *, blk):
    n = pl.program_id(2)
    lg = lg_ref[0][:, 0:1]

    @pl.when(n == 0)
    def _():
        state_ref[...] = jnp.zeros_like(state_ref)
        i = lax.broadcasted_iota(jnp.int32, (blk, blk), 0)
        j = lax.broadcasted_iota(jnp.int32, (blk, blk), 1)
        ci = i >> CHUNK_SHIFT
        cj = j >> CHUNK_SHIFT
        d = (i - j).astype(F32)
        expo = jnp.where(ci == cj, jnp.abs(d), d)
        dmat_ref[...] = jnp.where(ci >= cj, jnp.exp(lg * expo), 0.0)
        r = lax.broadcasted_iota(jnp.int32, (blk, RET_QK_DIM), 0).astype(F32)
        qd_ref[...] = jnp.exp(lg * r)
        kd_ref[...] = jnp.exp(lg * (blk - r))

    half = RET_QK_DIM // 2
    c = cos_ref[0]
    s = sin_ref[0]

    def rope(x):
        x1 = x[:, :half]
        x2 = x[:, half:]
        return jnp.concatenate([x1 * c - x2 * s, x2 * c + x1 * s], axis=-1)

    q = rope(q_ref[0].astype(F32))
    k = rope(k_ref[0].astype(F32)) * (RET_QK_DIM ** -0.5)
    v = v_ref[0]
    state = state_ref[...]

    scores = lax.dot_general(q.astype(BF16), k.astype(BF16), (((1,), (1,)), ((), ())),
                             preferred_element_type=F32) * dmat_ref[...]
    out = jnp.dot(scores.astype(BF16), v, preferred_element_type=F32)
    out = out + jnp.dot((q * qd_ref[...]).astype(BF16), state.astype(BF16),
                        preferred_element_type=F32)
    kv = lax.dot_general((k * kd_ref[...]).astype(BF16), v, (((0,), (0,)), ((), ())),
                         preferred_element_type=F32)
    state_ref[...] = state * jnp.exp(lg * blk) + kv

    y = _rms(out, gain_ref[0])
    g = g_ref[0].astype(F32)
    o_ref[0] = (g / (1.0 + jnp.exp(-g)) * y).astype(o_ref.dtype)


def _retention(proj, cos_r, sin_r, ret_gain, *, blk=512):
    b, s, _ = proj.shape
    h = RET_HEADS
    dk, dv = RET_QK_DIM, RET_V_DIM
    v_off = 2 * h * dk // dv
    g_off = v_off + h
    log_g = jnp.log(1.0 - 2.0 ** (-5.0 - jnp.arange(h, dtype=F32)))
    log_g = jnp.broadcast_to(log_g[:, None, None], (h, 1, LANES))
    blocks = (2 * _nbytes((blk, dk), BF16) + 3 * _nbytes((blk, dv), BF16)
              + 2 * _nbytes((blk, LANES), F32))
    scratch = (_nbytes((dk, dv), F32) + _nbytes((blk, blk), F32) + 2 * _nbytes((blk, dk), F32))
    return pl.pallas_call(
        functools.partial(_ret_kernel, blk=blk),
        grid=(b, h, s // blk),
        in_specs=[
            pl.BlockSpec((1, blk, dk), lambda bi, hi, n: (bi, n, hi)),
            pl.BlockSpec((1, blk, dk), lambda bi, hi, n: (bi, n, h + hi)),
            pl.BlockSpec((1, blk, dv), lambda bi, hi, n: (bi, n, v_off + hi)),
            pl.BlockSpec((1, blk, dv), lambda bi, hi, n: (bi, n, g_off + hi)),
            pl.BlockSpec((1, blk, LANES), lambda bi, hi, n: (bi, n, 0)),
            pl.BlockSpec((1, blk, LANES), lambda bi, hi, n: (bi, n, 0)),
            pl.BlockSpec((1, 1, dv), lambda bi, hi, n: (hi, 0, 0)),
            pl.BlockSpec((1, 1, LANES), lambda bi, hi, n: (hi, 0, 0)),
        ],
        out_specs=pl.BlockSpec((1, blk, dv), lambda bi, hi, n: (bi, n, hi)),
        out_shape=jax.ShapeDtypeStruct((b, s, h * dv), BF16),
        scratch_shapes=[
            pltpu.VMEM((dk, dv), F32),
            pltpu.VMEM((blk, blk), F32),
            pltpu.VMEM((blk, dk), F32),
            pltpu.VMEM((blk, dk), F32),
        ],
        compiler_params=pltpu.CompilerParams(
            dimension_semantics=("parallel", "parallel", "arbitrary"),
            vmem_limit_bytes=_vmem_limit(blocks, scratch)),
        name="retention",
    )(proj, proj, proj, proj, cos_r, sin_r, ret_gain.reshape(h, 1, dv), log_g)


def _sg_kernel(u_ref, v_ref, gain_ref, w_ref, b_ref, o_ref, *, rows):
    i = lax.broadcasted_iota(jnp.int32, (SG_BLOCK, SG_BLOCK), 0)
    j = lax.broadcasted_iota(jnp.int32, (SG_BLOCK, SG_BLOCK), 1)
    mask = (j >> CHUNK_SHIFT) <= (i >> CHUNK_SHIFT)
    for g in range(SG_GROUPS):
        cols = slice(g * SG_DIM, (g + 1) * SG_DIM)
        w = jnp.where(mask, w_ref[g], 0.0).astype(BF16)
        bias = b_ref[g]
        u = _gelu_tanh(u_ref[0, :, cols].astype(F32))
        v = _rms(_gelu_tanh(v_ref[0, :, cols].astype(F32)), gain_ref[:, cols]).astype(BF16)
        for n in range(rows // SG_BLOCK):
            r = slice(n * SG_BLOCK, (n + 1) * SG_BLOCK)
            mixed = jnp.dot(w, v[r], preferred_element_type=F32) + bias
            o_ref[0, r, cols] = (u[r] * mixed).astype(o_ref.dtype)


def _spatial_gate(proj, sg_norm, sg_w, sg_b, *, rows=512):
    b, s, width = proj.shape
    sgw = SG_GROUPS * SG_DIM
    u_blk = (width - 2 * sgw) // sgw
    blocks = 3 * _nbytes((rows, sgw), BF16) + _nbytes(sg_w.shape, F32)
    return pl.pallas_call(
        functools.partial(_sg_kernel, rows=rows),
        grid=(b, s // rows),
        in_specs=[
            pl.BlockSpec((1, rows, sgw), lambda bi, n: (bi, n, u_blk)),
            pl.BlockSpec((1, rows, sgw), lambda bi, n: (bi, n, u_blk + 1)),
            pl.BlockSpec((1, sgw), lambda bi, n: (0, 0)),
            pl.BlockSpec((SG_GROUPS, SG_BLOCK, SG_BLOCK), lambda bi, n: (0, 0, 0)),
            pl.BlockSpec((SG_GROUPS, SG_BLOCK, 1), lambda bi, n: (0, 0, 0)),
        ],
        out_specs=pl.BlockSpec((1, rows, sgw), lambda bi, n: (bi, n, 0)),
        out_shape=jax.ShapeDtypeStruct((b, s, sgw), BF16),
        compiler_params=pltpu.CompilerParams(
            dimension_semantics=("parallel", "parallel"),
            vmem_limit_bytes=_vmem_limit(blocks)),
        name="spatial_gate",
    )(proj, proj, sg_norm.reshape(1, sgw), sg_w, sg_b.reshape(SG_GROUPS, SG_BLOCK, 1))


def _diff_attn_kernel(q0_ref, k0_ref, cos0_ref, sin0_ref, q1_ref, k1_ref, cos1_ref, sin1_ref,
                      v_ref, qg_ref, kg_ref, sg_ref, lam_ref, o_ref,
                      qp_ref, kn_ref, sa_ref, sb_ref, m_ref, l_ref, acc_ref,
                      *, tq, nq, lambda_init):
    qi = pl.program_id(2)
    d = DIFF_HEAD_DIM
    n_lane_tiles = tq // LANES
    prep_rows = min(tq, 256)
    q_scale = d ** -0.5 * math.log2(math.e)

    def prepare(q_ref, k_ref, cos_ref, sin_ref, tile):
        ii = lax.broadcasted_iota(jnp.int32, (2 * d, 2 * d), 0)
        jj = lax.broadcasted_iota(jnp.int32, (2 * d, 2 * d), 1)
        swap_halves = jnp.where((ii ^ (d // 2)) == jj, 1.0, 0.0).astype(BF16)

        def norm_rope(x, gain_cos, gain_sin):
            xf = x.astype(F32)
            rot = jnp.dot(x, swap_halves, preferred_element_type=F32)
            out = []
            for mi in range(2):
                lanes = slice(mi * d, (mi + 1) * d)
                xm = xf[:, lanes]
                r = lax.rsqrt(jnp.mean(xm * xm, axis=-1, keepdims=True) + NORM_EPS)
                out.append(r * (xm * gain_cos + rot[:, lanes] * gain_sin))
            return jnp.concatenate(out, axis=-1).astype(BF16)

        def rolled(g):
            return pltpu.roll(jnp.broadcast_to(g, (8, d)), d // 2, 1)[0:1]

        qg = qg_ref[...] * q_scale
        kg = kg_ref[...]
        qg_rot = rolled(qg)
        kg_rot = rolled(kg)
        base = pl.multiple_of(tile * tq, tq)
        for r0 in range(0, tq, prep_rows):
            rows = slice(r0, r0 + prep_rows)
            c = cos_ref[0, rows, :]
            s = sin_ref[0, rows, :]
            qp_ref[rows, :] = norm_rope(q_ref[0, rows, :], qg * c, qg_rot * s)
            kn_ref[pl.ds(base + r0, prep_rows), :] = norm_rope(k_ref[0, rows, :], kg * c,
                                                               kg_rot * s)

    def scores(s_ref, slot, kv, masked):
        k0 = pl.multiple_of(kv * tq, tq)
        kt = kn_ref[pl.ds(k0, tq), :]
        for mi in range(2):
            lanes = slice(mi * d, (mi + 1) * d)
            s = lax.dot_general(qp_ref[:, lanes], kt[:, lanes], (((1,), (1,)), ((), ())),
                                preferred_element_type=F32)
            if masked:
                row = lax.broadcasted_iota(jnp.int32, (tq, tq), 0)
                col = lax.broadcasted_iota(jnp.int32, (tq, tq), 1)
                s = jnp.where((col >> CHUNK_SHIFT) <= (row >> CHUNK_SHIFT), s, MASK_VALUE)
            s_ref[mi, kv] = s
            m = m_ref[slot, mi]
            for c in range(n_lane_tiles):
                m = jnp.maximum(m, s[:, c * LANES:(c + 1) * LANES])
            m_ref[slot, mi] = m

    def finish_max(slot):
        for mi in range(2):
            m_row = jnp.max(m_ref[slot, mi], axis=-1, keepdims=True)
            m_ref[slot, mi] = jnp.broadcast_to(m_row, (tq, LANES))

    def weighted_values(s_ref, slot, kv):
        k0 = pl.multiple_of(kv * tq, tq)
        vt = v_ref[0, pl.ds(k0, tq), :]
        for mi in range(2):
            m = m_ref[slot, mi]
            l = l_ref[mi]
            ps = []
            for c in range(n_lane_tiles):
                p = jnp.exp2(s_ref[mi, kv, :, c * LANES:(c + 1) * LANES] - m)
                l = l + p
                ps.append(p.astype(BF16))
            l_ref[mi] = l
            acc_ref[mi] += jnp.dot(jnp.concatenate(ps, axis=-1), vt,
                                   preferred_element_type=F32)

    def run(cur_ref, nxt_ref, slot):
        nslot = 1 - slot
        l_ref[...] = jnp.zeros_like(l_ref)
        acc_ref[...] = jnp.zeros_like(acc_ref)

        @pl.when(qi < nq - 1)
        def _():
            prepare(q1_ref, k1_ref, cos1_ref, sin1_ref, qi + 1)
            m_ref[nslot] = jnp.full(m_ref.shape[1:], MASK_VALUE, F32)

            def body(kv, carry):
                weighted_values(cur_ref, slot, kv)
                scores(nxt_ref, nslot, kv, False)
                return carry

            lax.fori_loop(0, qi + 1, body, 0)
            scores(nxt_ref, nslot, qi + 1, True)
            finish_max(nslot)

        @pl.when(qi == nq - 1)
        def _():
            def body(kv, carry):
                weighted_values(cur_ref, slot, kv)
                return carry

            lax.fori_loop(0, qi + 1, body, 0)

    @pl.when(qi == 0)
    def _():
        prepare(q0_ref, k0_ref, cos0_ref, sin0_ref, 0)
        m_ref[0] = jnp.full(m_ref.shape[1:], MASK_VALUE, F32)
        scores(sa_ref, 0, 0, True)
        finish_max(0)

    @pl.when((qi & 1) == 0)
    def _():
        run(sa_ref, sb_ref, 0)

    @pl.when((qi & 1) == 1)
    def _():
        run(sb_ref, sa_ref, 1)

    lam_p = lam_ref[...]
    lam = (jnp.exp(jnp.sum(lam_p[0:1] * lam_p[1:2], axis=-1, keepdims=True))
           - jnp.exp(jnp.sum(lam_p[2:3] * lam_p[3:4], axis=-1, keepdims=True))
           + lambda_init)
    l0 = jnp.sum(l_ref[0], axis=-1, keepdims=True)
    l1 = jnp.sum(l_ref[1], axis=-1, keepdims=True)
    out = acc_ref[0] / l0 - lam * (acc_ref[1] / l1)
    o_ref[0] = (_rms(out, sg_ref[...]) * (1.0 - lambda_init)).astype(o_ref.dtype)


def _diff_attention(proj, cos_d, sin_d, q_gain, k_gain, sub_gain, lam_params, lambda_init,
                    *, tq=512):
    b, s, _ = proj.shape
    h = DIFF_HEADS
    d = DIFF_HEAD_DIM
    hw = 2 * d
    nq = s // tq
    blocks = (5 * _nbytes((tq, hw), BF16) + 4 * _nbytes((tq, d), F32) + _nbytes((s, hw), BF16))
    scratch = (_nbytes((tq, hw), BF16) + _nbytes((s, hw), BF16) + 2 * _nbytes((2, tq, s), F32)
               + _nbytes((2, 2, tq, LANES), F32) + _nbytes((2, tq, LANES), F32)
               + _nbytes((2, tq, hw), F32))
    first = lambda col: (lambda bi, hi, qi: (bi, 0, col(hi)))
    nxt = lambda col: (lambda bi, hi, qi: (bi, jnp.minimum(qi + 1, nq - 1), col(hi)))
    q_col = lambda hi: hi
    k_col = lambda hi: h + hi
    rope_col = lambda hi: 0
    return pl.pallas_call(
        functools.partial(_diff_attn_kernel, tq=tq, nq=nq, lambda_init=lambda_init),
        grid=(b, h, nq),
        in_specs=[
            pl.BlockSpec((1, tq, hw), first(q_col)),
            pl.BlockSpec((1, tq, hw), first(k_col)),
            pl.BlockSpec((1, tq, d), first(rope_col)),
            pl.BlockSpec((1, tq, d), first(rope_col)),
            pl.BlockSpec((1, tq, hw), nxt(q_col)),
            pl.BlockSpec((1, tq, hw), nxt(k_col)),
            pl.BlockSpec((1, tq, d), nxt(rope_col)),
            pl.BlockSpec((1, tq, d), nxt(rope_col)),
            pl.BlockSpec((1, s, hw), lambda bi, hi, qi: (bi, 0, 2 * h + hi)),
            pl.BlockSpec((1, d), lambda bi, hi, qi: (0, 0)),
            pl.BlockSpec((1, d), lambda bi, hi, qi: (0, 0)),
            pl.BlockSpec((1, hw), lambda bi, hi, qi: (0, 0)),
            pl.BlockSpec((4, d), lambda bi, hi, qi: (0, 0)),
        ],
        out_specs=pl.BlockSpec((1, tq, hw), lambda bi, hi, qi: (bi, qi, hi)),
        out_shape=jax.ShapeDtypeStruct((b, s, h * hw), BF16),
        scratch_shapes=[
            pltpu.VMEM((tq, hw), BF16),
            pltpu.VMEM((s, hw), BF16),
            pltpu.VMEM((2, nq, tq, tq), F32),
            pltpu.VMEM((2, nq, tq, tq), F32),
            pltpu.VMEM((2, 2, tq, LANES), F32),
            pltpu.VMEM((2, tq, LANES), F32),
            pltpu.VMEM((2, tq, hw), F32),
        ],
        compiler_params=pltpu.CompilerParams(
            dimension_semantics=("parallel", "parallel", "arbitrary"),
            vmem_limit_bytes=_vmem_limit(blocks, scratch)),
        name="diff_attention",
    )(proj, proj, cos_d, sin_d, proj, proj, cos_d, sin_d, proj,
      q_gain.reshape(1, d), k_gain.reshape(1, d), sub_gain.reshape(1, hw), lam_params)


def _xa_kernel(x_ref, g_ref, wq_ref, kv_ref, qg_ref, kg_ref, wo_ref, o_ref, kn_ref, *, heads):
    dm = x_ref.shape[2]
    hd = dm // heads

    @pl.when(pl.program_id(1) == 0)
    def _():
        for h in range(heads):
            cols = slice(h * hd, (h + 1) * hd)
            kn_ref[:, cols] = _rms(kv_ref[0, :, cols].astype(F32), kg_ref[...]).astype(BF16)

    x = x_ref[0]
    r = lax.rsqrt(jnp.mean(x * x, axis=-1, keepdims=True) + NORM_EPS)
    q = jnp.dot((x * g_ref[...]).astype(BF16), wq_ref[...], preferred_element_type=F32)
    scale = hd ** -0.5
    heads_out = []
    for h in range(heads):
        cols = slice(h * hd, (h + 1) * hd)
        qn = (_rms(q[:, cols] * r, qg_ref[...]) * scale).astype(BF16)
        s = lax.dot_general(qn, kn_ref[:, cols], (((1,), (1,)), ((), ())),
                            preferred_element_type=F32)
        e = jnp.exp(s - jnp.max(s, axis=-1, keepdims=True))
        p = e / jnp.sum(e, axis=-1, keepdims=True)
        v = kv_ref[0, :, dm + h * hd: dm + (h + 1) * hd]
        heads_out.append(jnp.dot(p.astype(BF16), v, preferred_element_type=F32).astype(BF16))
    att = jnp.concatenate(heads_out, axis=-1)
    o_ref[0] = x + jnp.dot(att, wo_ref[...], preferred_element_type=F32)


def _cross_attention_block(x, gain, wq, wo, layer, kv, q_gain, k_gain, *, tm=512):
    b, s, dm = x.shape
    mlen = kv.shape[1]
    hd = dm // XA_HEADS
    resident = pl.Buffered(buffer_count=1)
    blocks = 2 * _nbytes((tm, dm), F32) + _nbytes((mlen, 2 * dm), BF16)
    scratch = 2 * _nbytes((dm, dm), BF16) + _nbytes((mlen, dm), BF16)
    return pl.pallas_call(
        functools.partial(_xa_kernel, heads=XA_HEADS),
        grid=(b, s // tm),
        in_specs=[
            pl.BlockSpec((1, tm, dm), lambda bi, i: (bi, i, 0)),
            pl.BlockSpec((1, dm), lambda bi, i: (0, 0)),
            pl.BlockSpec((None, dm, dm), lambda bi, i: (layer, 0, 0), pipeline_mode=resident),
            pl.BlockSpec((1, mlen, 2 * dm), lambda bi, i: (bi, 0, 0)),
            pl.BlockSpec((1, hd), lambda bi, i: (0, 0)),
            pl.BlockSpec((1, hd), lambda bi, i: (0, 0)),
            pl.BlockSpec((None, dm, dm), lambda bi, i: (layer, 0, 0), pipeline_mode=resident),
        ],
        out_specs=pl.BlockSpec((1, tm, dm), lambda bi, i: (bi, i, 0)),
        out_shape=jax.ShapeDtypeStruct((b, s, dm), F32),
        scratch_shapes=[pltpu.VMEM((mlen, dm), BF16)],
        compiler_params=pltpu.CompilerParams(
            dimension_semantics=("parallel", "arbitrary"),
            vmem_limit_bytes=_vmem_limit(blocks, scratch)),
        name="cross_attention_block",
    )(x, gain.reshape(1, dm), wq, kv, q_gain.reshape(1, hd), k_gain.reshape(1, hd), wo)


def kernel(x, mem, positions, norm_mix, norm_xa, norm_mem, norm_ffn, ev_w_in, ev_ret_gain,
           ev_sg_norm, ev_sg_w, ev_sg_b, ev_w_out, od_w_qkv, od_q_gain, od_k_gain, od_lam_q1,
           od_lam_k1, od_lam_q2, od_lam_k2, od_sub_gain, od_w_o, xa_w_q, xa_w_kv, xa_q_gain,
           xa_k_gain, xa_w_o, ffn_w1, ffn_w2):
    b, s, dm = x.shape
    mlen = mem.shape[1]
    depth = norm_mix.shape[0]
    m = b * s
    ev_w_in, ev_w_out, od_w_qkv, od_w_o, xa_w_q, xa_w_kv, xa_w_o, ffn_w1, ffn_w2 = (
        w.astype(BF16) for w in (ev_w_in, ev_w_out, od_w_qkv, od_w_o, xa_w_q, xa_w_kv, xa_w_o,
                                 ffn_w1, ffn_w2))

    cos_r, sin_r, cos_d, sin_d = _rope_tables(positions)
    xf = x.reshape(m, dm)
    mem_f = mem.reshape(b * mlen, dm)

    for li in range(depth):
        if li % 2 == 0:
            e = li // 2
            proj = _norm_mm(xf, norm_mix[li], ev_w_in, e, name="even_in_proj")
            proj = proj.reshape(b, s, -1)
            out_a = _retention(proj, cos_r, sin_r, ev_ret_gain[e])
            out_b = _spatial_gate(proj, ev_sg_norm[e], ev_sg_w[e], ev_sg_b[e])
            xf = _mm2_res(out_a.reshape(m, -1), out_b.reshape(m, -1), ev_w_out, e, xf,
                          name="even_out_proj")
        else:
            o = li // 2
            lambda_init = 0.8 - 0.6 * math.exp(-0.3 * li)
            proj = _norm_mm(xf, norm_mix[li], od_w_qkv, o, name="diff_qkv_proj")
            lam_params = jnp.stack([od_lam_q1[o], od_lam_k1[o], od_lam_q2[o], od_lam_k2[o]])
            att = _diff_attention(proj.reshape(b, s, -1), cos_d, sin_d, od_q_gain[o],
                                  od_k_gain[o], od_sub_gain[o], lam_params, lambda_init)
            xf = _mm_res_full(att.reshape(m, -1), od_w_o, o, xf, name="diff_out_proj")

        kv = _norm_mm(mem_f, norm_mem[li], xa_w_kv, li, name="xa_kv_proj")
        xf = _cross_attention_block(xf.reshape(b, s, dm), norm_xa[li], xa_w_q, xa_w_o, li,
                                    kv.reshape(b, mlen, 2 * dm), xa_q_gain[li],
                                    xa_k_gain[li]).reshape(m, dm)

        hid = _norm_mm(xf, norm_ffn[li], ffn_w1, li, act="relu2", name="ffn_up")
        xf = _mm_res(hid, ffn_w2, li, xf, name="ffn_down")

    return xf.reshape(b, s, dm)
```

```python
import functools
import math

import jax
import jax.numpy as jnp
from jax import lax
from jax.experimental import pallas as pl
from jax.experimental.pallas import tpu as pltpu

F32 = jnp.float32
BF16 = jnp.bfloat16

CHUNK = 64
CHUNK_SHIFT = CHUNK.bit_length() - 1
ROPE_THETA = 10000.0
NORM_EPS = 1e-6
RET_HEADS = 4
RET_QK_DIM = 256
RET_V_DIM = 512
SG_GROUPS = 4
SG_DIM = 256
SG_BLOCK = 128
DIFF_HEADS = 8
DIFF_HEAD_DIM = 128
XA_HEADS = 4
MASK_VALUE = -1e30

V7X_VMEM_BYTES = 64 * 1024 * 1024
VMEM_REQUEST_CAP = V7X_VMEM_BYTES - 8 * 1024 * 1024
LANES = 128


def _vmem_limit(block_bytes, scratch_bytes=0):
    need = 2 * block_bytes + scratch_bytes
    return int(min(VMEM_REQUEST_CAP, max(32 * 1024 * 1024, 2 * need)))


def _nbytes(shape, dtype):
    return math.prod(shape) * jnp.dtype(dtype).itemsize


def _rms(x, gain):
    ms = jnp.mean(x * x, axis=-1, keepdims=True)
    return x * lax.rsqrt(ms + NORM_EPS) * gain


def _gelu_tanh(x):
    c = math.sqrt(2.0 / math.pi)
    return 0.5 * x * (1.0 + jnp.tanh(c * (x + 0.044715 * (x * x * x))))


def _rope_kernel(pos_ref, inv_r_ref, inv_d_ref, cr_ref, sr_ref, cd_ref, sd_ref):
    pos = pos_ref[...].astype(F32)
    ang_r = pos * inv_r_ref[...]
    cr_ref[...] = jnp.cos(ang_r)
    sr_ref[...] = jnp.sin(ang_r)
    ang_d = pos * inv_d_ref[...]
    lane = lax.broadcasted_iota(jnp.int32, ang_d.shape, 1)
    sin_d = jnp.sin(ang_d)
    cd_ref[...] = jnp.cos(ang_d)
    sd_ref[...] = jnp.where(lane < DIFF_HEAD_DIM // 2, -sin_d, sin_d)


def _rope_tables(positions):
    b, s = positions.shape
    rows = 1024
    n = b * s
    inv_r = ROPE_THETA ** (-jnp.arange(0, RET_QK_DIM, 2, dtype=F32) / RET_QK_DIM)
    inv_d = ROPE_THETA ** (-jnp.arange(0, DIFF_HEAD_DIM, 2, dtype=F32) / DIFF_HEAD_DIM)
    inv_d = jnp.concatenate([inv_d, inv_d])
    out = jax.ShapeDtypeStruct((n, LANES), F32)
    tab_spec = pl.BlockSpec((rows, LANES), lambda i: (i, 0))
    vec_spec = pl.BlockSpec((1, LANES), lambda i: (0, 0))
    outs = pl.pallas_call(
        _rope_kernel,
        grid=(n // rows,),
        in_specs=[pl.BlockSpec((rows, 1), lambda i: (i, 0)), vec_spec, vec_spec],
        out_specs=[tab_spec] * 4,
        out_shape=[out] * 4,
        name="rope_tables",
    )(positions.reshape(n, 1), inv_r.reshape(1, LANES), inv_d.reshape(1, LANES))
    return [t.reshape(b, s, LANES) for t in outs]


def _norm_mm_kernel(x_ref, g_ref, w_ref, o_ref, xg_ref, r_ref, *, act):
    def finish(y):
        r = r_ref[...]
        for c in range(y.shape[1] // LANES):
            lanes = slice(c * LANES, (c + 1) * LANES)
            z = y[:, lanes] * r
            if act == "relu2":
                z = jnp.maximum(z, 0.0)
                z = z * z
            o_ref[:, lanes] = z.astype(o_ref.dtype)

    @pl.when(pl.program_id(1) == 0)
    def _():
        x = x_ref[...]
        xg = (x * g_ref[...]).astype(BF16)
        xg_ref[...] = xg
        ms = jnp.mean(x * x, axis=-1, keepdims=True)
        r_ref[...] = jnp.broadcast_to(lax.rsqrt(ms + NORM_EPS), r_ref.shape)
        finish(jnp.dot(xg, w_ref[...].astype(BF16), preferred_element_type=F32))

    @pl.when(pl.program_id(1) != 0)
    def _():
        finish(jnp.dot(xg_ref[...], w_ref[...].astype(BF16), preferred_element_type=F32))


def _norm_mm(x, gain, w, layer, *, act=None, tm=1024, tn=2048, name):
    m, k = x.shape
    n = w.shape[2]
    tm = min(tm, m)
    assert m % tm == 0 and n % tn == 0
    blocks = (_nbytes((tm, k), F32) + _nbytes((k, tn), w.dtype) + _nbytes((tm, tn), BF16)
              + _nbytes((1, k), F32))
    return pl.pallas_call(
        functools.partial(_norm_mm_kernel, act=act),
        grid=(m // tm, n // tn),
        in_specs=[
            pl.BlockSpec((tm, k), lambda i, j: (i, 0)),
            pl.BlockSpec((1, k), lambda i, j: (0, 0)),
            pl.BlockSpec((None, k, tn), lambda i, j: (layer, 0, j)),
        ],
        out_specs=pl.BlockSpec((tm, tn), lambda i, j: (i, j)),
        out_shape=jax.ShapeDtypeStruct((m, n), BF16),
        scratch_shapes=[pltpu.VMEM((tm, k), BF16), pltpu.VMEM((tm, LANES), F32)],
        compiler_params=pltpu.CompilerParams(
            dimension_semantics=("parallel", "arbitrary"),
            vmem_limit_bytes=_vmem_limit(blocks, _nbytes((tm, k), BF16))),
        name=name,
    )(x, gain.reshape(1, k), w)


def _mm_res_kernel(a_ref, w_ref, r_ref, o_ref, acc_ref, *, nk):
    k = pl.program_id(2)

    @pl.when(k == 0)
    def _():
        acc_ref[...] = r_ref[...]

    acc_ref[...] += jnp.dot(a_ref[...], w_ref[...], preferred_element_type=F32)

    @pl.when(k == nk - 1)
    def _():
        o_ref[...] = acc_ref[...]


def _mm_res_full_kernel(a_ref, w_ref, r_ref, o_ref):
    o_ref[...] = r_ref[...] + jnp.dot(a_ref[...], w_ref[...], preferred_element_type=F32)


def _mm_res_full(a, w, layer, res, *, tm=512, name):
    m, kdim = a.shape
    n = w.shape[2]
    assert m % tm == 0
    blocks = (_nbytes((tm, kdim), BF16) + _nbytes((kdim, n), BF16) + 2 * _nbytes((tm, n), F32))
    return pl.pallas_call(
        _mm_res_full_kernel,
        grid=(m // tm,),
        in_specs=[
            pl.BlockSpec((tm, kdim), lambda i: (i, 0)),
            pl.BlockSpec((None, kdim, n), lambda i: (layer, 0, 0)),
            pl.BlockSpec((tm, n), lambda i: (i, 0)),
        ],
        out_specs=pl.BlockSpec((tm, n), lambda i: (i, 0)),
        out_shape=jax.ShapeDtypeStruct((m, n), F32),
        compiler_params=pltpu.CompilerParams(
            dimension_semantics=("parallel",),
            vmem_limit_bytes=_vmem_limit(blocks)),
        name=name,
    )(a, w, res)


def _mm_res(a, w, layer, res, *, tm=1024, tn=1024, tk=2048, name):
    m, kdim = a.shape
    n = w.shape[2]
    tk = min(tk, kdim)
    assert m % tm == 0 and n % tn == 0 and kdim % tk == 0
    nk = kdim // tk
    blocks = (_nbytes((tm, tk), BF16) + _nbytes((tk, tn), BF16) + 2 * _nbytes((tm, tn), F32))
    return pl.pallas_call(
        functools.partial(_mm_res_kernel, nk=nk),
        grid=(m // tm, n // tn, nk),
        in_specs=[
            pl.BlockSpec((tm, tk), lambda i, j, k: (i, k)),
            pl.BlockSpec((None, tk, tn), lambda i, j, k: (layer, k, j)),
            pl.BlockSpec((tm, tn), lambda i, j, k: (i, j)),
        ],
        out_specs=pl.BlockSpec((tm, tn), lambda i, j, k: (i, j)),
        out_shape=jax.ShapeDtypeStruct((m, n), F32),
        scratch_shapes=[pltpu.VMEM((tm, tn), F32)],
        compiler_params=pltpu.CompilerParams(
            dimension_semantics=("parallel", "parallel", "arbitrary"),
            vmem_limit_bytes=_vmem_limit(blocks, _nbytes((tm, tn), F32))),
        name=name,
    )(a, w, res)


def _mm2_res_kernel(a1_ref, a2_ref, w1_ref, w2_ref, r_ref, o_ref):
    acc = r_ref[...] + jnp.dot(a1_ref[...], w1_ref[...], preferred_element_type=F32)
    o_ref[...] = acc + jnp.dot(a2_ref[...], w2_ref[...], preferred_element_type=F32)


def _mm2_res(a1, a2, w, layer, res, *, tm=1024, tn=1024, name):
    m, k1 = a1.shape
    k2 = a2.shape[1]
    n = w.shape[2]
    assert w.shape[1] == k1 + k2 and k1 % k2 == 0
    assert m % tm == 0 and n % tn == 0
    blocks = (_nbytes((tm, k1 + k2), BF16) + _nbytes((k1 + k2, tn), BF16)
              + 2 * _nbytes((tm, tn), F32))
    return pl.pallas_call(
        _mm2_res_kernel,
        grid=(m // tm, n // tn),
        in_specs=[
            pl.BlockSpec((tm, k1), lambda i, j: (i, 0)),
            pl.BlockSpec((tm, k2), lambda i, j: (i, 0)),
            pl.BlockSpec((None, k1, tn), lambda i, j: (layer, 0, j)),
            pl.BlockSpec((None, k2, tn), lambda i, j: (layer, k1 // k2, j)),
            pl.BlockSpec((tm, tn), lambda i, j: (i, j)),
        ],
        out_specs=pl.BlockSpec((tm, tn), lambda i, j: (i, j)),
        out_shape=jax.ShapeDtypeStruct((m, n), F32),
        compiler_params=pltpu.CompilerParams(
            dimension_semantics=("parallel", "parallel"),
            vmem_limit_bytes=_vmem_limit(blocks)),
        name=name,
    )(a1, a2, w, w, res)


def _ret_kernel(q_ref, k_ref, v_ref, g_ref, cos_ref, sin_ref, gain_ref, lg_ref, o_ref,
                state_ref, dmat_ref, qd_ref, kd_ref, *, blk):
    n = pl.program_id(2)
    lg = lg_ref[0][:, 0:1]

    @pl.when(n == 0)
    def _():
        state_ref[...] = jnp.zeros_like(state_ref)
        i = lax.broadcasted_iota(jnp.int32, (blk, blk), 0)
        j = lax.broadcasted_iota(jnp.int32, (blk, blk), 1)
        ci = i >> CHUNK_SHIFT
        cj = j >> CHUNK_SHIFT
        d = (i - j).astype(F32)
        expo = jnp.where(ci == cj, jnp.abs(d), d)
        dmat_ref[...] = jnp.where(ci >= cj, jnp.exp(lg * expo), 0.0)
        r = lax.broadcasted_iota(jnp.int32, (blk, RET_QK_DIM), 0).astype(F32)
        qd_ref[...] = jnp.exp(lg * r)
        kd_ref[...] = jnp.exp(lg * (blk - r))

    half = RET_QK_DIM // 2
    c = cos_ref[0]
    s = sin_ref[0]

    def rope(x):
        x1 = x[:, :half]
        x2 = x[:, half:]
        return jnp.concatenate([x1 * c - x2 * s, x2 * c + x1 * s], axis=-1)

    q = rope(q_ref[0].astype(F32))
    k = rope(k_ref[0].astype(F32)) * (RET_QK_DIM ** -0.5)
    v = v_ref[0]
    state = state_ref[...]

    scores = lax.dot_general(q.astype(BF16), k.astype(BF16), (((1,), (1,)), ((), ())),
                             preferred_element_type=F32) * dmat_ref[...]
    out = jnp.dot(scores.astype(BF16), v, preferred_element_type=F32)
    out = out + jnp.dot((q * qd_ref[...]).astype(BF16), state.astype(BF16),
                        preferred_element_type=F32)
    kv = lax.dot_general((k * kd_ref[...]).astype(BF16), v, (((0,), (0,)), ((), ())),
                         preferred_element_type=F32)
    state_ref[...] = state * jnp.exp(lg * blk) + kv

    y = _rms(out, gain_ref[0])
    g = g_ref[0].astype(F32)
    o_ref[0] = (g / (1.0 + jnp.exp(-g)) * y).astype(o_ref.dtype)


def _retention(proj, cos_r, sin_r, ret_gain, *, blk=512):
    b, s, _ = proj.shape
    h = RET_HEADS
    dk, dv = RET_QK_DIM, RET_V_DIM
    v_off = 2 * h * dk // dv
    g_off = v_off + h
    log_g = jnp.log(1.0 - 2.0 ** (-5.0 - jnp.arange(h, dtype=F32)))
    log_g = jnp.broadcast_to(log_g[:, None, None], (h, 1, LANES))
    blocks = (2 * _nbytes((blk, dk), BF16) + 3 * _nbytes((blk, dv), BF16)
              + 2 * _nbytes((blk, LANES), F32))
    scratch = (_nbytes((dk, dv), F32) + _nbytes((blk, blk), F32) + 2 * _nbytes((blk, dk), F32))
    return pl.pallas_call(
        functools.partial(_ret_kernel, blk=blk),
        grid=(b, h, s // blk),
        in_specs=[
            pl.BlockSpec((1, blk, dk), lambda bi, hi, n: (bi, n, hi)),
            pl.BlockSpec((1, blk, dk), lambda bi, hi, n: (bi, n, h + hi)),
            pl.BlockSpec((1, blk, dv), lambda bi, hi, n: (bi, n, v_off + hi)),
            pl.BlockSpec((1, blk, dv), lambda bi, hi, n: (bi, n, g_off + hi)),
            pl.BlockSpec((1, blk, LANES), lambda bi, hi, n: (bi, n, 0)),
            pl.BlockSpec((1, blk, LANES), lambda bi, hi, n: (bi, n, 0)),
            pl.BlockSpec((1, 1, dv), lambda bi, hi, n: (hi, 0, 0)),
            pl.BlockSpec((1, 1, LANES), lambda bi, hi, n: (hi, 0, 0)),
        ],
        out_specs=pl.BlockSpec((1, blk, dv), lambda bi, hi, n: (bi, n, hi)),
        out_shape=jax.ShapeDtypeStruct((b, s, h * dv), BF16),
        scratch_shapes=[
            pltpu.VMEM((dk, dv), F32),
            pltpu.VMEM((blk, blk), F32),
            pltpu.VMEM((blk, dk), F32),
            pltpu.VMEM((blk, dk), F32),
        ],
        compiler_params=pltpu.CompilerParams(
            dimension_semantics=("parallel", "parallel", "arbitrary"),
            vmem_limit_bytes=_vmem_limit(blocks, scratch)),
        name="retention",
    )(proj, proj, proj, proj, cos_r, sin_r, ret_gain.reshape(h, 1, dv), log_g)


def _sg_kernel(u_ref, v_ref, gain_ref, w_ref, b_ref, o_ref, *, rows):
    i = lax.broadcasted_iota(jnp.int32, (SG_BLOCK, SG_BLOCK), 0)
    j = lax.broadcasted_iota(jnp.int32, (SG_BLOCK, SG_BLOCK), 1)
    mask = (j >> CHUNK_SHIFT) <= (i >> CHUNK_SHIFT)
    for g in range(SG_GROUPS):
        cols = slice(g * SG_DIM, (g + 1) * SG_DIM)
        w = jnp.where(mask, w_ref[g], 0.0).astype(BF16)
        bias = b_ref[g]
        u = _gelu_tanh(u_ref[0, :, cols].astype(F32))
        v = _rms(_gelu_tanh(v_ref[0, :, cols].astype(F32)), gain_ref[:, cols]).astype(BF16)
        for n in range(rows // SG_BLOCK):
            r = slice(n * SG_BLOCK, (n + 1) * SG_BLOCK)
            mixed = jnp.dot(w, v[r], preferred_element_type=F32) + bias
            o_ref[0, r, cols] = (u[r] * mixed).astype(o_ref.dtype)


def _spatial_gate(proj, sg_norm, sg_w, sg_b, *, rows=512):
    b, s, width = proj.shape
    sgw = SG_GROUPS * SG_DIM
    u_blk = (width - 2 * sgw) // sgw
    blocks = 3 * _nbytes((rows, sgw), BF16) + _nbytes(sg_w.shape, F32)
    return pl.pallas_call(
        functools.partial(_sg_kernel, rows=rows),
        grid=(b, s // rows),
        in_specs=[
            pl.BlockSpec((1, rows, sgw), lambda bi, n: (bi, n, u_blk)),
            pl.BlockSpec((1, rows, sgw), lambda bi, n: (bi, n, u_blk + 1)),
            pl.BlockSpec((1, sgw), lambda bi, n: (0, 0)),
            pl.BlockSpec((SG_GROUPS, SG_BLOCK, SG_BLOCK), lambda bi, n: (0, 0, 0)),
            pl.BlockSpec((SG_GROUPS, SG_BLOCK, 1), lambda bi, n: (0, 0, 0)),
        ],
        out_specs=pl.BlockSpec((1, rows, sgw), lambda bi, n: (bi, n, 0)),
        out_shape=jax.ShapeDtypeStruct((b, s, sgw), BF16),
        compiler_params=pltpu.CompilerParams(
            dimension_semantics=("parallel", "parallel"),
            vmem_limit_bytes=_vmem_limit(blocks)),
        name="spatial_gate",
    )(proj, proj, sg_norm.reshape(1, sgw), sg_w, sg_b.reshape(SG_GROUPS, SG_BLOCK, 1))


def _diff_attn_kernel(q0_ref, k0_ref, cos0_ref, sin0_ref, q1_ref, k1_ref, cos1_ref, sin1_ref,
                      v_ref, qg_ref, kg_ref, sg_ref, lam_ref, o_ref,
                      qp_ref, kn_ref, bias_ref, sa_ref, sb_ref, m_ref, l_ref, acc_ref,
                      *, tq, nq, lambda_init):
    qi = pl.program_id(2)
    d = DIFF_HEAD_DIM
    n_lane_tiles = tq // LANES
    prep_rows = min(tq, 256)
    q_scale = d ** -0.5 * math.log2(math.e)

    def prepare(q_ref, k_ref, cos_ref, sin_ref, tile):
        ii = lax.broadcasted_iota(jnp.int32, (2 * d, 2 * d), 0)
        jj = lax.broadcasted_iota(jnp.int32, (2 * d, 2 * d), 1)
        swap_halves = jnp.where((ii ^ (d // 2)) == jj, 1.0, 0.0).astype(BF16)

        def norm_rope(x, gain_cos, gain_sin):
            xf = x.astype(F32)
            rot = jnp.dot(x, swap_halves, preferred_element_type=F32)
            out = []
            for mi in range(2):
                lanes = slice(mi * d, (mi + 1) * d)
                xm = xf[:, lanes]
                r = lax.rsqrt(jnp.mean(xm * xm, axis=-1, keepdims=True) + NORM_EPS)
                out.append(r * (xm * gain_cos + rot[:, lanes] * gain_sin))
            return jnp.concatenate(out, axis=-1).astype(BF16)

        def rolled(g):
            return pltpu.roll(jnp.broadcast_to(g, (8, d)), d // 2, 1)[0:1]

        qg = qg_ref[...] * q_scale
        kg = kg_ref[...]
        qg_rot = rolled(qg)
        kg_rot = rolled(kg)
        base = pl.multiple_of(tile * tq, tq)
        for r0 in range(0, tq, prep_rows):
            rows = slice(r0, r0 + prep_rows)
            c = cos_ref[0, rows, :]
            s = sin_ref[0, rows, :]
            qp_ref[rows, :] = norm_rope(q_ref[0, rows, :], qg * c, qg_rot * s)
            kn_ref[pl.ds(base + r0, prep_rows), :] = norm_rope(k_ref[0, rows, :], kg * c,
                                                               kg_rot * s)

    def scores(s_ref, slot, kv, masked):
        k0 = pl.multiple_of(kv * tq, tq)
        kt = kn_ref[pl.ds(k0, tq), :]
        for mi in range(2):
            lanes = slice(mi * d, (mi + 1) * d)
            s = lax.dot_general(qp_ref[:, lanes], kt[:, lanes], (((1,), (1,)), ((), ())),
                                preferred_element_type=F32)
            if masked:
                s = s + bias_ref[...]
            s_ref[mi, kv] = s
            m = m_ref[slot, mi]
            for c in range(n_lane_tiles):
                m = jnp.maximum(m, s[:, c * LANES:(c + 1) * LANES])
            m_ref[slot, mi] = m

    def finish_max(slot):
        for mi in range(2):
            m_row = jnp.max(m_ref[slot, mi], axis=-1, keepdims=True)
            m_ref[slot, mi] = jnp.broadcast_to(m_row, (tq, LANES))

    def weighted_values(s_ref, slot, kv):
        k0 = pl.multiple_of(kv * tq, tq)
        vt = v_ref[0, pl.ds(k0, tq), :]
        for mi in range(2):
            m = m_ref[slot, mi]
            l = l_ref[mi]
            ps = []
            for c in range(n_lane_tiles):
                p = jnp.exp2(s_ref[mi, kv, :, c * LANES:(c + 1) * LANES] - m)
                l = l + p
                ps.append(p.astype(BF16))
            l_ref[mi] = l
            acc_ref[mi] += jnp.dot(jnp.concatenate(ps, axis=-1), vt,
                                   preferred_element_type=F32)

    def run(cur_ref, nxt_ref, slot):
        nslot = 1 - slot
        l_ref[...] = jnp.zeros_like(l_ref)
        acc_ref[...] = jnp.zeros_like(acc_ref)

        @pl.when(qi < nq - 1)
        def _():
            prepare(q1_ref, k1_ref, cos1_ref, sin1_ref, qi + 1)
            m_ref[nslot] = jnp.full(m_ref.shape[1:], MASK_VALUE, F32)

            def body(kv, carry):
                weighted_values(cur_ref, slot, kv)
                scores(nxt_ref, nslot, kv, False)
                return carry

            lax.fori_loop(0, qi + 1, body, 0)
            scores(nxt_ref, nslot, qi + 1, True)
            finish_max(nslot)

        @pl.when(qi == nq - 1)
        def _():
            def body(kv, carry):
                weighted_values(cur_ref, slot, kv)
                return carry

            lax.fori_loop(0, qi + 1, body, 0)

    @pl.when(qi == 0)
    def _():
        row = lax.broadcasted_iota(jnp.int32, (tq, tq), 0)
        col = lax.broadcasted_iota(jnp.int32, (tq, tq), 1)
        bias_ref[...] = jnp.where((col >> CHUNK_SHIFT) <= (row >> CHUNK_SHIFT), 0.0, MASK_VALUE)
        prepare(q0_ref, k0_ref, cos0_ref, sin0_ref, 0)
        m_ref[0] = jnp.full(m_ref.shape[1:], MASK_VALUE, F32)
        scores(sa_ref, 0, 0, True)
        finish_max(0)

    @pl.when((qi & 1) == 0)
    def _():
        run(sa_ref, sb_ref, 0)

    @pl.when((qi & 1) == 1)
    def _():
        run(sb_ref, sa_ref, 1)

    lam_p = lam_ref[...]
    lam = (jnp.exp(jnp.sum(lam_p[0:1] * lam_p[1:2], axis=-1, keepdims=True))
           - jnp.exp(jnp.sum(lam_p[2:3] * lam_p[3:4], axis=-1, keepdims=True))
           + lambda_init)
    l0 = jnp.sum(l_ref[0], axis=-1, keepdims=True)
    l1 = jnp.sum(l_ref[1], axis=-1, keepdims=True)
    out = acc_ref[0] / l0 - lam * (acc_ref[1] / l1)
    o_ref[0] = (_rms(out, sg_ref[...]) * (1.0 - lambda_init)).astype(o_ref.dtype)


def _diff_attention(proj, cos_d, sin_d, q_gain, k_gain, sub_gain, lam_params, lambda_init,
                    *, tq=512):
    b, s, _ = proj.shape
    h = DIFF_HEADS
    d = DIFF_HEAD_DIM
    hw = 2 * d
    nq = s // tq
    blocks = (5 * _nbytes((tq, hw), BF16) + 4 * _nbytes((tq, d), F32) + _nbytes((s, hw), BF16))
    scratch = (_nbytes((tq, hw), BF16) + _nbytes((s, hw), BF16) + 2 * _nbytes((2, tq, s), F32)
               + _nbytes((2, 2, tq, LANES), F32) + _nbytes((2, tq, LANES), F32)
               + _nbytes((2, tq, hw), F32))
    first = lambda col: (lambda bi, hi, qi: (bi, 0, col(hi)))
    nxt = lambda col: (lambda bi, hi, qi: (bi, jnp.minimum(qi + 1, nq - 1), col(hi)))
    q_col = lambda hi: hi
    k_col = lambda hi: h + hi
    rope_col = lambda hi: 0
    return pl.pallas_call(
        functools.partial(_diff_attn_kernel, tq=tq, nq=nq, lambda_init=lambda_init),
        grid=(b, h, nq),
        in_specs=[
            pl.BlockSpec((1, tq, hw), first(q_col)),
            pl.BlockSpec((1, tq, hw), first(k_col)),
            pl.BlockSpec((1, tq, d), first(rope_col)),
            pl.BlockSpec((1, tq, d), first(rope_col)),
            pl.BlockSpec((1, tq, hw), nxt(q_col)),
            pl.BlockSpec((1, tq, hw), nxt(k_col)),
            pl.BlockSpec((1, tq, d), nxt(rope_col)),
            pl.BlockSpec((1, tq, d), nxt(rope_col)),
            pl.BlockSpec((1, s, hw), lambda bi, hi, qi: (bi, 0, 2 * h + hi)),
            pl.BlockSpec((1, d), lambda bi, hi, qi: (0, 0)),
            pl.BlockSpec((1, d), lambda bi, hi, qi: (0, 0)),
            pl.BlockSpec((1, hw), lambda bi, hi, qi: (0, 0)),
            pl.BlockSpec((4, d), lambda bi, hi, qi: (0, 0)),
        ],
        out_specs=pl.BlockSpec((1, tq, hw), lambda bi, hi, qi: (bi, qi, hi)),
        out_shape=jax.ShapeDtypeStruct((b, s, h * hw), BF16),
        scratch_shapes=[
            pltpu.VMEM((tq, hw), BF16),
            pltpu.VMEM((s, hw), BF16),
            pltpu.VMEM((tq, tq), F32),
            pltpu.VMEM((2, nq, tq, tq), F32),
            pltpu.VMEM((2, nq, tq, tq), F32),
            pltpu.VMEM((2, 2, tq, LANES), F32),
            pltpu.VMEM((2, tq, LANES), F32),
            pltpu.VMEM((2, tq, hw), F32),
        ],
        compiler_params=pltpu.CompilerParams(
            dimension_semantics=("parallel", "parallel", "arbitrary"),
            vmem_limit_bytes=_vmem_limit(blocks, scratch)),
        name="diff_attention",
    )(proj, proj, cos_d, sin_d, proj, proj, cos_d, sin_d, proj,
      q_gain.reshape(1, d), k_gain.reshape(1, d), sub_gain.reshape(1, hw), lam_params)


def _xa_kernel(x_ref, g_ref, wq_ref, kv_ref, qg_ref, kg_ref, wo_ref, o_ref, kn_ref, *, heads):
    dm = x_ref.shape[2]
    hd = dm // heads

    @pl.when(pl.program_id(1) == 0)
    def _():
        for h in range(heads):
            cols = slice(h * hd, (h + 1) * hd)
            kn_ref[:, cols] = _rms(kv_ref[0, :, cols].astype(F32), kg_ref[...]).astype(BF16)

    x = x_ref[0]
    r = lax.rsqrt(jnp.mean(x * x, axis=-1, keepdims=True) + NORM_EPS)
    q = jnp.dot((x * g_ref[...]).astype(BF16), wq_ref[...], preferred_element_type=F32)
    scale = hd ** -0.5
    heads_out = []
    for h in range(heads):
        cols = slice(h * hd, (h + 1) * hd)
        qn = (_rms(q[:, cols] * r, qg_ref[...]) * scale).astype(BF16)
        s = lax.dot_general(qn, kn_ref[:, cols], (((1,), (1,)), ((), ())),
                            preferred_element_type=F32)
        e = jnp.exp(s - jnp.max(s, axis=-1, keepdims=True))
        p = e / jnp.sum(e, axis=-1, keepdims=True)
        v = kv_ref[0, :, dm + h * hd: dm + (h + 1) * hd]
        heads_out.append(jnp.dot(p.astype(BF16), v, preferred_element_type=F32).astype(BF16))
    att = jnp.concatenate(heads_out, axis=-1)
    o_ref[0] = x + jnp.dot(att, wo_ref[...], preferred_element_type=F32)


def _cross_attention_block(x, gain, wq, wo, layer, kv, q_gain, k_gain, *, tm=512):
    b, s, dm = x.shape
    mlen = kv.shape[1]
    hd = dm // XA_HEADS
    resident = pl.Buffered(buffer_count=1)
    blocks = 2 * _nbytes((tm, dm), F32) + _nbytes((mlen, 2 * dm), BF16)
    scratch = 2 * _nbytes((dm, dm), BF16) + _nbytes((mlen, dm), BF16)
    return pl.pallas_call(
        functools.partial(_xa_kernel, heads=XA_HEADS),
        grid=(b, s // tm),
        in_specs=[
            pl.BlockSpec((1, tm, dm), lambda bi, i: (bi, i, 0)),
            pl.BlockSpec((1, dm), lambda bi, i: (0, 0)),
            pl.BlockSpec((None, dm, dm), lambda bi, i: (layer, 0, 0), pipeline_mode=resident),
            pl.BlockSpec((1, mlen, 2 * dm), lambda bi, i: (bi, 0, 0)),
            pl.BlockSpec((1, hd), lambda bi, i: (0, 0)),
            pl.BlockSpec((1, hd), lambda bi, i: (0, 0)),
            pl.BlockSpec((None, dm, dm), lambda bi, i: (layer, 0, 0), pipeline_mode=resident),
        ],
        out_specs=pl.BlockSpec((1, tm, dm), lambda bi, i: (bi, i, 0)),
        out_shape=jax.ShapeDtypeStruct((b, s, dm), F32),
        scratch_shapes=[pltpu.VMEM((mlen, dm), BF16)],
        compiler_params=pltpu.CompilerParams(
            dimension_semantics=("parallel", "arbitrary"),
            vmem_limit_bytes=_vmem_limit(blocks, scratch)),
        name="cross_attention_block",
    )(x, gain.reshape(1, dm), wq, kv, q_gain.reshape(1, hd), k_gain.reshape(1, hd), wo)


def kernel(x, mem, positions, norm_mix, norm_xa, norm_mem, norm_ffn, ev_w_in, ev_ret_gain,
           ev_sg_norm, ev_sg_w, ev_sg_b, ev_w_out, od_w_qkv, od_q_gain, od_k_gain, od_lam_q1,
           od_lam_k1, od_lam_q2, od_lam_k2, od_sub_gain, od_w_o, xa_w_q, xa_w_kv, xa_q_gain,
           xa_k_gain, xa_w_o, ffn_w1, ffn_w2):
    b, s, dm = x.shape
    mlen = mem.shape[1]
    depth = norm_mix.shape[0]
    m = b * s
    ev_w_in, ev_w_out, od_w_qkv, od_w_o, xa_w_q, xa_w_o, ffn_w1, ffn_w2 = (
        w.astype(BF16) for w in (ev_w_in, ev_w_out, od_w_qkv, od_w_o, xa_w_q, xa_w_o,
                                 ffn_w1, ffn_w2))

    cos_r, sin_r, cos_d, sin_d = _rope_tables(positions)
    xf = x.reshape(m, dm)
    mem_f = mem.reshape(b * mlen, dm)

    for li in range(depth):
        if li % 2 == 0:
            e = li // 2
            proj = _norm_mm(xf, norm_mix[li], ev_w_in, e, name="even_in_proj")
            proj = proj.reshape(b, s, -1)
            out_a = _retention(proj, cos_r, sin_r, ev_ret_gain[e])
            out_b = _spatial_gate(proj, ev_sg_norm[e], ev_sg_w[e], ev_sg_b[e])
            xf = _mm2_res(out_a.reshape(m, -1), out_b.reshape(m, -1), ev_w_out, e, xf,
                          name="even_out_proj")
        else:
            o = li // 2
            lambda_init = 0.8 - 0.6 * math.exp(-0.3 * li)
            proj = _norm_mm(xf, norm_mix[li], od_w_qkv, o, name="diff_qkv_proj")
            lam_params = jnp.stack([od_lam_q1[o], od_lam_k1[o], od_lam_q2[o], od_lam_k2[o]])
            att = _diff_attention(proj.reshape(b, s, -1), cos_d, sin_d, od_q_gain[o],
                                  od_k_gain[o], od_sub_gain[o], lam_params, lambda_init)
            xf = _mm_res_full(att.reshape(m, -1), od_w_o, o, xf, name="diff_out_proj")

        kv = _norm_mm(mem_f, norm_mem[li], xa_w_kv, li, tn=1024, name="xa_kv_proj")
        xf = _cross_attention_block(xf.reshape(b, s, dm), norm_xa[li], xa_w_q, xa_w_o, li,
                                    kv.reshape(b, mlen, 2 * dm), xa_q_gain[li],
                                    xa_k_gain[li]).reshape(m, dm)

        hid = _norm_mm(xf, norm_ffn[li], ffn_w1, li, act="relu2", name="ffn_up")
        xf = _mm_res(hid, ffn_w2, li, xf, name="ffn_down")

    return xf.reshape(b, s, dm)
```

```python
import functools
import math

import jax
import jax.numpy as jnp
from jax import lax
from jax.experimental import pallas as pl
from jax.experimental.pallas import tpu as pltpu

F32 = jnp.float32
BF16 = jnp.bfloat16

CHUNK = 64
CHUNK_SHIFT = CHUNK.bit_length() - 1
ROPE_THETA = 10000.0
NORM_EPS = 1e-6
RET_HEADS = 4
RET_QK_DIM = 256
RET_V_DIM = 512
SG_GROUPS = 4
SG_DIM = 256
SG_BLOCK = 128
DIFF_HEADS = 8
DIFF_HEAD_DIM = 128
XA_HEADS = 4
MASK_VALUE = -1e30

V7X_VMEM_BYTES = 64 * 1024 * 1024
VMEM_REQUEST_CAP = V7X_VMEM_BYTES - 8 * 1024 * 1024
LANES = 128


def _vmem_limit(block_bytes, scratch_bytes=0):
    need = 2 * block_bytes + scratch_bytes
    return int(min(VMEM_REQUEST_CAP, max(32 * 1024 * 1024, 2 * need)))


def _nbytes(shape, dtype):
    return math.prod(shape) * jnp.dtype(dtype).itemsize


def _rms(x, gain):
    ms = jnp.mean(x * x, axis=-1, keepdims=True)
    return x * lax.rsqrt(ms + NORM_EPS) * gain


def _gelu_tanh(x):
    c = math.sqrt(2.0 / math.pi)
    return 0.5 * x * (1.0 + jnp.tanh(c * (x + 0.044715 * (x * x * x))))


def _rope_kernel(pos_ref, inv_r_ref, inv_d_ref, cr_ref, sr_ref, cd_ref, sd_ref):
    pos = pos_ref[...].astype(F32)
    ang_r = pos * inv_r_ref[...]
    cr_ref[...] = jnp.cos(ang_r)
    sr_ref[...] = jnp.sin(ang_r)
    ang_d = pos * inv_d_ref[...]
    lane = lax.broadcasted_iota(jnp.int32, ang_d.shape, 1)
    sin_d = jnp.sin(ang_d)
    cd_ref[...] = jnp.cos(ang_d)
    sd_ref[...] = jnp.where(lane < DIFF_HEAD_DIM // 2, -sin_d, sin_d)


def _rope_tables(positions):
    b, s = positions.shape
    rows = 1024
    n = b * s
    inv_r = ROPE_THETA ** (-jnp.arange(0, RET_QK_DIM, 2, dtype=F32) / RET_QK_DIM)
    inv_d = ROPE_THETA ** (-jnp.arange(0, DIFF_HEAD_DIM, 2, dtype=F32) / DIFF_HEAD_DIM)
    inv_d = jnp.concatenate([inv_d, inv_d])
    out = jax.ShapeDtypeStruct((n, LANES), F32)
    tab_spec = pl.BlockSpec((rows, LANES), lambda i: (i, 0))
    vec_spec = pl.BlockSpec((1, LANES), lambda i: (0, 0))
    outs = pl.pallas_call(
        _rope_kernel,
        grid=(n // rows,),
        in_specs=[pl.BlockSpec((rows, 1), lambda i: (i, 0)), vec_spec, vec_spec],
        out_specs=[tab_spec] * 4,
        out_shape=[out] * 4,
        name="rope_tables",
    )(positions.reshape(n, 1), inv_r.reshape(1, LANES), inv_d.reshape(1, LANES))
    return [t.reshape(b, s, LANES) for t in outs]


def _norm_mm_kernel(x_ref, g_ref, w_ref, o_ref, xg_ref, r_ref, *, act):
    def finish(y):
        r = r_ref[...]
        for c in range(y.shape[1] // LANES):
            lanes = slice(c * LANES, (c + 1) * LANES)
            z = y[:, lanes] * r
            if act == "relu2":
                z = jnp.maximum(z, 0.0)
                z = z * z
            o_ref[:, lanes] = z.astype(o_ref.dtype)

    @pl.when(pl.program_id(1) == 0)
    def _():
        x = x_ref[...]
        xg = (x * g_ref[...]).astype(BF16)
        xg_ref[...] = xg
        ms = jnp.mean(x * x, axis=-1, keepdims=True)
        r_ref[...] = jnp.broadcast_to(lax.rsqrt(ms + NORM_EPS), r_ref.shape)
        finish(jnp.dot(xg, w_ref[...].astype(BF16), preferred_element_type=F32))

    @pl.when(pl.program_id(1) != 0)
    def _():
        finish(jnp.dot(xg_ref[...], w_ref[...].astype(BF16), preferred_element_type=F32))


def _norm_mm(x, gain, w, layer, *, act=None, tm=1024, tn=2048, name):
    m, k = x.shape
    n = w.shape[2]
    tm = min(tm, m)
    assert m % tm == 0 and n % tn == 0
    blocks = (_nbytes((tm, k), F32) + _nbytes((k, tn), w.dtype) + _nbytes((tm, tn), BF16)
              + _nbytes((1, k), F32))
    return pl.pallas_call(
        functools.partial(_norm_mm_kernel, act=act),
        grid=(m // tm, n // tn),
        in_specs=[
            pl.BlockSpec((tm, k), lambda i, j: (i, 0)),
            pl.BlockSpec((1, k), lambda i, j: (0, 0)),
            pl.BlockSpec((None, k, tn), lambda i, j: (layer, 0, j)),
        ],
        out_specs=pl.BlockSpec((tm, tn), lambda i, j: (i, j)),
        out_shape=jax.ShapeDtypeStruct((m, n), BF16),
        scratch_shapes=[pltpu.VMEM((tm, k), BF16), pltpu.VMEM((tm, LANES), F32)],
        compiler_params=pltpu.CompilerParams(
            dimension_semantics=("parallel", "arbitrary"),
            vmem_limit_bytes=_vmem_limit(blocks, _nbytes((tm, k), BF16))),
        name=name,
    )(x, gain.reshape(1, k), w)


def _mm_res_kernel(a_ref, w_ref, r_ref, o_ref):
    @pl.when(pl.program_id(2) == 0)
    def _():
        o_ref[...] = r_ref[...] + jnp.dot(a_ref[...], w_ref[...], preferred_element_type=F32)

    @pl.when(pl.program_id(2) != 0)
    def _():
        o_ref[...] += jnp.dot(a_ref[...], w_ref[...], preferred_element_type=F32)


def _mm_res_full_kernel(a_ref, w_ref, r_ref, o_ref):
    o_ref[...] = r_ref[...] + jnp.dot(a_ref[...], w_ref[...], preferred_element_type=F32)


def _mm_res_full(a, w, layer, res, *, tm=512, name):
    m, kdim = a.shape
    n = w.shape[2]
    assert m % tm == 0
    blocks = (_nbytes((tm, kdim), BF16) + _nbytes((kdim, n), BF16) + 2 * _nbytes((tm, n), F32))
    return pl.pallas_call(
        _mm_res_full_kernel,
        grid=(m // tm,),
        in_specs=[
            pl.BlockSpec((tm, kdim), lambda i: (i, 0)),
            pl.BlockSpec((None, kdim, n), lambda i: (layer, 0, 0)),
            pl.BlockSpec((tm, n), lambda i: (i, 0)),
        ],
        out_specs=pl.BlockSpec((tm, n), lambda i: (i, 0)),
        out_shape=jax.ShapeDtypeStruct((m, n), F32),
        compiler_params=pltpu.CompilerParams(
            dimension_semantics=("parallel",),
            vmem_limit_bytes=_vmem_limit(blocks)),
        name=name,
    )(a, w, res)


def _mm_res(a, w, layer, res, *, tm=1024, tn=1024, tk=4096, name):
    m, kdim = a.shape
    n = w.shape[2]
    tk = min(tk, kdim)
    assert m % tm == 0 and n % tn == 0 and kdim % tk == 0
    nk = kdim // tk
    blocks = (_nbytes((tm, tk), BF16) + _nbytes((tk, tn), BF16) + 2 * _nbytes((tm, tn), F32))
    return pl.pallas_call(
        _mm_res_kernel,
        grid=(m // tm, n // tn, nk),
        in_specs=[
            pl.BlockSpec((tm, tk), lambda i, j, k: (i, k)),
            pl.BlockSpec((None, tk, tn), lambda i, j, k: (layer, k, j)),
            pl.BlockSpec((tm, tn), lambda i, j, k: (i, j)),
        ],
        out_specs=pl.BlockSpec((tm, tn), lambda i, j, k: (i, j)),
        out_shape=jax.ShapeDtypeStruct((m, n), F32),
        compiler_params=pltpu.CompilerParams(
            dimension_semantics=("parallel", "parallel", "arbitrary"),
            vmem_limit_bytes=_vmem_limit(blocks)),
        name=name,
    )(a, w, res)


def _mm2_res_kernel(a1_ref, a2_ref, w1_ref, w2_ref, r_ref, o_ref):
    acc = r_ref[...] + jnp.dot(a1_ref[...], w1_ref[...], preferred_element_type=F32)
    o_ref[...] = acc + jnp.dot(a2_ref[...], w2_ref[...], preferred_element_type=F32)


def _mm2_res(a1, a2, w, layer, res, *, tm=1024, tn=1024, name):
    m, k1 = a1.shape
    k2 = a2.shape[1]
    n = w.shape[2]
    assert w.shape[1] == k1 + k2 and k1 % k2 == 0
    assert m % tm == 0 and n % tn == 0
    blocks = (_nbytes((tm, k1 + k2), BF16) + _nbytes((k1 + k2, tn), BF16)
              + 2 * _nbytes((tm, tn), F32))
    return pl.pallas_call(
        _mm2_res_kernel,
        grid=(m // tm, n // tn),
        in_specs=[
            pl.BlockSpec((tm, k1), lambda i, j: (i, 0)),
            pl.BlockSpec((tm, k2), lambda i, j: (i, 0)),
            pl.BlockSpec((None, k1, tn), lambda i, j: (layer, 0, j)),
            pl.BlockSpec((None, k2, tn), lambda i, j: (layer, k1 // k2, j)),
            pl.BlockSpec((tm, tn), lambda i, j: (i, j)),
        ],
        out_specs=pl.BlockSpec((tm, tn), lambda i, j: (i, j)),
        out_shape=jax.ShapeDtypeStruct((m, n), F32),
        compiler_params=pltpu.CompilerParams(
            dimension_semantics=("parallel", "parallel"),
            vmem_limit_bytes=_vmem_limit(blocks)),
        name=name,
    )(a1, a2, w, w, res)


def _ret_kernel(q_ref, k_ref, v_ref, g_ref, cos_ref, sin_ref, gain_ref, lg_ref, o_ref,
                state_ref, dmat_ref, qd_ref, kd_ref, *, blk):
    n = pl.program_id(2)
    lg = lg_ref[0][:, 0:1]

    @pl.when(n == 0)
    def _():
        state_ref[...] = jnp.zeros_like(state_ref)
        i = lax.broadcasted_iota(jnp.int32, (blk, blk), 0)
        j = lax.broadcasted_iota(jnp.int32, (blk, blk), 1)
        ci = i >> CHUNK_SHIFT
        cj = j >> CHUNK_SHIFT
        d = (i - j).astype(F32)
        expo = jnp.where(ci == cj, jnp.abs(d), d)
        dmat_ref[...] = jnp.where(ci >= cj, jnp.exp(lg * expo), 0.0)
        r = lax.broadcasted_iota(jnp.int32, (blk, RET_QK_DIM), 0).astype(F32)
        qd_ref[...] = jnp.exp(lg * r)
        kd_ref[...] = jnp.exp(lg * (blk - r))

    half = RET_QK_DIM // 2
    c = cos_ref[0]
    s = sin_ref[0]

    def rope(x):
        x1 = x[:, :half]
        x2 = x[:, half:]
        return jnp.concatenate([x1 * c - x2 * s, x2 * c + x1 * s], axis=-1)

    q = rope(q_ref[0].astype(F32))
    k = rope(k_ref[0].astype(F32)) * (RET_QK_DIM ** -0.5)
    v = v_ref[0]
    state = state_ref[...]

    scores = lax.dot_general(q.astype(BF16), k.astype(BF16), (((1,), (1,)), ((), ())),
                             preferred_element_type=F32) * dmat_ref[...]
    out = jnp.dot(scores.astype(BF16), v, preferred_element_type=F32)
    out = out + jnp.dot((q * qd_ref[...]).astype(BF16), state.astype(BF16),
                        preferred_element_type=F32)
    kv = lax.dot_general((k * kd_ref[...]).astype(BF16), v, (((0,), (0,)), ((), ())),
                         preferred_element_type=F32)
    state_ref[...] = state * jnp.exp(lg * blk) + kv

    y = _rms(out, gain_ref[0])
    g = g_ref[0].astype(F32)
    o_ref[0] = (g / (1.0 + jnp.exp(-g)) * y).astype(o_ref.dtype)


def _retention(proj, cos_r, sin_r, ret_gain, *, blk=512):
    b, s, _ = proj.shape
    h = RET_HEADS
    dk, dv = RET_QK_DIM, RET_V_DIM
    v_off = 2 * h * dk // dv
    g_off = v_off + h
    log_g = jnp.log(1.0 - 2.0 ** (-5.0 - jnp.arange(h, dtype=F32)))
    log_g = jnp.broadcast_to(log_g[:, None, None], (h, 1, LANES))
    blocks = (2 * _nbytes((blk, dk), BF16) + 3 * _nbytes((blk, dv), BF16)
              + 2 * _nbytes((blk, LANES), F32))
    scratch = (_nbytes((dk, dv), F32) + _nbytes((blk, blk), F32) + 2 * _nbytes((blk, dk), F32))
    return pl.pallas_call(
        functools.partial(_ret_kernel, blk=blk),
        grid=(b, h, s // blk),
        in_specs=[
            pl.BlockSpec((1, blk, dk), lambda bi, hi, n: (bi, n, hi)),
            pl.BlockSpec((1, blk, dk), lambda bi, hi, n: (bi, n, h + hi)),
            pl.BlockSpec((1, blk, dv), lambda bi, hi, n: (bi, n, v_off + hi)),
            pl.BlockSpec((1, blk, dv), lambda bi, hi, n: (bi, n, g_off + hi)),
            pl.BlockSpec((1, blk, LANES), lambda bi, hi, n: (bi, n, 0)),
            pl.BlockSpec((1, blk, LANES), lambda bi, hi, n: (bi, n, 0)),
            pl.BlockSpec((1, 1, dv), lambda bi, hi, n: (hi, 0, 0)),
            pl.BlockSpec((1, 1, LANES), lambda bi, hi, n: (hi, 0, 0)),
        ],
        out_specs=pl.BlockSpec((1, blk, dv), lambda bi, hi, n: (bi, n, hi)),
        out_shape=jax.ShapeDtypeStruct((b, s, h * dv), BF16),
        scratch_shapes=[
            pltpu.VMEM((dk, dv), F32),
            pltpu.VMEM((blk, blk), F32),
            pltpu.VMEM((blk, dk), F32),
            pltpu.VMEM((blk, dk), F32),
        ],
        compiler_params=pltpu.CompilerParams(
            dimension_semantics=("parallel", "parallel", "arbitrary"),
            vmem_limit_bytes=_vmem_limit(blocks, scratch)),
        name="retention",
    )(proj, proj, proj, proj, cos_r, sin_r, ret_gain.reshape(h, 1, dv), log_g)


def _sg_kernel(u_ref, v_ref, gain_ref, w_ref, b_ref, o_ref, *, rows):
    i = lax.broadcasted_iota(jnp.int32, (SG_BLOCK, SG_BLOCK), 0)
    j = lax.broadcasted_iota(jnp.int32, (SG_BLOCK, SG_BLOCK), 1)
    mask = (j >> CHUNK_SHIFT) <= (i >> CHUNK_SHIFT)
    for g in range(SG_GROUPS):
        cols = slice(g * SG_DIM, (g + 1) * SG_DIM)
        w = jnp.where(mask, w_ref[g], 0.0).astype(BF16)
        bias = b_ref[g]
        u = _gelu_tanh(u_ref[0, :, cols].astype(F32))
        v = _rms(_gelu_tanh(v_ref[0, :, cols].astype(F32)), gain_ref[:, cols]).astype(BF16)
        for n in range(rows // SG_BLOCK):
            r = slice(n * SG_BLOCK, (n + 1) * SG_BLOCK)
            mixed = jnp.dot(w, v[r], preferred_element_type=F32) + bias
            o_ref[0, r, cols] = (u[r] * mixed).astype(o_ref.dtype)


def _spatial_gate(proj, sg_norm, sg_w, sg_b, *, rows=512):
    b, s, width = proj.shape
    sgw = SG_GROUPS * SG_DIM
    u_blk = (width - 2 * sgw) // sgw
    blocks = 3 * _nbytes((rows, sgw), BF16) + _nbytes(sg_w.shape, F32)
    return pl.pallas_call(
        functools.partial(_sg_kernel, rows=rows),
        grid=(b, s // rows),
        in_specs=[
            pl.BlockSpec((1, rows, sgw), lambda bi, n: (bi, n, u_blk)),
            pl.BlockSpec((1, rows, sgw), lambda bi, n: (bi, n, u_blk + 1)),
            pl.BlockSpec((1, sgw), lambda bi, n: (0, 0)),
            pl.BlockSpec((SG_GROUPS, SG_BLOCK, SG_BLOCK), lambda bi, n: (0, 0, 0)),
            pl.BlockSpec((SG_GROUPS, SG_BLOCK, 1), lambda bi, n: (0, 0, 0)),
        ],
        out_specs=pl.BlockSpec((1, rows, sgw), lambda bi, n: (bi, n, 0)),
        out_shape=jax.ShapeDtypeStruct((b, s, sgw), BF16),
        compiler_params=pltpu.CompilerParams(
            dimension_semantics=("parallel", "parallel"),
            vmem_limit_bytes=_vmem_limit(blocks)),
        name="spatial_gate",
    )(proj, proj, sg_norm.reshape(1, sgw), sg_w, sg_b.reshape(SG_GROUPS, SG_BLOCK, 1))


def _diff_attn_kernel(q0_ref, k0_ref, cos0_ref, sin0_ref, q1_ref, k1_ref, cos1_ref, sin1_ref,
                      v_ref, qg_ref, kg_ref, sg_ref, lam_ref, o_ref,
                      qp_ref, kn_ref, bias_ref, sa_ref, sb_ref, m_ref, l_ref, acc_ref,
                      *, tq, nq, lambda_init):
    qi = pl.program_id(2)
    d = DIFF_HEAD_DIM
    n_lane_tiles = tq // LANES
    prep_rows = min(tq, 256)
    q_scale = d ** -0.5 * math.log2(math.e)

    def prepare(q_ref, k_ref, cos_ref, sin_ref, tile):
        ii = lax.broadcasted_iota(jnp.int32, (2 * d, 2 * d), 0)
        jj = lax.broadcasted_iota(jnp.int32, (2 * d, 2 * d), 1)
        swap_halves = jnp.where((ii ^ (d // 2)) == jj, 1.0, 0.0).astype(BF16)

        def norm_rope(x, gain_cos, gain_sin):
            xf = x.astype(F32)
            rot = jnp.dot(x, swap_halves, preferred_element_type=F32)
            out = []
            for mi in range(2):
                lanes = slice(mi * d, (mi + 1) * d)
                xm = xf[:, lanes]
                r = lax.rsqrt(jnp.mean(xm * xm, axis=-1, keepdims=True) + NORM_EPS)
                out.append(r * (xm * gain_cos + rot[:, lanes] * gain_sin))
            return jnp.concatenate(out, axis=-1).astype(BF16)

        def rolled(g):
            return pltpu.roll(jnp.broadcast_to(g, (8, d)), d // 2, 1)[0:1]

        qg = qg_ref[...] * q_scale
        kg = kg_ref[...]
        qg_rot = rolled(qg)
        kg_rot = rolled(kg)
        base = pl.multiple_of(tile * tq, tq)
        for r0 in range(0, tq, prep_rows):
            rows = slice(r0, r0 + prep_rows)
            c = cos_ref[0, rows, :]
            s = sin_ref[0, rows, :]
            qp_ref[rows, :] = norm_rope(q_ref[0, rows, :], qg * c, qg_rot * s)
            kn_ref[pl.ds(base + r0, prep_rows), :] = norm_rope(k_ref[0, rows, :], kg * c,
                                                               kg_rot * s)

    def scores(s_ref, slot, kv, width, masked=False):
        k0 = pl.multiple_of(kv * tq, tq)
        kt = kn_ref[pl.ds(k0, width * tq), :]
        for mi in range(2):
            lanes = slice(mi * d, (mi + 1) * d)
            s = lax.dot_general(qp_ref[:, lanes], kt[:, lanes], (((1,), (1,)), ((), ())),
                                preferred_element_type=F32)
            if masked:
                s = s + bias_ref[...]
            for w in range(width):
                s_ref[mi, kv + w] = s[:, w * tq:(w + 1) * tq]
            m = m_ref[slot, mi]
            for c in range(width * n_lane_tiles):
                m = jnp.maximum(m, s[:, c * LANES:(c + 1) * LANES])
            m_ref[slot, mi] = m

    def finish_max(slot):
        for mi in range(2):
            m_row = jnp.max(m_ref[slot, mi], axis=-1, keepdims=True)
            m_ref[slot, mi] = jnp.broadcast_to(m_row, (tq, LANES))

    def weighted_values(s_ref, slot, kv, width):
        k0 = pl.multiple_of(kv * tq, tq)
        vt = v_ref[0, pl.ds(k0, width * tq), :]
        for mi in range(2):
            m = m_ref[slot, mi]
            l = l_ref[mi]
            ps = []
            for w in range(width):
                for c in range(n_lane_tiles):
                    p = jnp.exp2(s_ref[mi, kv + w, :, c * LANES:(c + 1) * LANES] - m)
                    l = l + p
                    ps.append(p.astype(BF16))
            l_ref[mi] = l
            acc_ref[mi] += jnp.dot(jnp.concatenate(ps, axis=-1), vt,
                                   preferred_element_type=F32)

    def run(cur_ref, nxt_ref, slot):
        nslot = 1 - slot
        odd_tile_count = slot == 0
        n_pairs = (qi + 1) >> 1
        l_ref[...] = jnp.zeros_like(l_ref)
        acc_ref[...] = jnp.zeros_like(acc_ref)

        @pl.when(qi < nq - 1)
        def _():
            prepare(q1_ref, k1_ref, cos1_ref, sin1_ref, qi + 1)
            m_ref[nslot] = jnp.full(m_ref.shape[1:], MASK_VALUE, F32)

            def body(pair, carry):
                weighted_values(cur_ref, slot, 2 * pair, 2)
                scores(nxt_ref, nslot, 2 * pair, 2)
                return carry

            lax.fori_loop(0, n_pairs, body, 0)
            if odd_tile_count:
                weighted_values(cur_ref, slot, qi, 1)
                scores(nxt_ref, nslot, qi, 1)
            scores(nxt_ref, nslot, qi + 1, 1, masked=True)
            finish_max(nslot)

        @pl.when(qi == nq - 1)
        def _():
            def body(pair, carry):
                weighted_values(cur_ref, slot, 2 * pair, 2)
                return carry

            lax.fori_loop(0, n_pairs, body, 0)
            if odd_tile_count:
                weighted_values(cur_ref, slot, qi, 1)

    @pl.when(qi == 0)
    def _():
        row = lax.broadcasted_iota(jnp.int32, (tq, tq), 0)
        col = lax.broadcasted_iota(jnp.int32, (tq, tq), 1)
        bias_ref[...] = jnp.where((col >> CHUNK_SHIFT) <= (row >> CHUNK_SHIFT), 0.0, MASK_VALUE)
        prepare(q0_ref, k0_ref, cos0_ref, sin0_ref, 0)
        m_ref[0] = jnp.full(m_ref.shape[1:], MASK_VALUE, F32)
        scores(sa_ref, 0, 0, 1, masked=True)
        finish_max(0)

    @pl.when((qi & 1) == 0)
    def _():
        run(sa_ref, sb_ref, 0)

    @pl.when((qi & 1) == 1)
    def _():
        run(sb_ref, sa_ref, 1)

    lam_p = lam_ref[...]
    lam = (jnp.exp(jnp.sum(lam_p[0:1] * lam_p[1:2], axis=-1, keepdims=True))
           - jnp.exp(jnp.sum(lam_p[2:3] * lam_p[3:4], axis=-1, keepdims=True))
           + lambda_init)
    l0 = jnp.sum(l_ref[0], axis=-1, keepdims=True)
    l1 = jnp.sum(l_ref[1], axis=-1, keepdims=True)
    out = acc_ref[0] / l0 - lam * (acc_ref[1] / l1)
    o_ref[0] = (_rms(out, sg_ref[...]) * (1.0 - lambda_init)).astype(o_ref.dtype)


def _diff_attention(proj, cos_d, sin_d, q_gain, k_gain, sub_gain, lam_params, lambda_init,
                    *, tq=512):
    b, s, _ = proj.shape
    h = DIFF_HEADS
    d = DIFF_HEAD_DIM
    hw = 2 * d
    nq = s // tq
    blocks = (5 * _nbytes((tq, hw), BF16) + 4 * _nbytes((tq, d), F32) + _nbytes((s, hw), BF16))
    scratch = (_nbytes((tq, hw), BF16) + _nbytes((s, hw), BF16) + 2 * _nbytes((2, tq, s), F32)
               + _nbytes((2, 2, tq, LANES), F32) + _nbytes((2, tq, LANES), F32)
               + _nbytes((2, tq, hw), F32))
    first = lambda col: (lambda bi, hi, qi: (bi, 0, col(hi)))
    nxt = lambda col: (lambda bi, hi, qi: (bi, jnp.minimum(qi + 1, nq - 1), col(hi)))
    q_col = lambda hi: hi
    k_col = lambda hi: h + hi
    rope_col = lambda hi: 0
    return pl.pallas_call(
        functools.partial(_diff_attn_kernel, tq=tq, nq=nq, lambda_init=lambda_init),
        grid=(b, h, nq),
        in_specs=[
            pl.BlockSpec((1, tq, hw), first(q_col)),
            pl.BlockSpec((1, tq, hw), first(k_col)),
            pl.BlockSpec((1, tq, d), first(rope_col)),
            pl.BlockSpec((1, tq, d), first(rope_col)),
            pl.BlockSpec((1, tq, hw), nxt(q_col)),
            pl.BlockSpec((1, tq, hw), nxt(k_col)),
            pl.BlockSpec((1, tq, d), nxt(rope_col)),
            pl.BlockSpec((1, tq, d), nxt(rope_col)),
            pl.BlockSpec((1, s, hw), lambda bi, hi, qi: (bi, 0, 2 * h + hi)),
            pl.BlockSpec((1, d), lambda bi, hi, qi: (0, 0)),
            pl.BlockSpec((1, d), lambda bi, hi, qi: (0, 0)),
            pl.BlockSpec((1, hw), lambda bi, hi, qi: (0, 0)),
            pl.BlockSpec((4, d), lambda bi, hi, qi: (0, 0)),
        ],
        out_specs=pl.BlockSpec((1, tq, hw), lambda bi, hi, qi: (bi, qi, hi)),
        out_shape=jax.ShapeDtypeStruct((b, s, h * hw), BF16),
        scratch_shapes=[
            pltpu.VMEM((tq, hw), BF16),
            pltpu.VMEM((s, hw), BF16),
            pltpu.VMEM((tq, tq), F32),
            pltpu.VMEM((2, nq, tq, tq), F32),
            pltpu.VMEM((2, nq, tq, tq), F32),
            pltpu.VMEM((2, 2, tq, LANES), F32),
            pltpu.VMEM((2, tq, LANES), F32),
            pltpu.VMEM((2, tq, hw), F32),
        ],
        compiler_params=pltpu.CompilerParams(
            dimension_semantics=("parallel", "parallel", "arbitrary"),
            vmem_limit_bytes=_vmem_limit(blocks, scratch)),
        name="diff_attention",
    )(proj, proj, cos_d, sin_d, proj, proj, cos_d, sin_d, proj,
      q_gain.reshape(1, d), k_gain.reshape(1, d), sub_gain.reshape(1, hw), lam_params)


def _xa_kernel(x_ref, g_ref, wq_ref, kv_ref, qg_ref, kg_ref, wo_ref, o_ref, kn_ref, *, heads):
    dm = x_ref.shape[2]
    hd = dm // heads

    @pl.when(pl.program_id(1) == 0)
    def _():
        for h in range(heads):
            cols = slice(h * hd, (h + 1) * hd)
            kn_ref[:, cols] = _rms(kv_ref[0, :, cols].astype(F32), kg_ref[...]).astype(BF16)

    x = x_ref[0]
    r = lax.rsqrt(jnp.mean(x * x, axis=-1, keepdims=True) + NORM_EPS)
    q = jnp.dot((x * g_ref[...]).astype(BF16), wq_ref[...], preferred_element_type=F32)
    scale = hd ** -0.5
    heads_out = []
    for h in range(heads):
        cols = slice(h * hd, (h + 1) * hd)
        qn = (_rms(q[:, cols] * r, qg_ref[...]) * scale).astype(BF16)
        s = lax.dot_general(qn, kn_ref[:, cols], (((1,), (1,)), ((), ())),
                            preferred_element_type=F32)
        e = jnp.exp(s - jnp.max(s, axis=-1, keepdims=True))
        p = e / jnp.sum(e, axis=-1, keepdims=True)
        v = kv_ref[0, :, dm + h * hd: dm + (h + 1) * hd]
        heads_out.append(jnp.dot(p.astype(BF16), v, preferred_element_type=F32).astype(BF16))
    att = jnp.concatenate(heads_out, axis=-1)
    o_ref[0] = x + jnp.dot(att, wo_ref[...], preferred_element_type=F32)


def _cross_attention_block(x, gain, wq, wo, layer, kv, q_gain, k_gain, *, tm=512):
    b, s, dm = x.shape
    mlen = kv.shape[1]
    hd = dm // XA_HEADS
    resident = pl.Buffered(buffer_count=1)
    blocks = 2 * _nbytes((tm, dm), F32) + _nbytes((mlen, 2 * dm), BF16)
    scratch = 2 * _nbytes((dm, dm), BF16) + _nbytes((mlen, dm), BF16)
    return pl.pallas_call(
        functools.partial(_xa_kernel, heads=XA_HEADS),
        grid=(b, s // tm),
        in_specs=[
            pl.BlockSpec((1, tm, dm), lambda bi, i: (bi, i, 0)),
            pl.BlockSpec((1, dm), lambda bi, i: (0, 0)),
            pl.BlockSpec((None, dm, dm), lambda bi, i: (layer, 0, 0), pipeline_mode=resident),
            pl.BlockSpec((1, mlen, 2 * dm), lambda bi, i: (bi, 0, 0)),
            pl.BlockSpec((1, hd), lambda bi, i: (0, 0)),
            pl.BlockSpec((1, hd), lambda bi, i: (0, 0)),
            pl.BlockSpec((None, dm, dm), lambda bi, i: (layer, 0, 0), pipeline_mode=resident),
        ],
        out_specs=pl.BlockSpec((1, tm, dm), lambda bi, i: (bi, i, 0)),
        out_shape=jax.ShapeDtypeStruct((b, s, dm), F32),
        scratch_shapes=[pltpu.VMEM((mlen, dm), BF16)],
        compiler_params=pltpu.CompilerParams(
            dimension_semantics=("parallel", "arbitrary"),
            vmem_limit_bytes=_vmem_limit(blocks, scratch)),
        name="cross_attention_block",
    )(x, gain.reshape(1, dm), wq, kv, q_gain.reshape(1, hd), k_gain.reshape(1, hd), wo)


def kernel(x, mem, positions, norm_mix, norm_xa, norm_mem, norm_ffn, ev_w_in, ev_ret_gain,
           ev_sg_norm, ev_sg_w, ev_sg_b, ev_w_out, od_w_qkv, od_q_gain, od_k_gain, od_lam_q1,
           od_lam_k1, od_lam_q2, od_lam_k2, od_sub_gain, od_w_o, xa_w_q, xa_w_kv, xa_q_gain,
           xa_k_gain, xa_w_o, ffn_w1, ffn_w2):
    b, s, dm = x.shape
    mlen = mem.shape[1]
    depth = norm_mix.shape[0]
    m = b * s
    ev_w_in, ev_w_out, od_w_qkv, od_w_o, xa_w_q, xa_w_o, ffn_w1, ffn_w2 = (
        w.astype(BF16) for w in (ev_w_in, ev_w_out, od_w_qkv, od_w_o, xa_w_q, xa_w_o,
                                 ffn_w1, ffn_w2))

    cos_r, sin_r, cos_d, sin_d = _rope_tables(positions)
    xf = x.reshape(m, dm)
    mem_f = mem.reshape(b * mlen, dm)

    for li in range(depth):
        if li % 2 == 0:
            e = li // 2
            proj = _norm_mm(xf, norm_mix[li], ev_w_in, e, name="even_in_proj")
            proj = proj.reshape(b, s, -1)
            out_a = _retention(proj, cos_r, sin_r, ev_ret_gain[e])
            out_b = _spatial_gate(proj, ev_sg_norm[e], ev_sg_w[e], ev_sg_b[e])
            xf = _mm2_res(out_a.reshape(m, -1), out_b.reshape(m, -1), ev_w_out, e, xf,
                          name="even_out_proj")
        else:
            o = li // 2
            lambda_init = 0.8 - 0.6 * math.exp(-0.3 * li)
            proj = _norm_mm(xf, norm_mix[li], od_w_qkv, o, name="diff_qkv_proj")
            lam_params = jnp.stack([od_lam_q1[o], od_lam_k1[o], od_lam_q2[o], od_lam_k2[o]])
            att = _diff_attention(proj.reshape(b, s, -1), cos_d, sin_d, od_q_gain[o],
                                  od_k_gain[o], od_sub_gain[o], lam_params, lambda_init)
            xf = _mm_res_full(att.reshape(m, -1), od_w_o, o, xf, name="diff_out_proj")

        kv = _norm_mm(mem_f, norm_mem[li], xa_w_kv, li, tn=1024, name="xa_kv_proj")
        xf = _cross_attention_block(xf.reshape(b, s, dm), norm_xa[li], xa_w_q, xa_w_o, li,
                                    kv.reshape(b, mlen, 2 * dm), xa_q_gain[li],
                                    xa_k_gain[li]).reshape(m, dm)

        hid = _norm_mm(xf, norm_ffn[li], ffn_w1, li, act="relu2", name="ffn_up")
        xf = _mm_res(hid, ffn_w2, li, xf, name="ffn_down")

    return xf.reshape(b, s, dm)
```

```python
import functools
import math

import jax
import jax.numpy as jnp
from jax import lax
from jax.experimental import pallas as pl
from jax.experimental.pallas import tpu as pltpu

F32 = jnp.float32
BF16 = jnp.bfloat16

CHUNK = 64
CHUNK_SHIFT = CHUNK.bit_length() - 1
ROPE_THETA = 10000.0
NORM_EPS = 1e-6
RET_HEADS = 4
RET_QK_DIM = 256
RET_V_DIM = 512
SG_GROUPS = 4
SG_DIM = 256
SG_BLOCK = 128
DIFF_HEADS = 8
DIFF_HEAD_DIM = 128
XA_HEADS = 4
MASK_VALUE = -1e30

V7X_VMEM_BYTES = 64 * 1024 * 1024
VMEM_REQUEST_CAP = V7X_VMEM_BYTES - 8 * 1024 * 1024
LANES = 128


def _vmem_limit(block_bytes, scratch_bytes=0):
    need = 2 * block_bytes + scratch_bytes
    return int(min(VMEM_REQUEST_CAP, max(32 * 1024 * 1024, 2 * need)))


def _nbytes(shape, dtype):
    return math.prod(shape) * jnp.dtype(dtype).itemsize


def _rms(x, gain):
    ms = jnp.mean(x * x, axis=-1, keepdims=True)
    return x * lax.rsqrt(ms + NORM_EPS) * gain


def _gelu_tanh(x):
    c = math.sqrt(2.0 / math.pi)
    return 0.5 * x * (1.0 + jnp.tanh(c * (x + 0.044715 * (x * x * x))))


def _rope_kernel(pos_ref, inv_r_ref, inv_d_ref, cr_ref, sr_ref, cd_ref, sd_ref):
    pos = pos_ref[...].astype(F32)
    ang_r = pos * inv_r_ref[...]
    cr_ref[...] = jnp.cos(ang_r)
    sr_ref[...] = jnp.sin(ang_r)
    ang_d = pos * inv_d_ref[...]
    lane = lax.broadcasted_iota(jnp.int32, ang_d.shape, 1)
    sin_d = jnp.sin(ang_d)
    cd_ref[...] = jnp.cos(ang_d)
    sd_ref[...] = jnp.where(lane < DIFF_HEAD_DIM // 2, -sin_d, sin_d)


def _rope_tables(positions):
    b, s = positions.shape
    rows = 1024
    n = b * s
    inv_r = ROPE_THETA ** (-jnp.arange(0, RET_QK_DIM, 2, dtype=F32) / RET_QK_DIM)
    inv_d = ROPE_THETA ** (-jnp.arange(0, DIFF_HEAD_DIM, 2, dtype=F32) / DIFF_HEAD_DIM)
    inv_d = jnp.concatenate([inv_d, inv_d])
    out = jax.ShapeDtypeStruct((n, LANES), F32)
    tab_spec = pl.BlockSpec((rows, LANES), lambda i: (i, 0))
    vec_spec = pl.BlockSpec((1, LANES), lambda i: (0, 0))
    outs = pl.pallas_call(
        _rope_kernel,
        grid=(n // rows,),
        in_specs=[pl.BlockSpec((rows, 1), lambda i: (i, 0)), vec_spec, vec_spec],
        out_specs=[tab_spec] * 4,
        out_shape=[out] * 4,
        name="rope_tables",
    )(positions.reshape(n, 1), inv_r.reshape(1, LANES), inv_d.reshape(1, LANES))
    return [t.reshape(b, s, LANES) for t in outs]


def _norm_mm_kernel(x_ref, g_ref, w_ref, o_ref, xg_ref, r_ref, *, act):
    def finish(y):
        r = r_ref[...]
        for c in range(y.shape[1] // LANES):
            lanes = slice(c * LANES, (c + 1) * LANES)
            z = y[:, lanes] * r
            if act == "relu2":
                z = jnp.maximum(z, 0.0)
                z = z * z
            o_ref[:, lanes] = z.astype(o_ref.dtype)

    @pl.when(pl.program_id(1) == 0)
    def _():
        x = x_ref[...]
        xg = (x * g_ref[...]).astype(BF16)
        xg_ref[...] = xg
        ms = jnp.mean(x * x, axis=-1, keepdims=True)
        r_ref[...] = jnp.broadcast_to(lax.rsqrt(ms + NORM_EPS), r_ref.shape)
        finish(jnp.dot(xg, w_ref[...].astype(BF16), preferred_element_type=F32))

    @pl.when(pl.program_id(1) != 0)
    def _():
        finish(jnp.dot(xg_ref[...], w_ref[...].astype(BF16), preferred_element_type=F32))


def _norm_mm(x, gain, w, layer, *, act=None, tm=1024, tn=2048, name):
    m, k = x.shape
    n = w.shape[2]
    tm = min(tm, m)
    assert m % tm == 0 and n % tn == 0
    blocks = (_nbytes((tm, k), F32) + _nbytes((k, tn), w.dtype) + _nbytes((tm, tn), BF16)
              + _nbytes((1, k), F32))
    return pl.pallas_call(
        functools.partial(_norm_mm_kernel, act=act),
        grid=(m // tm, n // tn),
        in_specs=[
            pl.BlockSpec((tm, k), lambda i, j: (i, 0)),
            pl.BlockSpec((1, k), lambda i, j: (0, 0)),
            pl.BlockSpec((None, k, tn), lambda i, j: (layer, 0, j)),
        ],
        out_specs=pl.BlockSpec((tm, tn), lambda i, j: (i, j)),
        out_shape=jax.ShapeDtypeStruct((m, n), BF16),
        scratch_shapes=[pltpu.VMEM((tm, k), BF16), pltpu.VMEM((tm, LANES), F32)],
        compiler_params=pltpu.CompilerParams(
            dimension_semantics=("parallel", "arbitrary"),
            vmem_limit_bytes=_vmem_limit(blocks, _nbytes((tm, k), BF16))),
        name=name,
    )(x, gain.reshape(1, k), w)


def _mm_res_kernel(a_ref, w_ref, r_ref, o_ref):
    @pl.when(pl.program_id(2) == 0)
    def _():
        o_ref[...] = r_ref[...] + jnp.dot(a_ref[...], w_ref[...], preferred_element_type=F32)

    @pl.when(pl.program_id(2) != 0)
    def _():
        o_ref[...] += jnp.dot(a_ref[...], w_ref[...], preferred_element_type=F32)


def _mm_res_full_kernel(a_ref, w_ref, r_ref, o_ref):
    o_ref[...] = r_ref[...] + jnp.dot(a_ref[...], w_ref[...], preferred_element_type=F32)


def _mm_res_full(a, w, layer, res, *, tm=1024, name):
    m, kdim = a.shape
    n = w.shape[2]
    assert m % tm == 0
    blocks = _nbytes((tm, kdim), BF16) + 2 * _nbytes((tm, n), F32)
    return pl.pallas_call(
        _mm_res_full_kernel,
        grid=(m // tm,),
        in_specs=[
            pl.BlockSpec((tm, kdim), lambda i: (i, 0)),
            pl.BlockSpec((None, kdim, n), lambda i: (layer, 0, 0),
                         pipeline_mode=pl.Buffered(buffer_count=1)),
            pl.BlockSpec((tm, n), lambda i: (i, 0)),
        ],
        out_specs=pl.BlockSpec((tm, n), lambda i: (i, 0)),
        out_shape=jax.ShapeDtypeStruct((m, n), F32),
        compiler_params=pltpu.CompilerParams(
            dimension_semantics=("parallel",),
            vmem_limit_bytes=_vmem_limit(blocks, _nbytes((kdim, n), BF16))),
        name=name,
    )(a, w, res)


def _mm_res(a, w, layer, res, *, tm=1024, tn=1024, tk=4096, name):
    m, kdim = a.shape
    n = w.shape[2]
    tk = min(tk, kdim)
    assert m % tm == 0 and n % tn == 0 and kdim % tk == 0
    nk = kdim // tk
    blocks = (_nbytes((tm, tk), BF16) + _nbytes((tk, tn), BF16) + 2 * _nbytes((tm, tn), F32))
    return pl.pallas_call(
        _mm_res_kernel,
        grid=(m // tm, n // tn, nk),
        in_specs=[
            pl.BlockSpec((tm, tk), lambda i, j, k: (i, k)),
            pl.BlockSpec((None, tk, tn), lambda i, j, k: (layer, k, j)),
            pl.BlockSpec((tm, tn), lambda i, j, k: (i, j)),
        ],
        out_specs=pl.BlockSpec((tm, tn), lambda i, j, k: (i, j)),
        out_shape=jax.ShapeDtypeStruct((m, n), F32),
        compiler_params=pltpu.CompilerParams(
            dimension_semantics=("parallel", "parallel", "arbitrary"),
            vmem_limit_bytes=_vmem_limit(blocks)),
        name=name,
    )(a, w, res)


def _mm2_res_kernel(a1_ref, a2_ref, w1_ref, w2_ref, r_ref, o_ref):
    acc = r_ref[...] + jnp.dot(a1_ref[...], w1_ref[...], preferred_element_type=F32)
    o_ref[...] = acc + jnp.dot(a2_ref[...], w2_ref[...], preferred_element_type=F32)


def _mm2_res(a1, a2, w, layer, res, *, tm=512, name):
    m, k1 = a1.shape
    k2 = a2.shape[1]
    n = w.shape[2]
    assert w.shape[1] == k1 + k2 and k1 % k2 == 0
    assert m % tm == 0
    resident = pl.Buffered(buffer_count=1)
    blocks = _nbytes((tm, k1 + k2), BF16) + 2 * _nbytes((tm, n), F32)
    return pl.pallas_call(
        _mm2_res_kernel,
        grid=(m // tm,),
        in_specs=[
            pl.BlockSpec((tm, k1), lambda i: (i, 0)),
            pl.BlockSpec((tm, k2), lambda i: (i, 0)),
            pl.BlockSpec((None, k1, n), lambda i: (layer, 0, 0), pipeline_mode=resident),
            pl.BlockSpec((None, k2, n), lambda i: (layer, k1 // k2, 0), pipeline_mode=resident),
            pl.BlockSpec((tm, n), lambda i: (i, 0)),
        ],
        out_specs=pl.BlockSpec((tm, n), lambda i: (i, 0)),
        out_shape=jax.ShapeDtypeStruct((m, n), F32),
        compiler_params=pltpu.CompilerParams(
            dimension_semantics=("parallel",),
            vmem_limit_bytes=_vmem_limit(blocks, _nbytes((k1 + k2, n), BF16))),
        name=name,
    )(a1, a2, w, w, res)


def _ret_kernel(q_ref, k_ref, v_ref, g_ref, cos_ref, sin_ref, gain_ref, lg_ref, o_ref,
                state_ref, dmat_ref, qd_ref, kd_ref, *, blk):
    n = pl.program_id(2)
    lg = lg_ref[0][:, 0:1]

    @pl.when(n == 0)
    def _():
        state_ref[...] = jnp.zeros_like(state_ref)
        i = lax.broadcasted_iota(jnp.int32, (blk, blk), 0)
        j = lax.broadcasted_iota(jnp.int32, (blk, blk), 1)
        ci = i >> CHUNK_SHIFT
        cj = j >> CHUNK_SHIFT
        d = (i - j).astype(F32)
        expo = jnp.where(ci == cj, jnp.abs(d), d)
        dmat_ref[...] = jnp.where(ci >= cj, jnp.exp(lg * expo), 0.0)
        r = lax.broadcasted_iota(jnp.int32, (blk, RET_QK_DIM), 0).astype(F32)
        qd_ref[...] = jnp.exp(lg * r)
        kd_ref[...] = jnp.exp(lg * (blk - r))

    half = RET_QK_DIM // 2
    c = cos_ref[0]
    s = sin_ref[0]

    def rope(x):
        x1 = x[:, :half]
        x2 = x[:, half:]
        return jnp.concatenate([x1 * c - x2 * s, x2 * c + x1 * s], axis=-1)

    q = rope(q_ref[0].astype(F32))
    k = rope(k_ref[0].astype(F32)) * (RET_QK_DIM ** -0.5)
    v = v_ref[0]
    state = state_ref[...]

    scores = lax.dot_general(q.astype(BF16), k.astype(BF16), (((1,), (1,)), ((), ())),
                             preferred_element_type=F32) * dmat_ref[...]
    out = jnp.dot(scores.astype(BF16), v, preferred_element_type=F32)
    out = out + jnp.dot((q * qd_ref[...]).astype(BF16), state.astype(BF16),
                        preferred_element_type=F32)
    kv = lax.dot_general((k * kd_ref[...]).astype(BF16), v, (((0,), (0,)), ((), ())),
                         preferred_element_type=F32)
    state_ref[...] = state * jnp.exp(lg * blk) + kv

    y = _rms(out, gain_ref[0])
    g = g_ref[0].astype(F32)
    o_ref[0] = (g / (1.0 + jnp.exp(-g)) * y).astype(o_ref.dtype)


def _retention(proj, cos_r, sin_r, ret_gain, *, blk=512):
    b, s, _ = proj.shape
    h = RET_HEADS
    dk, dv = RET_QK_DIM, RET_V_DIM
    v_off = 2 * h * dk // dv
    g_off = v_off + h
    log_g = jnp.log(1.0 - 2.0 ** (-5.0 - jnp.arange(h, dtype=F32)))
    log_g = jnp.broadcast_to(log_g[:, None, None], (h, 1, LANES))
    blocks = (2 * _nbytes((blk, dk), BF16) + 3 * _nbytes((blk, dv), BF16)
              + 2 * _nbytes((blk, LANES), F32))
    scratch = (_nbytes((dk, dv), F32) + _nbytes((blk, blk), F32) + 2 * _nbytes((blk, dk), F32))
    return pl.pallas_call(
        functools.partial(_ret_kernel, blk=blk),
        grid=(b, h, s // blk),
        in_specs=[
            pl.BlockSpec((1, blk, dk), lambda bi, hi, n: (bi, n, hi)),
            pl.BlockSpec((1, blk, dk), lambda bi, hi, n: (bi, n, h + hi)),
            pl.BlockSpec((1, blk, dv), lambda bi, hi, n: (bi, n, v_off + hi)),
            pl.BlockSpec((1, blk, dv), lambda bi, hi, n: (bi, n, g_off + hi)),
            pl.BlockSpec((1, blk, LANES), lambda bi, hi, n: (bi, n, 0)),
            pl.BlockSpec((1, blk, LANES), lambda bi, hi, n: (bi, n, 0)),
            pl.BlockSpec((1, 1, dv), lambda bi, hi, n: (hi, 0, 0)),
            pl.BlockSpec((1, 1, LANES), lambda bi, hi, n: (hi, 0, 0)),
        ],
        out_specs=pl.BlockSpec((1, blk, dv), lambda bi, hi, n: (bi, n, hi)),
        out_shape=jax.ShapeDtypeStruct((b, s, h * dv), BF16),
        scratch_shapes=[
            pltpu.VMEM((dk, dv), F32),
            pltpu.VMEM((blk, blk), F32),
            pltpu.VMEM((blk, dk), F32),
            pltpu.VMEM((blk, dk), F32),
        ],
        compiler_params=pltpu.CompilerParams(
            dimension_semantics=("parallel", "parallel", "arbitrary"),
            vmem_limit_bytes=_vmem_limit(blocks, scratch)),
        name="retention",
    )(proj, proj, proj, proj, cos_r, sin_r, ret_gain.reshape(h, 1, dv), log_g)


def _sg_kernel(u_ref, v_ref, gain_ref, w_ref, b_ref, o_ref, *, rows):
    i = lax.broadcasted_iota(jnp.int32, (SG_BLOCK, SG_BLOCK), 0)
    j = lax.broadcasted_iota(jnp.int32, (SG_BLOCK, SG_BLOCK), 1)
    mask = (j >> CHUNK_SHIFT) <= (i >> CHUNK_SHIFT)
    for g in range(SG_GROUPS):
        cols = slice(g * SG_DIM, (g + 1) * SG_DIM)
        w = jnp.where(mask, w_ref[g], 0.0).astype(BF16)
        bias = b_ref[g]
        u = _gelu_tanh(u_ref[0, :, cols].astype(F32))
        v = _rms(_gelu_tanh(v_ref[0, :, cols].astype(F32)), gain_ref[:, cols]).astype(BF16)
        for n in range(rows // SG_BLOCK):
            r = slice(n * SG_BLOCK, (n + 1) * SG_BLOCK)
            mixed = jnp.dot(w, v[r], preferred_element_type=F32) + bias
            o_ref[0, r, cols] = (u[r] * mixed).astype(o_ref.dtype)


def _spatial_gate(proj, sg_norm, sg_w, sg_b, *, rows=512):
    b, s, width = proj.shape
    sgw = SG_GROUPS * SG_DIM
    u_blk = (width - 2 * sgw) // sgw
    blocks = 3 * _nbytes((rows, sgw), BF16) + _nbytes(sg_w.shape, F32)
    return pl.pallas_call(
        functools.partial(_sg_kernel, rows=rows),
        grid=(b, s // rows),
        in_specs=[
            pl.BlockSpec((1, rows, sgw), lambda bi, n: (bi, n, u_blk)),
            pl.BlockSpec((1, rows, sgw), lambda bi, n: (bi, n, u_blk + 1)),
            pl.BlockSpec((1, sgw), lambda bi, n: (0, 0)),
            pl.BlockSpec((SG_GROUPS, SG_BLOCK, SG_BLOCK), lambda bi, n: (0, 0, 0)),
            pl.BlockSpec((SG_GROUPS, SG_BLOCK, 1), lambda bi, n: (0, 0, 0)),
        ],
        out_specs=pl.BlockSpec((1, rows, sgw), lambda bi, n: (bi, n, 0)),
        out_shape=jax.ShapeDtypeStruct((b, s, sgw), BF16),
        compiler_params=pltpu.CompilerParams(
            dimension_semantics=("parallel", "parallel"),
            vmem_limit_bytes=_vmem_limit(blocks)),
        name="spatial_gate",
    )(proj, proj, sg_norm.reshape(1, sgw), sg_w, sg_b.reshape(SG_GROUPS, SG_BLOCK, 1))


def _diff_attn_kernel(q0_ref, k0_ref, cos0_ref, sin0_ref, q1_ref, k1_ref, cos1_ref, sin1_ref,
                      q2_ref, k2_ref, cos2_ref, sin2_ref,
                      v_ref, qg_ref, kg_ref, sg_ref, lam_ref, o_ref,
                      qp_ref, kn_ref, bias_ref, sa_ref, sb_ref, m_ref, l_ref, acc_ref,
                      *, tq, nq, lambda_init):
    qi = pl.program_id(2)
    d = DIFF_HEAD_DIM
    n_lane_tiles = tq // LANES
    prep_rows = min(tq, 256)
    q_scale = d ** -0.5 * math.log2(math.e)

    def prepare(q_ref, k_ref, cos_ref, sin_ref, tile, qbuf):
        ii = lax.broadcasted_iota(jnp.int32, (2 * d, 2 * d), 0)
        jj = lax.broadcasted_iota(jnp.int32, (2 * d, 2 * d), 1)
        swap_halves = jnp.where((ii ^ (d // 2)) == jj, 1.0, 0.0).astype(BF16)

        def norm_rope(x, gain_cos, gain_sin):
            xf = x.astype(F32)
            rot = jnp.dot(x, swap_halves, preferred_element_type=F32)
            out = []
            for mi in range(2):
                lanes = slice(mi * d, (mi + 1) * d)
                xm = xf[:, lanes]
                r = lax.rsqrt(jnp.mean(xm * xm, axis=-1, keepdims=True) + NORM_EPS)
                out.append(r * (xm * gain_cos + rot[:, lanes] * gain_sin))
            return jnp.concatenate(out, axis=-1).astype(BF16)

        def rolled(g):
            return pltpu.roll(jnp.broadcast_to(g, (8, d)), d // 2, 1)[0:1]

        qg = qg_ref[...] * q_scale
        kg = kg_ref[...]
        qg_rot = rolled(qg)
        kg_rot = rolled(kg)
        base = pl.multiple_of(tile * tq, tq)
        for r0 in range(0, tq, prep_rows):
            rows = slice(r0, r0 + prep_rows)
            c = cos_ref[0, rows, :]
            s = sin_ref[0, rows, :]
            qp_ref[qbuf, rows, :] = norm_rope(q_ref[0, rows, :], qg * c, qg_rot * s)
            kn_ref[pl.ds(base + r0, prep_rows), :] = norm_rope(k_ref[0, rows, :], kg * c,
                                                               kg_rot * s)

    def scores(s_ref, slot, kv, width, masked=False):
        k0 = pl.multiple_of(kv * tq, tq)
        kt = kn_ref[pl.ds(k0, width * tq), :]
        for mi in range(2):
            lanes = slice(mi * d, (mi + 1) * d)
            s = lax.dot_general(qp_ref[slot, :, lanes], kt[:, lanes], (((1,), (1,)), ((), ())),
                                preferred_element_type=F32)
            if masked:
                s = s + bias_ref[...]
            for w in range(width):
                s_ref[mi, kv + w] = s[:, w * tq:(w + 1) * tq]
            m = m_ref[slot, mi]
            for c in range(width * n_lane_tiles):
                m = jnp.maximum(m, s[:, c * LANES:(c + 1) * LANES])
            m_ref[slot, mi] = m

    def finish_max(slot):
        for mi in range(2):
            m_row = jnp.max(m_ref[slot, mi], axis=-1, keepdims=True)
            m_ref[slot, mi] = jnp.broadcast_to(m_row, (tq, LANES))

    def weighted_values(s_ref, slot, kv, width):
        k0 = pl.multiple_of(kv * tq, tq)
        vt = v_ref[0, pl.ds(k0, width * tq), :]
        for mi in range(2):
            m = m_ref[slot, mi]
            l = l_ref[mi]
            ps = []
            for w in range(width):
                for c in range(n_lane_tiles):
                    p = jnp.exp2(s_ref[mi, kv + w, :, c * LANES:(c + 1) * LANES] - m)
                    l = l + p
                    ps.append(p.astype(BF16))
            l_ref[mi] = l
            acc_ref[mi] += jnp.dot(jnp.concatenate(ps, axis=-1), vt,
                                   preferred_element_type=F32)

    def run(cur_ref, nxt_ref, slot):
        nslot = 1 - slot
        odd_tile_count = slot == 0
        n_pairs = (qi + 1) >> 1
        l_ref[...] = jnp.zeros_like(l_ref)
        acc_ref[...] = jnp.zeros_like(acc_ref)

        @pl.when(qi < nq - 1)
        def _():
            m_ref[nslot] = jnp.full(m_ref.shape[1:], MASK_VALUE, F32)

            def body(pair, carry):
                weighted_values(cur_ref, slot, 2 * pair, 2)
                scores(nxt_ref, nslot, 2 * pair, 2)
                return carry

            lax.fori_loop(0, n_pairs, body, 0)

        def finish_next_tile(prepare_ahead):
            if odd_tile_count:
                weighted_values(cur_ref, slot, qi, 1)
                scores(nxt_ref, nslot, qi, 1)
            if prepare_ahead:
                prepare(q2_ref, k2_ref, cos2_ref, sin2_ref, qi + 2, slot)
            scores(nxt_ref, nslot, qi + 1, 1, masked=True)
            finish_max(nslot)

        @pl.when(qi < nq - 2)
        def _():
            finish_next_tile(True)

        if (nq - 2) % 2 == slot:
            @pl.when(qi == nq - 2)
            def _():
                finish_next_tile(False)

        if (nq - 1) % 2 == slot:
            @pl.when(qi == nq - 1)
            def _():
                def body(pair, carry):
                    weighted_values(cur_ref, slot, 2 * pair, 2)
                    return carry

                lax.fori_loop(0, n_pairs, body, 0)
                if odd_tile_count:
                    weighted_values(cur_ref, slot, qi, 1)

    @pl.when(qi == 0)
    def _():
        row = lax.broadcasted_iota(jnp.int32, (tq, tq), 0)
        col = lax.broadcasted_iota(jnp.int32, (tq, tq), 1)
        bias_ref[...] = jnp.where((col >> CHUNK_SHIFT) <= (row >> CHUNK_SHIFT), 0.0, MASK_VALUE)
        prepare(q0_ref, k0_ref, cos0_ref, sin0_ref, 0, 0)
        prepare(q1_ref, k1_ref, cos1_ref, sin1_ref, 1, 1)
        m_ref[0] = jnp.full(m_ref.shape[1:], MASK_VALUE, F32)
        scores(sa_ref, 0, 0, 1, masked=True)
        finish_max(0)

    @pl.when((qi & 1) == 0)
    def _():
        run(sa_ref, sb_ref, 0)

    @pl.when((qi & 1) == 1)
    def _():
        run(sb_ref, sa_ref, 1)

    lam_p = lam_ref[...]
    lam = (jnp.exp(jnp.sum(lam_p[0:1] * lam_p[1:2], axis=-1, keepdims=True))
           - jnp.exp(jnp.sum(lam_p[2:3] * lam_p[3:4], axis=-1, keepdims=True))
           + lambda_init)
    l0 = jnp.sum(l_ref[0], axis=-1, keepdims=True)
    l1 = jnp.sum(l_ref[1], axis=-1, keepdims=True)
    out = acc_ref[0] / l0 - lam * (acc_ref[1] / l1)
    o_ref[0] = (_rms(out, sg_ref[...]) * (1.0 - lambda_init)).astype(o_ref.dtype)


def _diff_attention(proj, cos_d, sin_d, q_gain, k_gain, sub_gain, lam_params, lambda_init,
                    *, tq=512):
    b, s, _ = proj.shape
    h = DIFF_HEADS
    d = DIFF_HEAD_DIM
    hw = 2 * d
    nq = s // tq
    assert nq >= 2
    blocks = (7 * _nbytes((tq, hw), BF16) + 6 * _nbytes((tq, d), F32) + _nbytes((s, hw), BF16))
    scratch = (_nbytes((2, tq, hw), BF16) + _nbytes((s, hw), BF16) + 2 * _nbytes((2, tq, s), F32)
               + _nbytes((2, 2, tq, LANES), F32) + _nbytes((2, tq, LANES), F32)
               + _nbytes((2, tq, hw), F32))
    fixed = lambda tile, col: (lambda bi, hi, qi: (bi, tile, col(hi)))
    ahead = lambda col: (lambda bi, hi, qi: (bi, jnp.minimum(qi + 2, nq - 1), col(hi)))
    q_col = lambda hi: hi
    k_col = lambda hi: h + hi
    rope_col = lambda hi: 0
    tile_specs = lambda index: [
        pl.BlockSpec((1, tq, hw), index(q_col)),
        pl.BlockSpec((1, tq, hw), index(k_col)),
        pl.BlockSpec((1, tq, d), index(rope_col)),
        pl.BlockSpec((1, tq, d), index(rope_col)),
    ]
    tile_args = (proj, proj, cos_d, sin_d)
    return pl.pallas_call(
        functools.partial(_diff_attn_kernel, tq=tq, nq=nq, lambda_init=lambda_init),
        grid=(b, h, nq),
        in_specs=[
            *tile_specs(functools.partial(fixed, 0)),
            *tile_specs(functools.partial(fixed, 1)),
            *tile_specs(ahead),
            pl.BlockSpec((1, s, hw), lambda bi, hi, qi: (bi, 0, 2 * h + hi)),
            pl.BlockSpec((1, d), lambda bi, hi, qi: (0, 0)),
            pl.BlockSpec((1, d), lambda bi, hi, qi: (0, 0)),
            pl.BlockSpec((1, hw), lambda bi, hi, qi: (0, 0)),
            pl.BlockSpec((4, d), lambda bi, hi, qi: (0, 0)),
        ],
        out_specs=pl.BlockSpec((1, tq, hw), lambda bi, hi, qi: (bi, qi, hi)),
        out_shape=jax.ShapeDtypeStruct((b, s, h * hw), BF16),
        scratch_shapes=[
            pltpu.VMEM((2, tq, hw), BF16),
            pltpu.VMEM((s, hw), BF16),
            pltpu.VMEM((tq, tq), F32),
            pltpu.VMEM((2, nq, tq, tq), F32),
            pltpu.VMEM((2, nq, tq, tq), F32),
            pltpu.VMEM((2, 2, tq, LANES), F32),
            pltpu.VMEM((2, tq, LANES), F32),
            pltpu.VMEM((2, tq, hw), F32),
        ],
        compiler_params=pltpu.CompilerParams(
            dimension_semantics=("parallel", "parallel", "arbitrary"),
            vmem_limit_bytes=_vmem_limit(blocks, scratch)),
        name="diff_attention",
    )(*tile_args, *tile_args, *tile_args, proj,
      q_gain.reshape(1, d), k_gain.reshape(1, d), sub_gain.reshape(1, hw), lam_params)


def _xa_kernel(x_ref, g_ref, wq_ref, kv_ref, qg_ref, kg_ref, wo_ref, o_ref, kn_ref, *, heads):
    dm = x_ref.shape[2]
    hd = dm // heads

    @pl.when(pl.program_id(1) == 0)
    def _():
        for h in range(heads):
            cols = slice(h * hd, (h + 1) * hd)
            kn_ref[:, cols] = _rms(kv_ref[0, :, cols].astype(F32), kg_ref[...]).astype(BF16)

    x = x_ref[0]
    r = lax.rsqrt(jnp.mean(x * x, axis=-1, keepdims=True) + NORM_EPS)
    q = jnp.dot((x * g_ref[...]).astype(BF16), wq_ref[...], preferred_element_type=F32)
    scale = hd ** -0.5
    heads_out = []
    for h in range(heads):
        cols = slice(h * hd, (h + 1) * hd)
        qn = (_rms(q[:, cols] * r, qg_ref[...]) * scale).astype(BF16)
        s = lax.dot_general(qn, kn_ref[:, cols], (((1,), (1,)), ((), ())),
                            preferred_element_type=F32)
        e = jnp.exp(s - jnp.max(s, axis=-1, keepdims=True))
        p = e / jnp.sum(e, axis=-1, keepdims=True)
        v = kv_ref[0, :, dm + h * hd: dm + (h + 1) * hd]
        heads_out.append(jnp.dot(p.astype(BF16), v, preferred_element_type=F32).astype(BF16))
    att = jnp.concatenate(heads_out, axis=-1)
    o_ref[0] = x + jnp.dot(att, wo_ref[...], preferred_element_type=F32)


def _cross_attention_block(x, gain, wq, wo, layer, kv, q_gain, k_gain, *, tm=512):
    b, s, dm = x.shape
    mlen = kv.shape[1]
    hd = dm // XA_HEADS
    resident = pl.Buffered(buffer_count=1)
    blocks = 2 * _nbytes((tm, dm), F32) + _nbytes((mlen, 2 * dm), BF16)
    scratch = 2 * _nbytes((dm, dm), BF16) + _nbytes((mlen, dm), BF16)
    return pl.pallas_call(
        functools.partial(_xa_kernel, heads=XA_HEADS),
        grid=(b, s // tm),
        in_specs=[
            pl.BlockSpec((1, tm, dm), lambda bi, i: (bi, i, 0)),
            pl.BlockSpec((1, dm), lambda bi, i: (0, 0)),
            pl.BlockSpec((None, dm, dm), lambda bi, i: (layer, 0, 0), pipeline_mode=resident),
            pl.BlockSpec((1, mlen, 2 * dm), lambda bi, i: (bi, 0, 0)),
            pl.BlockSpec((1, hd), lambda bi, i: (0, 0)),
            pl.BlockSpec((1, hd), lambda bi, i: (0, 0)),
            pl.BlockSpec((None, dm, dm), lambda bi, i: (layer, 0, 0), pipeline_mode=resident),
        ],
        out_specs=pl.BlockSpec((1, tm, dm), lambda bi, i: (bi, i, 0)),
        out_shape=jax.ShapeDtypeStruct((b, s, dm), F32),
        scratch_shapes=[pltpu.VMEM((mlen, dm), BF16)],
        compiler_params=pltpu.CompilerParams(
            dimension_semantics=("parallel", "arbitrary"),
            vmem_limit_bytes=_vmem_limit(blocks, scratch)),
        name="cross_attention_block",
    )(x, gain.reshape(1, dm), wq, kv, q_gain.reshape(1, hd), k_gain.reshape(1, hd), wo)


def kernel(x, mem, positions, norm_mix, norm_xa, norm_mem, norm_ffn, ev_w_in, ev_ret_gain,
           ev_sg_norm, ev_sg_w, ev_sg_b, ev_w_out, od_w_qkv, od_q_gain, od_k_gain, od_lam_q1,
           od_lam_k1, od_lam_q2, od_lam_k2, od_sub_gain, od_w_o, xa_w_q, xa_w_kv, xa_q_gain,
           xa_k_gain, xa_w_o, ffn_w1, ffn_w2):
    b, s, dm = x.shape
    mlen = mem.shape[1]
    depth = norm_mix.shape[0]
    m = b * s
    ev_w_in, ev_w_out, od_w_qkv, od_w_o, xa_w_q, xa_w_o, ffn_w1, ffn_w2 = (
        w.astype(BF16) for w in (ev_w_in, ev_w_out, od_w_qkv, od_w_o, xa_w_q, xa_w_o,
                                 ffn_w1, ffn_w2))

    cos_r, sin_r, cos_d, sin_d = _rope_tables(positions)
    xf = x.reshape(m, dm)
    mem_f = mem.reshape(b * mlen, dm)

    for li in range(depth):
        if li % 2 == 0:
            e = li // 2
            proj = _norm_mm(xf, norm_mix[li], ev_w_in, e, name="even_in_proj")
            proj = proj.reshape(b, s, -1)
            out_a = _retention(proj, cos_r, sin_r, ev_ret_gain[e])
            out_b = _spatial_gate(proj, ev_sg_norm[e], ev_sg_w[e], ev_sg_b[e])
            xf = _mm2_res(out_a.reshape(m, -1), out_b.reshape(m, -1), ev_w_out, e, xf,
                          name="even_out_proj")
        else:
            o = li // 2
            lambda_init = 0.8 - 0.6 * math.exp(-0.3 * li)
            proj = _norm_mm(xf, norm_mix[li], od_w_qkv, o, name="diff_qkv_proj")
            lam_params = jnp.stack([od_lam_q1[o], od_lam_k1[o], od_lam_q2[o], od_lam_k2[o]])
            att = _diff_attention(proj.reshape(b, s, -1), cos_d, sin_d, od_q_gain[o],
                                  od_k_gain[o], od_sub_gain[o], lam_params, lambda_init)
            xf = _mm_res_full(att.reshape(m, -1), od_w_o, o, xf, name="diff_out_proj")

        kv = _norm_mm(mem_f, norm_mem[li], xa_w_kv, li, tn=1024, name="xa_kv_proj")
        xf = _cross_attention_block(xf.reshape(b, s, dm), norm_xa[li], xa_w_q, xa_w_o, li,
                                    kv.reshape(b, mlen, 2 * dm), xa_q_gain[li],
                                    xa_k_gain[li]).reshape(m, dm)

        hid = _norm_mm(xf, norm_ffn[li], ffn_w1, li, act="relu2", name="ffn_up")
        xf = _mm_res(hid, ffn_w2, li, xf, name="ffn_down")

    return xf.reshape(b, s, dm)
```

```python
import functools
import math

import jax
import jax.numpy as jnp
from jax import lax
from jax.experimental import pallas as pl
from jax.experimental.pallas import tpu as pltpu

F32 = jnp.float32
BF16 = jnp.bfloat16

CHUNK = 64
CHUNK_SHIFT = CHUNK.bit_length() - 1
ROPE_THETA = 10000.0
NORM_EPS = 1e-6
RET_HEADS = 4
RET_QK_DIM = 256
RET_V_DIM = 512
SG_GROUPS = 4
SG_DIM = 256
SG_BLOCK = 128
DIFF_HEADS = 8
DIFF_HEAD_DIM = 128
XA_HEADS = 4
MASK_VALUE = -1e30

V7X_VMEM_BYTES = 64 * 1024 * 1024
VMEM_REQUEST_CAP = V7X_VMEM_BYTES - 8 * 1024 * 1024
LANES = 128


def _vmem_limit(block_bytes, scratch_bytes=0):
    need = 2 * block_bytes + scratch_bytes
    return int(min(VMEM_REQUEST_CAP, max(32 * 1024 * 1024, 2 * need)))


def _nbytes(shape, dtype):
    return math.prod(shape) * jnp.dtype(dtype).itemsize


def _rms(x, gain):
    ms = jnp.mean(x * x, axis=-1, keepdims=True)
    return x * lax.rsqrt(ms + NORM_EPS) * gain


def _gelu_tanh(x):
    c = math.sqrt(2.0 / math.pi)
    return 0.5 * x * (1.0 + jnp.tanh(c * (x + 0.044715 * (x * x * x))))


class _Casts:
    def __init__(self, weights, steps, step_of):
        self.weights = weights
        self.n = len(weights)
        self.args, self.in_specs, self.out_specs, self.out_shapes = [], [], [], []
        self.block_bytes = 0
        for w in weights:
            flat = w.reshape(-1, w.shape[-1])
            rows = flat.shape[0] // steps
            assert rows * steps == flat.shape[0] and rows % 16 == 0
            block = (rows, flat.shape[1])
            index = lambda *grid_ids: (step_of(*grid_ids), 0)
            self.args.append(flat)
            self.in_specs.append(pl.BlockSpec(block, index))
            self.out_specs.append(pl.BlockSpec(block, index))
            self.out_shapes.append(jax.ShapeDtypeStruct(flat.shape, BF16))
            self.block_bytes += _nbytes(block, F32) + _nbytes(block, BF16)

    @staticmethod
    def run(src_refs, dst_refs):
        for src, dst in zip(src_refs, dst_refs):
            dst[...] = src[...].astype(dst.dtype)

    def finish(self, outs):
        return [o.reshape(w.shape) for o, w in zip(outs, self.weights)]


def _rope_kernel(pos_ref, inv_r_ref, inv_d_ref, cr_ref, sr_ref, cd_ref, sd_ref):
    pos = pos_ref[...].astype(F32)
    ang_r = pos * inv_r_ref[...]
    cr_ref[...] = jnp.cos(ang_r)
    sr_ref[...] = jnp.sin(ang_r)
    ang_d = pos * inv_d_ref[...]
    lane = lax.broadcasted_iota(jnp.int32, ang_d.shape, 1)
    sin_d = jnp.sin(ang_d)
    cd_ref[...] = jnp.cos(ang_d)
    sd_ref[...] = jnp.where(lane < DIFF_HEAD_DIM // 2, -sin_d, sin_d)


def _rope_tables(positions):
    b, s = positions.shape
    rows = 1024
    n = b * s
    inv_r = ROPE_THETA ** (-jnp.arange(0, RET_QK_DIM, 2, dtype=F32) / RET_QK_DIM)
    inv_d = ROPE_THETA ** (-jnp.arange(0, DIFF_HEAD_DIM, 2, dtype=F32) / DIFF_HEAD_DIM)
    inv_d = jnp.concatenate([inv_d, inv_d])
    out = jax.ShapeDtypeStruct((n, LANES), F32)
    tab_spec = pl.BlockSpec((rows, LANES), lambda i: (i, 0))
    vec_spec = pl.BlockSpec((1, LANES), lambda i: (0, 0))
    outs = pl.pallas_call(
        _rope_kernel,
        grid=(n // rows,),
        in_specs=[pl.BlockSpec((rows, 1), lambda i: (i, 0)), vec_spec, vec_spec],
        out_specs=[tab_spec] * 4,
        out_shape=[out] * 4,
        name="rope_tables",
    )(positions.reshape(n, 1), inv_r.reshape(1, LANES), inv_d.reshape(1, LANES))
    return [t.reshape(b, s, LANES) for t in outs]


def _norm_mm_kernel(*refs, act, n_cast):
    x_ref, g_ref, w_ref = refs[:3]
    cast_src = refs[3:3 + n_cast]
    o_ref = refs[3 + n_cast]
    cast_dst = refs[4 + n_cast:4 + 2 * n_cast]
    xg_ref, r_ref = refs[4 + 2 * n_cast:]

    def finish(y):
        r = r_ref[...]
        for c in range(y.shape[1] // LANES):
            lanes = slice(c * LANES, (c + 1) * LANES)
            z = y[:, lanes] * r
            if act == "relu2":
                z = jnp.maximum(z, 0.0)
                z = z * z
            o_ref[:, lanes] = z.astype(o_ref.dtype)

    @pl.when(pl.program_id(1) == 0)
    def _():
        x = x_ref[...]
        xg = (x * g_ref[...]).astype(BF16)
        xg_ref[...] = xg
        ms = jnp.mean(x * x, axis=-1, keepdims=True)
        r_ref[...] = jnp.broadcast_to(lax.rsqrt(ms + NORM_EPS), r_ref.shape)
        _Casts.run(cast_src, cast_dst)
        finish(jnp.dot(xg, w_ref[...].astype(BF16), preferred_element_type=F32))

    @pl.when(pl.program_id(1) != 0)
    def _():
        _Casts.run(cast_src, cast_dst)
        finish(jnp.dot(xg_ref[...], w_ref[...].astype(BF16), preferred_element_type=F32))


def _norm_mm(x, gain, w, layer, *, act=None, tm=1024, tn=2048, cast_weights=(), name):
    m, k = x.shape
    n = w.shape[2]
    tm = min(tm, m)
    assert m % tm == 0 and n % tn == 0
    nj = n // tn
    casts = _Casts(cast_weights, (m // tm) * nj, lambda i, j: i * nj + j)
    blocks = (_nbytes((tm, k), F32) + _nbytes((k, tn), w.dtype) + _nbytes((tm, tn), BF16)
              + _nbytes((1, k), F32) + casts.block_bytes)
    out, *cast_out = pl.pallas_call(
        functools.partial(_norm_mm_kernel, act=act, n_cast=casts.n),
        grid=(m // tm, nj),
        in_specs=[
            pl.BlockSpec((tm, k), lambda i, j: (i, 0)),
            pl.BlockSpec((1, k), lambda i, j: (0, 0)),
            pl.BlockSpec((None, k, tn), lambda i, j: (layer, 0, j)),
            *casts.in_specs,
        ],
        out_specs=[pl.BlockSpec((tm, tn), lambda i, j: (i, j)), *casts.out_specs],
        out_shape=[jax.ShapeDtypeStruct((m, n), BF16), *casts.out_shapes],
        scratch_shapes=[pltpu.VMEM((tm, k), BF16), pltpu.VMEM((tm, LANES), F32)],
        compiler_params=pltpu.CompilerParams(
            dimension_semantics=("parallel", "arbitrary"),
            vmem_limit_bytes=_vmem_limit(blocks, _nbytes((tm, k), BF16))),
        name=name,
    )(x, gain.reshape(1, k), w, *casts.args)
    return (out, *casts.finish(cast_out)) if casts.n else out


def _mm_res_kernel(a_ref, w_ref, r_ref, o_ref):
    @pl.when(pl.program_id(2) == 0)
    def _():
        o_ref[...] = r_ref[...] + jnp.dot(a_ref[...], w_ref[...], preferred_element_type=F32)

    @pl.when(pl.program_id(2) != 0)
    def _():
        o_ref[...] += jnp.dot(a_ref[...], w_ref[...], preferred_element_type=F32)


def _mm_res_full_kernel(a_ref, w_ref, r_ref, o_ref):
    o_ref[...] = r_ref[...] + jnp.dot(a_ref[...], w_ref[...], preferred_element_type=F32)


def _mm_res_full(a, w, layer, res, *, tm=1024, name):
    m, kdim = a.shape
    n = w.shape[2]
    assert m % tm == 0
    blocks = _nbytes((tm, kdim), BF16) + 2 * _nbytes((tm, n), F32)
    return pl.pallas_call(
        _mm_res_full_kernel,
        grid=(m // tm,),
        in_specs=[
            pl.BlockSpec((tm, kdim), lambda i: (i, 0)),
            pl.BlockSpec((None, kdim, n), lambda i: (layer, 0, 0),
                         pipeline_mode=pl.Buffered(buffer_count=1)),
            pl.BlockSpec((tm, n), lambda i: (i, 0)),
        ],
        out_specs=pl.BlockSpec((tm, n), lambda i: (i, 0)),
        out_shape=jax.ShapeDtypeStruct((m, n), F32),
        compiler_params=pltpu.CompilerParams(
            dimension_semantics=("parallel",),
            vmem_limit_bytes=_vmem_limit(blocks, _nbytes((kdim, n), BF16))),
        name=name,
    )(a, w, res)


def _mm_res(a, w, layer, res, *, tm=1024, tn=1024, tk=4096, name):
    m, kdim = a.shape
    n = w.shape[2]
    tk = min(tk, kdim)
    assert m % tm == 0 and n % tn == 0 and kdim % tk == 0
    nk = kdim // tk
    blocks = (_nbytes((tm, tk), BF16) + _nbytes((tk, tn), BF16) + 2 * _nbytes((tm, tn), F32))
    return pl.pallas_call(
        _mm_res_kernel,
        grid=(m // tm, n // tn, nk),
        in_specs=[
            pl.BlockSpec((tm, tk), lambda i, j, k: (i, k)),
            pl.BlockSpec((None, tk, tn), lambda i, j, k: (layer, k, j)),
            pl.BlockSpec((tm, tn), lambda i, j, k: (i, j)),
        ],
        out_specs=pl.BlockSpec((tm, tn), lambda i, j, k: (i, j)),
        out_shape=jax.ShapeDtypeStruct((m, n), F32),
        compiler_params=pltpu.CompilerParams(
            dimension_semantics=("parallel", "parallel", "arbitrary"),
            vmem_limit_bytes=_vmem_limit(blocks)),
        name=name,
    )(a, w, res)


def _mm2_res_kernel(*refs, n_cast):
    a1_ref, a2_ref, w1_ref, w2_ref, r_ref = refs[:5]
    o_ref = refs[5 + n_cast]
    _Casts.run(refs[5:5 + n_cast], refs[6 + n_cast:])
    acc = r_ref[...] + jnp.dot(a1_ref[...], w1_ref[...], preferred_element_type=F32)
    o_ref[...] = acc + jnp.dot(a2_ref[...], w2_ref[...], preferred_element_type=F32)


def _mm2_res(a1, a2, w, layer, res, *, tm=512, cast_weights=(), name):
    m, k1 = a1.shape
    k2 = a2.shape[1]
    n = w.shape[2]
    assert w.shape[1] == k1 + k2 and k1 % k2 == 0
    assert m % tm == 0
    resident = pl.Buffered(buffer_count=1)
    casts = _Casts(cast_weights, m // tm, lambda i: i)
    blocks = _nbytes((tm, k1 + k2), BF16) + 2 * _nbytes((tm, n), F32) + casts.block_bytes
    out, *cast_out = pl.pallas_call(
        functools.partial(_mm2_res_kernel, n_cast=casts.n),
        grid=(m // tm,),
        in_specs=[
            pl.BlockSpec((tm, k1), lambda i: (i, 0)),
            pl.BlockSpec((tm, k2), lambda i: (i, 0)),
            pl.BlockSpec((None, k1, n), lambda i: (layer, 0, 0), pipeline_mode=resident),
            pl.BlockSpec((None, k2, n), lambda i: (layer, k1 // k2, 0), pipeline_mode=resident),
            pl.BlockSpec((tm, n), lambda i: (i, 0)),
            *casts.in_specs,
        ],
        out_specs=[pl.BlockSpec((tm, n), lambda i: (i, 0)), *casts.out_specs],
        out_shape=[jax.ShapeDtypeStruct((m, n), F32), *casts.out_shapes],
        compiler_params=pltpu.CompilerParams(
            dimension_semantics=("parallel",),
            vmem_limit_bytes=_vmem_limit(blocks, _nbytes((k1 + k2, n), BF16))),
        name=name,
    )(a1, a2, w, w, res, *casts.args)
    return (out, *casts.finish(cast_out)) if casts.n else out


def _ret_kernel(q_ref, k_ref, v_ref, g_ref, cos_ref, sin_ref, gain_ref, lg_ref, o_ref,
                state_ref, dmat_ref, qd_ref, kd_ref, *, blk):
    n = pl.program_id(2)
    lg = lg_ref[0][:, 0:1]

    @pl.when(n == 0)
    def _():
        state_ref[...] = jnp.zeros_like(state_ref)
        i = lax.broadcasted_iota(jnp.int32, (blk, blk), 0)
        j = lax.broadcasted_iota(jnp.int32, (blk, blk), 1)
        ci = i >> CHUNK_SHIFT
        cj = j >> CHUNK_SHIFT
        d = (i - j).astype(F32)
        expo = jnp.where(ci == cj, jnp.abs(d), d)
        dmat_ref[...] = jnp.where(ci >= cj, jnp.exp(lg * expo), 0.0)
        r = lax.broadcasted_iota(jnp.int32, (blk, RET_QK_DIM), 0).astype(F32)
        qd_ref[...] = jnp.exp(lg * r)
        kd_ref[...] = jnp.exp(lg * (blk - r))

    half = RET_QK_DIM // 2
    c = cos_ref[0]
    s = sin_ref[0]

    def rope(x):
        x1 = x[:, :half]
        x2 = x[:, half:]
        return jnp.concatenate([x1 * c - x2 * s, x2 * c + x1 * s], axis=-1)

    q = rope(q_ref[0].astype(F32))
    k = rope(k_ref[0].astype(F32)) * (RET_QK_DIM ** -0.5)
    v = v_ref[0]
    state = state_ref[...]

    scores = lax.dot_general(q.astype(BF16), k.astype(BF16), (((1,), (1,)), ((), ())),
                             preferred_element_type=F32) * dmat_ref[...]
    out = jnp.dot(scores.astype(BF16), v, preferred_element_type=F32)
    out = out + jnp.dot((q * qd_ref[...]).astype(BF16), state.astype(BF16),
                        preferred_element_type=F32)
    kv = lax.dot_general((k * kd_ref[...]).astype(BF16), v, (((0,), (0,)), ((), ())),
                         preferred_element_type=F32)
    state_ref[...] = state * jnp.exp(lg * blk) + kv

    y = _rms(out, gain_ref[0])
    g = g_ref[0].astype(F32)
    o_ref[0] = (g / (1.0 + jnp.exp(-g)) * y).astype(o_ref.dtype)


def _retention(proj, cos_r, sin_r, ret_gain, *, blk=512):
    b, s, _ = proj.shape
    h = RET_HEADS
    dk, dv = RET_QK_DIM, RET_V_DIM
    v_off = 2 * h * dk // dv
    g_off = v_off + h
    log_g = jnp.log(1.0 - 2.0 ** (-5.0 - jnp.arange(h, dtype=F32)))
    log_g = jnp.broadcast_to(log_g[:, None, None], (h, 1, LANES))
    blocks = (2 * _nbytes((blk, dk), BF16) + 3 * _nbytes((blk, dv), BF16)
              + 2 * _nbytes((blk, LANES), F32))
    scratch = (_nbytes((dk, dv), F32) + _nbytes((blk, blk), F32) + 2 * _nbytes((blk, dk), F32))
    return pl.pallas_call(
        functools.partial(_ret_kernel, blk=blk),
        grid=(b, h, s // blk),
        in_specs=[
            pl.BlockSpec((1, blk, dk), lambda bi, hi, n: (bi, n, hi)),
            pl.BlockSpec((1, blk, dk), lambda bi, hi, n: (bi, n, h + hi)),
            pl.BlockSpec((1, blk, dv), lambda bi, hi, n: (bi, n, v_off + hi)),
            pl.BlockSpec((1, blk, dv), lambda bi, hi, n: (bi, n, g_off + hi)),
            pl.BlockSpec((1, blk, LANES), lambda bi, hi, n: (bi, n, 0)),
            pl.BlockSpec((1, blk, LANES), lambda bi, hi, n: (bi, n, 0)),
            pl.BlockSpec((1, 1, dv), lambda bi, hi, n: (hi, 0, 0)),
            pl.BlockSpec((1, 1, LANES), lambda bi, hi, n: (hi, 0, 0)),
        ],
        out_specs=pl.BlockSpec((1, blk, dv), lambda bi, hi, n: (bi, n, hi)),
        out_shape=jax.ShapeDtypeStruct((b, s, h * dv), BF16),
        scratch_shapes=[
            pltpu.VMEM((dk, dv), F32),
            pltpu.VMEM((blk, blk), F32),
            pltpu.VMEM((blk, dk), F32),
            pltpu.VMEM((blk, dk), F32),
        ],
        compiler_params=pltpu.CompilerParams(
            dimension_semantics=("parallel", "parallel", "arbitrary"),
            vmem_limit_bytes=_vmem_limit(blocks, scratch)),
        name="retention",
    )(proj, proj, proj, proj, cos_r, sin_r, ret_gain.reshape(h, 1, dv), log_g)


def _sg_kernel(u_ref, v_ref, gain_ref, w_ref, b_ref, o_ref, *, rows):
    i = lax.broadcasted_iota(jnp.int32, (SG_BLOCK, SG_BLOCK), 0)
    j = lax.broadcasted_iota(jnp.int32, (SG_BLOCK, SG_BLOCK), 1)
    mask = (j >> CHUNK_SHIFT) <= (i >> CHUNK_SHIFT)
    for g in range(SG_GROUPS):
        cols = slice(g * SG_DIM, (g + 1) * SG_DIM)
        w = jnp.where(mask, w_ref[g], 0.0).astype(BF16)
        bias = b_ref[g]
        u = _gelu_tanh(u_ref[0, :, cols].astype(F32))
        v = _rms(_gelu_tanh(v_ref[0, :, cols].astype(F32)), gain_ref[:, cols]).astype(BF16)
        for n in range(rows // SG_BLOCK):
            r = slice(n * SG_BLOCK, (n + 1) * SG_BLOCK)
            mixed = jnp.dot(w, v[r], preferred_element_type=F32) + bias
            o_ref[0, r, cols] = (u[r] * mixed).astype(o_ref.dtype)


def _spatial_gate(proj, sg_norm, sg_w, sg_b, *, rows=512):
    b, s, width = proj.shape
    sgw = SG_GROUPS * SG_DIM
    u_blk = (width - 2 * sgw) // sgw
    blocks = 3 * _nbytes((rows, sgw), BF16) + _nbytes(sg_w.shape, F32)
    return pl.pallas_call(
        functools.partial(_sg_kernel, rows=rows),
        grid=(b, s // rows),
        in_specs=[
            pl.BlockSpec((1, rows, sgw), lambda bi, n: (bi, n, u_blk)),
            pl.BlockSpec((1, rows, sgw), lambda bi, n: (bi, n, u_blk + 1)),
            pl.BlockSpec((1, sgw), lambda bi, n: (0, 0)),
            pl.BlockSpec((SG_GROUPS, SG_BLOCK, SG_BLOCK), lambda bi, n: (0, 0, 0)),
            pl.BlockSpec((SG_GROUPS, SG_BLOCK, 1), lambda bi, n: (0, 0, 0)),
        ],
        out_specs=pl.BlockSpec((1, rows, sgw), lambda bi, n: (bi, n, 0)),
        out_shape=jax.ShapeDtypeStruct((b, s, sgw), BF16),
        compiler_params=pltpu.CompilerParams(
            dimension_semantics=("parallel", "parallel"),
            vmem_limit_bytes=_vmem_limit(blocks)),
        name="spatial_gate",
    )(proj, proj, sg_norm.reshape(1, sgw), sg_w, sg_b.reshape(SG_GROUPS, SG_BLOCK, 1))


def _diff_attn_kernel(q0_ref, k0_ref, cos0_ref, sin0_ref, q1_ref, k1_ref, cos1_ref, sin1_ref,
                      q2_ref, k2_ref, cos2_ref, sin2_ref,
                      v_ref, qg_ref, kg_ref, sg_ref, lam_ref, o_ref,
                      qp_ref, kn_ref, bias_ref, sa_ref, sb_ref, m_ref, l_ref, acc_ref,
                      *, tq, nq, lambda_init):
    qi = pl.program_id(2)
    d = DIFF_HEAD_DIM
    n_lane_tiles = tq // LANES
    prep_rows = min(tq, 256)
    q_scale = d ** -0.5 * math.log2(math.e)

    def prepare(q_ref, k_ref, cos_ref, sin_ref, tile, qbuf):
        ii = lax.broadcasted_iota(jnp.int32, (2 * d, 2 * d), 0)
        jj = lax.broadcasted_iota(jnp.int32, (2 * d, 2 * d), 1)
        swap_halves = jnp.where((ii ^ (d // 2)) == jj, 1.0, 0.0).astype(BF16)

        def norm_rope(x, gain_cos, gain_sin):
            xf = x.astype(F32)
            rot = jnp.dot(x, swap_halves, preferred_element_type=F32)
            out = []
            for mi in range(2):
                lanes = slice(mi * d, (mi + 1) * d)
                xm = xf[:, lanes]
                r = lax.rsqrt(jnp.mean(xm * xm, axis=-1, keepdims=True) + NORM_EPS)
                out.append(r * (xm * gain_cos + rot[:, lanes] * gain_sin))
            return jnp.concatenate(out, axis=-1).astype(BF16)

        def rolled(g):
            return pltpu.roll(jnp.broadcast_to(g, (8, d)), d // 2, 1)[0:1]

        qg = qg_ref[...] * q_scale
        kg = kg_ref[...]
        qg_rot = rolled(qg)
        kg_rot = rolled(kg)
        base = pl.multiple_of(tile * tq, tq)
        for r0 in range(0, tq, prep_rows):
            rows = slice(r0, r0 + prep_rows)
            c = cos_ref[0, rows, :]
            s = sin_ref[0, rows, :]
            qp_ref[qbuf, rows, :] = norm_rope(q_ref[0, rows, :], qg * c, qg_rot * s)
            kn_ref[pl.ds(base + r0, prep_rows), :] = norm_rope(k_ref[0, rows, :], kg * c,
                                                               kg_rot * s)

    def scores(s_ref, slot, kv, width, masked=False):
        k0 = pl.multiple_of(kv * tq, tq)
        kt = kn_ref[pl.ds(k0, width * tq), :]
        for mi in range(2):
            lanes = slice(mi * d, (mi + 1) * d)
            s = lax.dot_general(qp_ref[slot, :, lanes], kt[:, lanes], (((1,), (1,)), ((), ())),
                                preferred_element_type=F32)
            if masked:
                s = s + bias_ref[...]
            for w in range(width):
                s_ref[mi, kv + w] = s[:, w * tq:(w + 1) * tq]
            m = m_ref[slot, mi]
            for c in range(width * n_lane_tiles):
                m = jnp.maximum(m, s[:, c * LANES:(c + 1) * LANES])
            m_ref[slot, mi] = m

    def finish_max(slot):
        for mi in range(2):
            m_row = jnp.max(m_ref[slot, mi], axis=-1, keepdims=True)
            m_ref[slot, mi] = jnp.broadcast_to(m_row, (tq, LANES))

    def weighted_values(s_ref, slot, kv, width):
        k0 = pl.multiple_of(kv * tq, tq)
        vt = v_ref[0, pl.ds(k0, width * tq), :]
        for mi in range(2):
            m = m_ref[slot, mi]
            l = l_ref[mi]
            ps = []
            for w in range(width):
                for c in range(n_lane_tiles):
                    p = jnp.exp2(s_ref[mi, kv + w, :, c * LANES:(c + 1) * LANES] - m)
                    l = l + p
                    ps.append(p.astype(BF16))
            l_ref[mi] = l
            acc_ref[mi] += jnp.dot(jnp.concatenate(ps, axis=-1), vt,
                                   preferred_element_type=F32)

    def run(cur_ref, nxt_ref, slot):
        nslot = 1 - slot
        odd_tile_count = slot == 0
        n_pairs = (qi + 1) >> 1
        l_ref[...] = jnp.zeros_like(l_ref)
        acc_ref[...] = jnp.zeros_like(acc_ref)

        @pl.when(qi < nq - 1)
        def _():
            m_ref[nslot] = jnp.full(m_ref.shape[1:], MASK_VALUE, F32)

            def body(pair, carry):
                weighted_values(cur_ref, slot, 2 * pair, 2)
                scores(nxt_ref, nslot, 2 * pair, 2)
                return carry

            lax.fori_loop(0, n_pairs, body, 0)

        def finish_next_tile(prepare_ahead):
            if odd_tile_count:
                weighted_values(cur_ref, slot, qi, 1)
                scores(nxt_ref, nslot, qi, 1)
            if prepare_ahead:
                prepare(q2_ref, k2_ref, cos2_ref, sin2_ref, qi + 2, slot)
            scores(nxt_ref, nslot, qi + 1, 1, masked=True)
            finish_max(nslot)

        @pl.when(qi < nq - 2)
        def _():
            finish_next_tile(True)

        if (nq - 2) % 2 == slot:
            @pl.when(qi == nq - 2)
            def _():
                finish_next_tile(False)

        if (nq - 1) % 2 == slot:
            @pl.when(qi == nq - 1)
            def _():
                def body(pair, carry):
                    weighted_values(cur_ref, slot, 2 * pair, 2)
                    return carry

                lax.fori_loop(0, n_pairs, body, 0)
                if odd_tile_count:
                    weighted_values(cur_ref, slot, qi, 1)

    @pl.when(qi == 0)
    def _():
        row = lax.broadcasted_iota(jnp.int32, (tq, tq), 0)
        col = lax.broadcasted_iota(jnp.int32, (tq, tq), 1)
        bias_ref[...] = jnp.where((col >> CHUNK_SHIFT) <= (row >> CHUNK_SHIFT), 0.0, MASK_VALUE)
        prepare(q0_ref, k0_ref, cos0_ref, sin0_ref, 0, 0)
        prepare(q1_ref, k1_ref, cos1_ref, sin1_ref, 1, 1)
        m_ref[0] = jnp.full(m_ref.shape[1:], MASK_VALUE, F32)
        scores(sa_ref, 0, 0, 1, masked=True)
        finish_max(0)

    @pl.when((qi & 1) == 0)
    def _():
        run(sa_ref, sb_ref, 0)

    @pl.when((qi & 1) == 1)
    def _():
        run(sb_ref, sa_ref, 1)

    lam_p = lam_ref[...]
    lam = (jnp.exp(jnp.sum(lam_p[0:1] * lam_p[1:2], axis=-1, keepdims=True))
           - jnp.exp(jnp.sum(lam_p[2:3] * lam_p[3:4], axis=-1, keepdims=True))
           + lambda_init)
    l0 = jnp.sum(l_ref[0], axis=-1, keepdims=True)
    l1 = jnp.sum(l_ref[1], axis=-1, keepdims=True)
    out = acc_ref[0] / l0 - lam * (acc_ref[1] / l1)
    o_ref[0] = (_rms(out, sg_ref[...]) * (1.0 - lambda_init)).astype(o_ref.dtype)


def _diff_attention(proj, cos_d, sin_d, q_gain, k_gain, sub_gain, lam_params, lambda_init,
                    *, tq=512):
    b, s, _ = proj.shape
    h = DIFF_HEADS
    d = DIFF_HEAD_DIM
    hw = 2 * d
    nq = s // tq
    assert nq >= 2
    blocks = (7 * _nbytes((tq, hw), BF16) + 6 * _nbytes((tq, d), F32) + _nbytes((s, hw), BF16))
    scratch = (_nbytes((2, tq, hw), BF16) + _nbytes((s, hw), BF16) + 2 * _nbytes((2, tq, s), F32)
               + _nbytes((2, 2, tq, LANES), F32) + _nbytes((2, tq, LANES), F32)
               + _nbytes((2, tq, hw), F32))
    fixed = lambda tile, col: (lambda bi, hi, qi: (bi, tile, col(hi)))
    ahead = lambda col: (lambda bi, hi, qi: (bi, jnp.minimum(qi + 2, nq - 1), col(hi)))
    q_col = lambda hi: hi
    k_col = lambda hi: h + hi
    rope_col = lambda hi: 0
    tile_specs = lambda index: [
        pl.BlockSpec((1, tq, hw), index(q_col)),
        pl.BlockSpec((1, tq, hw), index(k_col)),
        pl.BlockSpec((1, tq, d), index(rope_col)),
        pl.BlockSpec((1, tq, d), index(rope_col)),
    ]
    tile_args = (proj, proj, cos_d, sin_d)
    return pl.pallas_call(
        functools.partial(_diff_attn_kernel, tq=tq, nq=nq, lambda_init=lambda_init),
        grid=(b, h, nq),
        in_specs=[
            *tile_specs(functools.partial(fixed, 0)),
            *tile_specs(functools.partial(fixed, 1)),
            *tile_specs(ahead),
            pl.BlockSpec((1, s, hw), lambda bi, hi, qi: (bi, 0, 2 * h + hi)),
            pl.BlockSpec((1, d), lambda bi, hi, qi: (0, 0)),
            pl.BlockSpec((1, d), lambda bi, hi, qi: (0, 0)),
            pl.BlockSpec((1, hw), lambda bi, hi, qi: (0, 0)),
            pl.BlockSpec((4, d), lambda bi, hi, qi: (0, 0)),
        ],
        out_specs=pl.BlockSpec((1, tq, hw), lambda bi, hi, qi: (bi, qi, hi)),
        out_shape=jax.ShapeDtypeStruct((b, s, h * hw), BF16),
        scratch_shapes=[
            pltpu.VMEM((2, tq, hw), BF16),
            pltpu.VMEM((s, hw), BF16),
            pltpu.VMEM((tq, tq), F32),
            pltpu.VMEM((2, nq, tq, tq), F32),
            pltpu.VMEM((2, nq, tq, tq), F32),
            pltpu.VMEM((2, 2, tq, LANES), F32),
            pltpu.VMEM((2, tq, LANES), F32),
            pltpu.VMEM((2, tq, hw), F32),
        ],
        compiler_params=pltpu.CompilerParams(
            dimension_semantics=("parallel", "parallel", "arbitrary"),
            vmem_limit_bytes=_vmem_limit(blocks, scratch)),
        name="diff_attention",
    )(*tile_args, *tile_args, *tile_args, proj,
      q_gain.reshape(1, d), k_gain.reshape(1, d), sub_gain.reshape(1, hw), lam_params)


def _xa_kernel(*refs, heads, n_cast):
    x_ref, g_ref, wq_ref, kv_ref, qg_ref, kg_ref, wo_ref = refs[:7]
    o_ref = refs[7 + n_cast]
    kn_ref = refs[-1]
    dm = x_ref.shape[2]
    hd = dm // heads

    @pl.when(pl.program_id(1) == 0)
    def _():
        for h in range(heads):
            cols = slice(h * hd, (h + 1) * hd)
            kn_ref[:, cols] = _rms(kv_ref[0, :, cols].astype(F32), kg_ref[...]).astype(BF16)

    _Casts.run(refs[7:7 + n_cast], refs[8 + n_cast:8 + 2 * n_cast])
    x = x_ref[0]
    r = lax.rsqrt(jnp.mean(x * x, axis=-1, keepdims=True) + NORM_EPS)
    q = jnp.dot((x * g_ref[...]).astype(BF16), wq_ref[...], preferred_element_type=F32)
    scale = hd ** -0.5
    heads_out = []
    for h in range(heads):
        cols = slice(h * hd, (h + 1) * hd)
        qn = (_rms(q[:, cols] * r, qg_ref[...]) * scale).astype(BF16)
        s = lax.dot_general(qn, kn_ref[:, cols], (((1,), (1,)), ((), ())),
                            preferred_element_type=F32)
        e = jnp.exp(s - jnp.max(s, axis=-1, keepdims=True))
        p = e / jnp.sum(e, axis=-1, keepdims=True)
        v = kv_ref[0, :, dm + h * hd: dm + (h + 1) * hd]
        heads_out.append(jnp.dot(p.astype(BF16), v, preferred_element_type=F32).astype(BF16))
    att = jnp.concatenate(heads_out, axis=-1)
    o_ref[0] = x + jnp.dot(att, wo_ref[...], preferred_element_type=F32)


def _cross_attention_block(x, gain, wq, wo, layer, kv, q_gain, k_gain, *, tm=512,
                           cast_weights=()):
    b, s, dm = x.shape
    mlen = kv.shape[1]
    hd = dm // XA_HEADS
    ni = s // tm
    resident = pl.Buffered(buffer_count=1)
    casts = _Casts(cast_weights, b * ni, lambda bi, i: bi * ni + i)
    blocks = 2 * _nbytes((tm, dm), F32) + _nbytes((mlen, 2 * dm), BF16) + casts.block_bytes
    scratch = 2 * _nbytes((dm, dm), BF16) + _nbytes((mlen, dm), BF16)
    out, *cast_out = pl.pallas_call(
        functools.partial(_xa_kernel, heads=XA_HEADS, n_cast=casts.n),
        grid=(b, ni),
        in_specs=[
            pl.BlockSpec((1, tm, dm), lambda bi, i: (bi, i, 0)),
            pl.BlockSpec((1, dm), lambda bi, i: (0, 0)),
            pl.BlockSpec((None, dm, dm), lambda bi, i: (layer, 0, 0), pipeline_mode=resident),
            pl.BlockSpec((1, mlen, 2 * dm), lambda bi, i: (bi, 0, 0)),
            pl.BlockSpec((1, hd), lambda bi, i: (0, 0)),
            pl.BlockSpec((1, hd), lambda bi, i: (0, 0)),
            pl.BlockSpec((None, dm, dm), lambda bi, i: (layer, 0, 0), pipeline_mode=resident),
            *casts.in_specs,
        ],
        out_specs=[pl.BlockSpec((1, tm, dm), lambda bi, i: (bi, i, 0)), *casts.out_specs],
        out_shape=[jax.ShapeDtypeStruct((b, s, dm), F32), *casts.out_shapes],
        scratch_shapes=[pltpu.VMEM((mlen, dm), BF16)],
        compiler_params=pltpu.CompilerParams(
            dimension_semantics=("parallel", "arbitrary"),
            vmem_limit_bytes=_vmem_limit(blocks, scratch)),
        name="cross_attention_block",
    )(x, gain.reshape(1, dm), wq, kv, q_gain.reshape(1, hd), k_gain.reshape(1, hd), wo,
      *casts.args)
    return (out, *casts.finish(cast_out)) if casts.n else out


def kernel(x, mem, positions, norm_mix, norm_xa, norm_mem, norm_ffn, ev_w_in, ev_ret_gain,
           ev_sg_norm, ev_sg_w, ev_sg_b, ev_w_out, od_w_qkv, od_q_gain, od_k_gain, od_lam_q1,
           od_lam_k1, od_lam_q2, od_lam_k2, od_sub_gain, od_w_o, xa_w_q, xa_w_kv, xa_q_gain,
           xa_k_gain, xa_w_o, ffn_w1, ffn_w2):
    b, s, dm = x.shape
    mlen = mem.shape[1]
    depth = norm_mix.shape[0]
    m = b * s
    assert depth == 2, "the weight-cast schedule below is laid out for one even + one odd layer"
    ev_w_in = ev_w_in.astype(BF16)

    cos_r, sin_r, cos_d, sin_d = _rope_tables(positions)
    xf = x.reshape(m, dm)
    mem_f = mem.reshape(b * mlen, dm)

    proj, ev_w_out, xa_w_q, xa_w_o = _norm_mm(
        xf, norm_mix[0], ev_w_in, 0, cast_weights=(ev_w_out, xa_w_q, xa_w_o),
        name="even_in_proj")
    proj = proj.reshape(b, s, -1)
    out_a = _retention(proj, cos_r, sin_r, ev_ret_gain[0])
    out_b = _spatial_gate(proj, ev_sg_norm[0], ev_sg_w[0], ev_sg_b[0])
    xf, ffn_w1 = _mm2_res(out_a.reshape(m, -1), out_b.reshape(m, -1), ev_w_out, 0, xf,
                          cast_weights=(ffn_w1,), name="even_out_proj")

    for li in range(depth):
        if li == 1:
            lambda_init = 0.8 - 0.6 * math.exp(-0.3 * li)
            proj = _norm_mm(xf, norm_mix[li], od_w_qkv, 0, name="diff_qkv_proj")
            lam_params = jnp.stack([od_lam_q1[0], od_lam_k1[0], od_lam_q2[0], od_lam_k2[0]])
            att = _diff_attention(proj.reshape(b, s, -1), cos_d, sin_d, od_q_gain[0],
                                  od_k_gain[0], od_sub_gain[0], lam_params, lambda_init)
            xf = _mm_res_full(att.reshape(m, -1), od_w_o, 0, xf, name="diff_out_proj")

        kv = _norm_mm(mem_f, norm_mem[li], xa_w_kv, li, tn=1024, name="xa_kv_proj")
        xa_out = _cross_attention_block(
            xf.reshape(b, s, dm), norm_xa[li], xa_w_q, xa_w_o, li, kv.reshape(b, mlen, 2 * dm),
            xa_q_gain[li], xa_k_gain[li], cast_weights=(od_w_qkv, od_w_o) if li == 0 else ())
        if li == 0:
            xa_out, od_w_qkv, od_w_o = xa_out
        xf = xa_out.reshape(m, dm)

        hid = _norm_mm(xf, norm_ffn[li], ffn_w1, li, act="relu2",
                       cast_weights=(ffn_w2,) if li == 0 else (), name="ffn_up")
        if li == 0:
            hid, ffn_w2 = hid
        xf = _mm_res(hid, ffn_w2, li, xf, name="ffn_down")

    return xf.reshape(b, s, dm)
```

```python
import functools
import math

import jax
import jax.numpy as jnp
from jax import lax
from jax.experimental import pallas as pl
from jax.experimental.pallas import tpu as pltpu

F32 = jnp.float32
BF16 = jnp.bfloat16

CHUNK = 64
CHUNK_SHIFT = CHUNK.bit_length() - 1
ROPE_THETA = 10000.0
NORM_EPS = 1e-6
RET_HEADS = 4
RET_QK_DIM = 256
RET_V_DIM = 512
SG_GROUPS = 4
SG_DIM = 256
SG_BLOCK = 128
DIFF_HEADS = 8
DIFF_HEAD_DIM = 128
XA_HEADS = 4
MASK_VALUE = -1e30

V7X_VMEM_BYTES = 64 * 1024 * 1024
VMEM_REQUEST_CAP = V7X_VMEM_BYTES - 8 * 1024 * 1024
LANES = 128


def _vmem_limit(block_bytes, scratch_bytes=0):
    need = 2 * block_bytes + scratch_bytes
    return int(min(VMEM_REQUEST_CAP, max(32 * 1024 * 1024, 2 * need)))


def _nbytes(shape, dtype):
    return math.prod(shape) * jnp.dtype(dtype).itemsize


def _rms(x, gain):
    ms = jnp.mean(x * x, axis=-1, keepdims=True)
    return x * lax.rsqrt(ms + NORM_EPS) * gain


def _gelu_tanh(x):
    c = math.sqrt(2.0 / math.pi)
    return 0.5 * x * (1.0 + jnp.tanh(c * (x + 0.044715 * (x * x * x))))


class _Casts:
    def __init__(self, weights, steps, step_of):
        self.weights = weights
        self.n = len(weights)
        self.args, self.in_specs, self.out_specs, self.out_shapes = [], [], [], []
        self.block_bytes = 0
        for w in weights:
            flat = w.reshape(-1, w.shape[-1])
            rows = flat.shape[0] // steps
            assert rows * steps == flat.shape[0] and rows % 16 == 0
            block = (rows, flat.shape[1])
            index = lambda *grid_ids: (step_of(*grid_ids), 0)
            self.args.append(flat)
            self.in_specs.append(pl.BlockSpec(block, index))
            self.out_specs.append(pl.BlockSpec(block, index))
            self.out_shapes.append(jax.ShapeDtypeStruct(flat.shape, BF16))
            self.block_bytes += _nbytes(block, F32) + _nbytes(block, BF16)

    @staticmethod
    def run(src_refs, dst_refs):
        for src, dst in zip(src_refs, dst_refs):
            dst[...] = src[...].astype(dst.dtype)

    def finish(self, outs):
        return [o.reshape(w.shape) for o, w in zip(outs, self.weights)]


def _rope_kernel(*refs, n_cast):
    pos_ref, inv_r_ref, inv_d_ref = refs[:3]
    cr_ref, sr_ref, cd_ref, sd_ref = refs[3 + n_cast:7 + n_cast]
    _Casts.run(refs[3:3 + n_cast], refs[7 + n_cast:])
    pos = pos_ref[...].astype(F32)
    ang_r = pos * inv_r_ref[...]
    cr_ref[...] = jnp.cos(ang_r)
    sr_ref[...] = jnp.sin(ang_r)
    ang_d = pos * inv_d_ref[...]
    lane = lax.broadcasted_iota(jnp.int32, ang_d.shape, 1)
    sin_d = jnp.sin(ang_d)
    cd_ref[...] = jnp.cos(ang_d)
    sd_ref[...] = jnp.where(lane < DIFF_HEAD_DIM // 2, -sin_d, sin_d)


def _rope_tables(positions, cast_weights=()):
    b, s = positions.shape
    rows = 1024
    n = b * s
    inv_r = ROPE_THETA ** (-jnp.arange(0, RET_QK_DIM, 2, dtype=F32) / RET_QK_DIM)
    inv_d = ROPE_THETA ** (-jnp.arange(0, DIFF_HEAD_DIM, 2, dtype=F32) / DIFF_HEAD_DIM)
    inv_d = jnp.concatenate([inv_d, inv_d])
    out = jax.ShapeDtypeStruct((n, LANES), F32)
    tab_spec = pl.BlockSpec((rows, LANES), lambda i: (i, 0))
    vec_spec = pl.BlockSpec((1, LANES), lambda i: (0, 0))
    casts = _Casts(cast_weights, n // rows, lambda i: i)
    outs = pl.pallas_call(
        functools.partial(_rope_kernel, n_cast=casts.n),
        grid=(n // rows,),
        in_specs=[pl.BlockSpec((rows, 1), lambda i: (i, 0)), vec_spec, vec_spec,
                  *casts.in_specs],
        out_specs=[tab_spec] * 4 + casts.out_specs,
        out_shape=[out] * 4 + casts.out_shapes,
        compiler_params=pltpu.CompilerParams(
            dimension_semantics=("parallel",),
            vmem_limit_bytes=_vmem_limit(casts.block_bytes + 5 * _nbytes((rows, LANES), F32))),
        name="rope_tables",
    )(positions.reshape(n, 1), inv_r.reshape(1, LANES), inv_d.reshape(1, LANES), *casts.args)
    return [t.reshape(b, s, LANES) for t in outs[:4]] + casts.finish(outs[4:])


def _norm_mm_kernel(*refs, act, n_cast):
    x_ref, g_ref, w_ref = refs[:3]
    cast_src = refs[3:3 + n_cast]
    o_ref = refs[3 + n_cast]
    cast_dst = refs[4 + n_cast:4 + 2 * n_cast]
    xg_ref, r_ref = refs[4 + 2 * n_cast:]

    def finish(y):
        r = r_ref[...]
        for c in range(y.shape[1] // LANES):
            lanes = slice(c * LANES, (c + 1) * LANES)
            z = y[:, lanes] * r
            if act == "relu2":
                z = jnp.maximum(z, 0.0)
                z = z * z
            o_ref[:, lanes] = z.astype(o_ref.dtype)

    @pl.when(pl.program_id(1) == 0)
    def _():
        x = x_ref[...]
        xg = (x * g_ref[...]).astype(BF16)
        xg_ref[...] = xg
        ms = jnp.mean(x * x, axis=-1, keepdims=True)
        r_ref[...] = jnp.broadcast_to(lax.rsqrt(ms + NORM_EPS), r_ref.shape)
        _Casts.run(cast_src, cast_dst)
        finish(jnp.dot(xg, w_ref[...].astype(BF16), preferred_element_type=F32))

    @pl.when(pl.program_id(1) != 0)
    def _():
        _Casts.run(cast_src, cast_dst)
        finish(jnp.dot(xg_ref[...], w_ref[...].astype(BF16), preferred_element_type=F32))


def _norm_mm(x, gain, w, layer, *, act=None, tm=1024, tn=2048, cast_weights=(), name):
    m, k = x.shape
    n = w.shape[2]
    tm = min(tm, m)
    assert m % tm == 0 and n % tn == 0
    nj = n // tn
    casts = _Casts(cast_weights, (m // tm) * nj, lambda i, j: i * nj + j)
    blocks = (_nbytes((tm, k), F32) + _nbytes((k, tn), w.dtype) + _nbytes((tm, tn), BF16)
              + _nbytes((1, k), F32) + casts.block_bytes)
    out, *cast_out = pl.pallas_call(
        functools.partial(_norm_mm_kernel, act=act, n_cast=casts.n),
        grid=(m // tm, nj),
        in_specs=[
            pl.BlockSpec((tm, k), lambda i, j: (i, 0)),
            pl.BlockSpec((1, k), lambda i, j: (0, 0)),
            pl.BlockSpec((None, k, tn), lambda i, j: (layer, 0, j)),
            *casts.in_specs,
        ],
        out_specs=[pl.BlockSpec((tm, tn), lambda i, j: (i, j)), *casts.out_specs],
        out_shape=[jax.ShapeDtypeStruct((m, n), BF16), *casts.out_shapes],
        scratch_shapes=[pltpu.VMEM((tm, k), BF16), pltpu.VMEM((tm, LANES), F32)],
        compiler_params=pltpu.CompilerParams(
            dimension_semantics=("parallel", "arbitrary"),
            vmem_limit_bytes=_vmem_limit(blocks, _nbytes((tm, k), BF16))),
        name=name,
    )(x, gain.reshape(1, k), w, *casts.args)
    return (out, *casts.finish(cast_out)) if casts.n else out


def _mm_res_kernel(a_ref, w_ref, r_ref, o_ref):
    @pl.when(pl.program_id(2) == 0)
    def _():
        o_ref[...] = r_ref[...] + jnp.dot(a_ref[...], w_ref[...], preferred_element_type=F32)

    @pl.when(pl.program_id(2) != 0)
    def _():
        o_ref[...] += jnp.dot(a_ref[...], w_ref[...], preferred_element_type=F32)


def _mm_res_full_kernel(a_ref, w_ref, r_ref, o_ref):
    o_ref[...] = r_ref[...] + jnp.dot(a_ref[...], w_ref[...], preferred_element_type=F32)


def _mm_res_full(a, w, layer, res, *, tm=1024, name):
    m, kdim = a.shape
    n = w.shape[2]
    assert m % tm == 0
    blocks = _nbytes((tm, kdim), BF16) + 2 * _nbytes((tm, n), F32)
    return pl.pallas_call(
        _mm_res_full_kernel,
        grid=(m // tm,),
        in_specs=[
            pl.BlockSpec((tm, kdim), lambda i: (i, 0)),
            pl.BlockSpec((None, kdim, n), lambda i: (layer, 0, 0),
                         pipeline_mode=pl.Buffered(buffer_count=1)),
            pl.BlockSpec((tm, n), lambda i: (i, 0)),
        ],
        out_specs=pl.BlockSpec((tm, n), lambda i: (i, 0)),
        out_shape=jax.ShapeDtypeStruct((m, n), F32),
        compiler_params=pltpu.CompilerParams(
            dimension_semantics=("parallel",),
            vmem_limit_bytes=_vmem_limit(blocks, _nbytes((kdim, n), BF16))),
        name=name,
    )(a, w, res)


def _mm_res(a, w, layer, res, *, tm=1024, tn=1024, tk=4096, name):
    m, kdim = a.shape
    n = w.shape[2]
    tk = min(tk, kdim)
    assert m % tm == 0 and n % tn == 0 and kdim % tk == 0
    nk = kdim // tk
    blocks = (_nbytes((tm, tk), BF16) + _nbytes((tk, tn), BF16) + 2 * _nbytes((tm, tn), F32))
    return pl.pallas_call(
        _mm_res_kernel,
        grid=(m // tm, n // tn, nk),
        in_specs=[
            pl.BlockSpec((tm, tk), lambda i, j, k: (i, k)),
            pl.BlockSpec((None, tk, tn), lambda i, j, k: (layer, k, j)),
            pl.BlockSpec((tm, tn), lambda i, j, k: (i, j)),
        ],
        out_specs=pl.BlockSpec((tm, tn), lambda i, j, k: (i, j)),
        out_shape=jax.ShapeDtypeStruct((m, n), F32),
        compiler_params=pltpu.CompilerParams(
            dimension_semantics=("parallel", "parallel", "arbitrary"),
            vmem_limit_bytes=_vmem_limit(blocks)),
        name=name,
    )(a, w, res)


def _mm2_res_kernel(*refs, n_cast):
    a1_ref, a2_ref, w1_ref, w2_ref, r_ref = refs[:5]
    o_ref = refs[5 + n_cast]
    _Casts.run(refs[5:5 + n_cast], refs[6 + n_cast:])
    acc = r_ref[...] + jnp.dot(a1_ref[...], w1_ref[...], preferred_element_type=F32)
    o_ref[...] = acc + jnp.dot(a2_ref[...], w2_ref[...], preferred_element_type=F32)


def _mm2_res(a1, a2, w, layer, res, *, tm=512, cast_weights=(), name):
    m, k1 = a1.shape
    k2 = a2.shape[1]
    n = w.shape[2]
    assert w.shape[1] == k1 + k2 and k1 % k2 == 0
    assert m % tm == 0
    resident = pl.Buffered(buffer_count=1)
    casts = _Casts(cast_weights, m // tm, lambda i: i)
    blocks = _nbytes((tm, k1 + k2), BF16) + 2 * _nbytes((tm, n), F32) + casts.block_bytes
    out, *cast_out = pl.pallas_call(
        functools.partial(_mm2_res_kernel, n_cast=casts.n),
        grid=(m // tm,),
        in_specs=[
            pl.BlockSpec((tm, k1), lambda i: (i, 0)),
            pl.BlockSpec((tm, k2), lambda i: (i, 0)),
            pl.BlockSpec((None, k1, n), lambda i: (layer, 0, 0), pipeline_mode=resident),
            pl.BlockSpec((None, k2, n), lambda i: (layer, k1 // k2, 0), pipeline_mode=resident),
            pl.BlockSpec((tm, n), lambda i: (i, 0)),
            *casts.in_specs,
        ],
        out_specs=[pl.BlockSpec((tm, n), lambda i: (i, 0)), *casts.out_specs],
        out_shape=[jax.ShapeDtypeStruct((m, n), F32), *casts.out_shapes],
        compiler_params=pltpu.CompilerParams(
            dimension_semantics=("parallel",),
            vmem_limit_bytes=_vmem_limit(blocks, _nbytes((k1 + k2, n), BF16))),
        name=name,
    )(a1, a2, w, w, res, *casts.args)
    return (out, *casts.finish(cast_out)) if casts.n else out


def _ret_kernel(q_ref, k_ref, v_ref, g_ref, cos_ref, sin_ref, gain_ref, lg_ref, o_ref,
                state_ref, dmat_ref, qd_ref, kd_ref, *, blk):
    lg = lg_ref[0][:, 0:1]

    state_ref[...] = jnp.zeros_like(state_ref)
    i = lax.broadcasted_iota(jnp.int32, (blk, blk), 0)
    j = lax.broadcasted_iota(jnp.int32, (blk, blk), 1)
    ci = i >> CHUNK_SHIFT
    cj = j >> CHUNK_SHIFT
    d = (i - j).astype(F32)
    expo = jnp.where(ci == cj, jnp.abs(d), d)
    dmat_ref[...] = jnp.where(ci >= cj, jnp.exp(lg * expo), 0.0)
    r = lax.broadcasted_iota(jnp.int32, (blk, RET_QK_DIM), 0).astype(F32)
    qd_ref[...] = jnp.exp(lg * r)
    kd_ref[...] = jnp.exp(lg * (blk - r))

    half = RET_QK_DIM // 2

    def block(n, carry):
        rows = pl.ds(pl.multiple_of(n * blk, blk), blk)
        c = cos_ref[0, rows, :]
        s = sin_ref[0, rows, :]

        def rope(x):
            x1 = x[:, :half]
            x2 = x[:, half:]
            return jnp.concatenate([x1 * c - x2 * s, x2 * c + x1 * s], axis=-1)

        q = rope(q_ref[0, rows, :].astype(F32))
        k = rope(k_ref[0, rows, :].astype(F32)) * (RET_QK_DIM ** -0.5)
        v = v_ref[0, rows, :]
        state = state_ref[...]

        scores = lax.dot_general(q.astype(BF16), k.astype(BF16), (((1,), (1,)), ((), ())),
                                 preferred_element_type=F32) * dmat_ref[...]
        out = jnp.dot(scores.astype(BF16), v, preferred_element_type=F32)
        out = out + jnp.dot((q * qd_ref[...]).astype(BF16), state.astype(BF16),
                            preferred_element_type=F32)
        kv = lax.dot_general((k * kd_ref[...]).astype(BF16), v, (((0,), (0,)), ((), ())),
                             preferred_element_type=F32)
        state_ref[...] = state * jnp.exp(lg * blk) + kv

        y = _rms(out, gain_ref[0])
        g = g_ref[0, rows, :].astype(F32)
        o_ref[0, rows, :] = (g / (1.0 + jnp.exp(-g)) * y).astype(o_ref.dtype)
        return carry

    lax.fori_loop(0, q_ref.shape[1] // blk, block, 0)


def _retention(proj, cos_r, sin_r, ret_gain, *, blk=512):
    b, s, _ = proj.shape
    h = RET_HEADS
    dk, dv = RET_QK_DIM, RET_V_DIM
    v_off = 2 * h * dk // dv
    g_off = v_off + h
    assert s % blk == 0
    log_g = jnp.log(1.0 - 2.0 ** (-5.0 - jnp.arange(h, dtype=F32)))
    log_g = jnp.broadcast_to(log_g[:, None, None], (h, 1, LANES))
    blocks = (2 * _nbytes((s, dk), BF16) + 3 * _nbytes((s, dv), BF16)
              + 2 * _nbytes((s, LANES), F32))
    scratch = (_nbytes((dk, dv), F32) + _nbytes((blk, blk), F32) + 2 * _nbytes((blk, dk), F32))
    return pl.pallas_call(
        functools.partial(_ret_kernel, blk=blk),
        grid=(b, h),
        in_specs=[
            pl.BlockSpec((1, s, dk), lambda bi, hi: (bi, 0, hi)),
            pl.BlockSpec((1, s, dk), lambda bi, hi: (bi, 0, h + hi)),
            pl.BlockSpec((1, s, dv), lambda bi, hi: (bi, 0, v_off + hi)),
            pl.BlockSpec((1, s, dv), lambda bi, hi: (bi, 0, g_off + hi)),
            pl.BlockSpec((1, s, LANES), lambda bi, hi: (bi, 0, 0)),
            pl.BlockSpec((1, s, LANES), lambda bi, hi: (bi, 0, 0)),
            pl.BlockSpec((1, 1, dv), lambda bi, hi: (hi, 0, 0)),
            pl.BlockSpec((1, 1, LANES), lambda bi, hi: (hi, 0, 0)),
        ],
        out_specs=pl.BlockSpec((1, s, dv), lambda bi, hi: (bi, 0, hi)),
        out_shape=jax.ShapeDtypeStruct((b, s, h * dv), BF16),
        scratch_shapes=[
            pltpu.VMEM((dk, dv), F32),
            pltpu.VMEM((blk, blk), F32),
            pltpu.VMEM((blk, dk), F32),
            pltpu.VMEM((blk, dk), F32),
        ],
        compiler_params=pltpu.CompilerParams(
            dimension_semantics=("parallel", "parallel"),
            vmem_limit_bytes=_vmem_limit(blocks, scratch)),
        name="retention",
    )(proj, proj, proj, proj, cos_r, sin_r, ret_gain.reshape(h, 1, dv), log_g)


def _sg_kernel(u_ref, v_ref, gain_ref, w_ref, b_ref, o_ref, *, rows):
    i = lax.broadcasted_iota(jnp.int32, (SG_BLOCK, SG_BLOCK), 0)
    j = lax.broadcasted_iota(jnp.int32, (SG_BLOCK, SG_BLOCK), 1)
    mask = (j >> CHUNK_SHIFT) <= (i >> CHUNK_SHIFT)
    for g in range(SG_GROUPS):
        cols = slice(g * SG_DIM, (g + 1) * SG_DIM)
        w = jnp.where(mask, w_ref[g], 0.0).astype(BF16)
        bias = b_ref[g]
        u = _gelu_tanh(u_ref[0, :, cols].astype(F32))
        v = _rms(_gelu_tanh(v_ref[0, :, cols].astype(F32)), gain_ref[:, cols]).astype(BF16)
        for n in range(rows // SG_BLOCK):
            r = slice(n * SG_BLOCK, (n + 1) * SG_BLOCK)
            mixed = jnp.dot(w, v[r], preferred_element_type=F32) + bias
            o_ref[0, r, cols] = (u[r] * mixed).astype(o_ref.dtype)


def _spatial_gate(proj, sg_norm, sg_w, sg_b, *, rows=512):
    b, s, width = proj.shape
    sgw = SG_GROUPS * SG_DIM
    u_blk = (width - 2 * sgw) // sgw
    blocks = 3 * _nbytes((rows, sgw), BF16) + _nbytes(sg_w.shape, F32)
    return pl.pallas_call(
        functools.partial(_sg_kernel, rows=rows),
        grid=(b, s // rows),
        in_specs=[
            pl.BlockSpec((1, rows, sgw), lambda bi, n: (bi, n, u_blk)),
            pl.BlockSpec((1, rows, sgw), lambda bi, n: (bi, n, u_blk + 1)),
            pl.BlockSpec((1, sgw), lambda bi, n: (0, 0)),
            pl.BlockSpec((SG_GROUPS, SG_BLOCK, SG_BLOCK), lambda bi, n: (0, 0, 0)),
            pl.BlockSpec((SG_GROUPS, SG_BLOCK, 1), lambda bi, n: (0, 0, 0)),
        ],
        out_specs=pl.BlockSpec((1, rows, sgw), lambda bi, n: (bi, n, 0)),
        out_shape=jax.ShapeDtypeStruct((b, s, sgw), BF16),
        compiler_params=pltpu.CompilerParams(
            dimension_semantics=("parallel", "parallel"),
            vmem_limit_bytes=_vmem_limit(blocks)),
        name="spatial_gate",
    )(proj, proj, sg_norm.reshape(1, sgw), sg_w, sg_b.reshape(SG_GROUPS, SG_BLOCK, 1))


def _diff_attn_kernel(q0_ref, k0_ref, cos0_ref, sin0_ref, q1_ref, k1_ref, cos1_ref, sin1_ref,
                      q2_ref, k2_ref, cos2_ref, sin2_ref,
                      v_ref, qg_ref, kg_ref, sg_ref, lam_ref, o_ref,
                      qp_ref, kn_ref, bias_ref, sa_ref, sb_ref, m_ref, l_ref, acc_ref,
                      *, tq, nq, lambda_init):
    qi = pl.program_id(2)
    d = DIFF_HEAD_DIM
    n_lane_tiles = tq // LANES
    prep_rows = min(tq, 256)
    q_scale = d ** -0.5 * math.log2(math.e)

    def prepare(q_ref, k_ref, cos_ref, sin_ref, tile, qbuf):
        ii = lax.broadcasted_iota(jnp.int32, (2 * d, 2 * d), 0)
        jj = lax.broadcasted_iota(jnp.int32, (2 * d, 2 * d), 1)
        swap_halves = jnp.where((ii ^ (d // 2)) == jj, 1.0, 0.0).astype(BF16)

        def norm_rope(x, gain_cos, gain_sin):
            xf = x.astype(F32)
            rot = jnp.dot(x, swap_halves, preferred_element_type=F32)
            out = []
            for mi in range(2):
                lanes = slice(mi * d, (mi + 1) * d)
                xm = xf[:, lanes]
                r = lax.rsqrt(jnp.mean(xm * xm, axis=-1, keepdims=True) + NORM_EPS)
                out.append(r * (xm * gain_cos + rot[:, lanes] * gain_sin))
            return jnp.concatenate(out, axis=-1).astype(BF16)

        def rolled(g):
            return pltpu.roll(jnp.broadcast_to(g, (8, d)), d // 2, 1)[0:1]

        qg = qg_ref[...] * q_scale
        kg = kg_ref[...]
        qg_rot = rolled(qg)
        kg_rot = rolled(kg)
        base = pl.multiple_of(tile * tq, tq)
        for r0 in range(0, tq, prep_rows):
            rows = slice(r0, r0 + prep_rows)
            c = cos_ref[0, rows, :]
            s = sin_ref[0, rows, :]
            qp_ref[qbuf, rows, :] = norm_rope(q_ref[0, rows, :], qg * c, qg_rot * s)
            kn_ref[pl.ds(base + r0, prep_rows), :] = norm_rope(k_ref[0, rows, :], kg * c,
                                                               kg_rot * s)

    def scores(s_ref, slot, kv, width, masked=False):
        k0 = pl.multiple_of(kv * tq, tq)
        kt = kn_ref[pl.ds(k0, width * tq), :]
        for mi in range(2):
            lanes = slice(mi * d, (mi + 1) * d)
            s = lax.dot_general(qp_ref[slot, :, lanes], kt[:, lanes], (((1,), (1,)), ((), ())),
                                preferred_element_type=F32)
            if masked:
                s = s + bias_ref[...]
            for w in range(width):
                s_ref[mi, kv + w] = s[:, w * tq:(w + 1) * tq]
            m = m_ref[slot, mi]
            for c in range(width * n_lane_tiles):
                m = jnp.maximum(m, s[:, c * LANES:(c + 1) * LANES])
            m_ref[slot, mi] = m

    def finish_max(slot):
        for mi in range(2):
            m_row = jnp.max(m_ref[slot, mi], axis=-1, keepdims=True)
            m_ref[slot, mi] = jnp.broadcast_to(m_row, (tq, LANES))

    def weighted_values(s_ref, slot, kv, width):
        k0 = pl.multiple_of(kv * tq, tq)
        vt = v_ref[0, pl.ds(k0, width * tq), :]
        for mi in range(2):
            m = m_ref[slot, mi]
            l = l_ref[mi]
            ps = []
            for w in range(width):
                for c in range(n_lane_tiles):
                    p = jnp.exp2(s_ref[mi, kv + w, :, c * LANES:(c + 1) * LANES] - m)
                    l = l + p
                    ps.append(p.astype(BF16))
            l_ref[mi] = l
            acc_ref[mi] += jnp.dot(jnp.concatenate(ps, axis=-1), vt,
                                   preferred_element_type=F32)

    def run(cur_ref, nxt_ref, slot):
        nslot = 1 - slot
        odd_tile_count = slot == 0
        n_pairs = (qi + 1) >> 1
        l_ref[...] = jnp.zeros_like(l_ref)
        acc_ref[...] = jnp.zeros_like(acc_ref)

        @pl.when(qi < nq - 1)
        def _():
            m_ref[nslot] = jnp.full(m_ref.shape[1:], MASK_VALUE, F32)

            def body(pair, carry):
                weighted_values(cur_ref, slot, 2 * pair, 2)
                scores(nxt_ref, nslot, 2 * pair, 2)
                return carry

            lax.fori_loop(0, n_pairs, body, 0)

        def finish_next_tile(prepare_ahead):
            if odd_tile_count:
                weighted_values(cur_ref, slot, qi, 1)
                scores(nxt_ref, nslot, qi, 1)
            if prepare_ahead:
                prepare(q2_ref, k2_ref, cos2_ref, sin2_ref, qi + 2, slot)
            scores(nxt_ref, nslot, qi + 1, 1, masked=True)
            finish_max(nslot)

        @pl.when(qi < nq - 2)
        def _():
            finish_next_tile(True)

        if (nq - 2) % 2 == slot:
            @pl.when(qi == nq - 2)
            def _():
                finish_next_tile(False)

        if (nq - 1) % 2 == slot:
            @pl.when(qi == nq - 1)
            def _():
                def body(pair, carry):
                    weighted_values(cur_ref, slot, 2 * pair, 2)
                    return carry

                lax.fori_loop(0, n_pairs, body, 0)
                if odd_tile_count:
                    weighted_values(cur_ref, slot, qi, 1)

    @pl.when(qi == 0)
    def _():
        row = lax.broadcasted_iota(jnp.int32, (tq, tq), 0)
        col = lax.broadcasted_iota(jnp.int32, (tq, tq), 1)
        bias_ref[...] = jnp.where((col >> CHUNK_SHIFT) <= (row >> CHUNK_SHIFT), 0.0, MASK_VALUE)
        prepare(q0_ref, k0_ref, cos0_ref, sin0_ref, 0, 0)
        prepare(q1_ref, k1_ref, cos1_ref, sin1_ref, 1, 1)
        m_ref[0] = jnp.full(m_ref.shape[1:], MASK_VALUE, F32)
        scores(sa_ref, 0, 0, 1, masked=True)
        finish_max(0)

    @pl.when((qi & 1) == 0)
    def _():
        run(sa_ref, sb_ref, 0)

    @pl.when((qi & 1) == 1)
    def _():
        run(sb_ref, sa_ref, 1)

    lam_p = lam_ref[...]
    lam = (jnp.exp(jnp.sum(lam_p[0:1] * lam_p[1:2], axis=-1, keepdims=True))
           - jnp.exp(jnp.sum(lam_p[2:3] * lam_p[3:4], axis=-1, keepdims=True))
           + lambda_init)
    l0 = jnp.sum(l_ref[0], axis=-1, keepdims=True)
    l1 = jnp.sum(l_ref[1], axis=-1, keepdims=True)
    out = acc_ref[0] / l0 - lam * (acc_ref[1] / l1)
    o_ref[0] = (_rms(out, sg_ref[...]) * (1.0 - lambda_init)).astype(o_ref.dtype)


def _diff_attention(proj, cos_d, sin_d, q_gain, k_gain, sub_gain, lam_params, lambda_init,
                    *, tq=512):
    b, s, _ = proj.shape
    h = DIFF_HEADS
    d = DIFF_HEAD_DIM
    hw = 2 * d
    nq = s // tq
    assert nq >= 2
    blocks = (7 * _nbytes((tq, hw), BF16) + 6 * _nbytes((tq, d), F32) + _nbytes((s, hw), BF16))
    scratch = (_nbytes((2, tq, hw), BF16) + _nbytes((s, hw), BF16) + 2 * _nbytes((2, tq, s), F32)
               + _nbytes((2, 2, tq, LANES), F32) + _nbytes((2, tq, LANES), F32)
               + _nbytes((2, tq, hw), F32))
    fixed = lambda tile, col: (lambda bi, hi, qi: (bi, tile, col(hi)))
    ahead = lambda col: (lambda bi, hi, qi: (bi, jnp.minimum(qi + 2, nq - 1), col(hi)))
    q_col = lambda hi: hi
    k_col = lambda hi: h + hi
    rope_col = lambda hi: 0
    tile_specs = lambda index: [
        pl.BlockSpec((1, tq, hw), index(q_col)),
        pl.BlockSpec((1, tq, hw), index(k_col)),
        pl.BlockSpec((1, tq, d), index(rope_col)),
        pl.BlockSpec((1, tq, d), index(rope_col)),
    ]
    tile_args = (proj, proj, cos_d, sin_d)
    return pl.pallas_call(
        functools.partial(_diff_attn_kernel, tq=tq, nq=nq, lambda_init=lambda_init),
        grid=(b, h, nq),
        in_specs=[
            *tile_specs(functools.partial(fixed, 0)),
            *tile_specs(functools.partial(fixed, 1)),
            *tile_specs(ahead),
            pl.BlockSpec((1, s, hw), lambda bi, hi, qi: (bi, 0, 2 * h + hi)),
            pl.BlockSpec((1, d), lambda bi, hi, qi: (0, 0)),
            pl.BlockSpec((1, d), lambda bi, hi, qi: (0, 0)),
            pl.BlockSpec((1, hw), lambda bi, hi, qi: (0, 0)),
            pl.BlockSpec((4, d), lambda bi, hi, qi: (0, 0)),
        ],
        out_specs=pl.BlockSpec((1, tq, hw), lambda bi, hi, qi: (bi, qi, hi)),
        out_shape=jax.ShapeDtypeStruct((b, s, h * hw), BF16),
        scratch_shapes=[
            pltpu.VMEM((2, tq, hw), BF16),
            pltpu.VMEM((s, hw), BF16),
            pltpu.VMEM((tq, tq), F32),
            pltpu.VMEM((2, nq, tq, tq), F32),
            pltpu.VMEM((2, nq, tq, tq), F32),
            pltpu.VMEM((2, 2, tq, LANES), F32),
            pltpu.VMEM((2, tq, LANES), F32),
            pltpu.VMEM((2, tq, hw), F32),
        ],
        compiler_params=pltpu.CompilerParams(
            dimension_semantics=("parallel", "parallel", "arbitrary"),
            vmem_limit_bytes=_vmem_limit(blocks, scratch)),
        name="diff_attention",
    )(*tile_args, *tile_args, *tile_args, proj,
      q_gain.reshape(1, d), k_gain.reshape(1, d), sub_gain.reshape(1, hw), lam_params)


def _xa_kernel(*refs, heads, n_cast):
    x_ref, g_ref, wq_ref, kv_ref, qg_ref, kg_ref, wo_ref = refs[:7]
    o_ref = refs[7 + n_cast]
    kn_ref = refs[-1]
    dm = x_ref.shape[2]
    hd = dm // heads

    @pl.when(pl.program_id(1) == 0)
    def _():
        for h in range(heads):
            cols = slice(h * hd, (h + 1) * hd)
            kn_ref[:, cols] = _rms(kv_ref[0, :, cols].astype(F32), kg_ref[...]).astype(BF16)

    _Casts.run(refs[7:7 + n_cast], refs[8 + n_cast:8 + 2 * n_cast])
    x = x_ref[0]
    r = lax.rsqrt(jnp.mean(x * x, axis=-1, keepdims=True) + NORM_EPS)
    q = jnp.dot((x * g_ref[...]).astype(BF16), wq_ref[...], preferred_element_type=F32)
    scale = hd ** -0.5
    heads_out = []
    for h in range(heads):
        cols = slice(h * hd, (h + 1) * hd)
        qn = (_rms(q[:, cols] * r, qg_ref[...]) * scale).astype(BF16)
        s = lax.dot_general(qn, kn_ref[:, cols], (((1,), (1,)), ((), ())),
                            preferred_element_type=F32)
        e = jnp.exp(s - jnp.max(s, axis=-1, keepdims=True))
        p = e / jnp.sum(e, axis=-1, keepdims=True)
        v = kv_ref[0, :, dm + h * hd: dm + (h + 1) * hd]
        heads_out.append(jnp.dot(p.astype(BF16), v, preferred_element_type=F32).astype(BF16))
    att = jnp.concatenate(heads_out, axis=-1)
    o_ref[0] = x + jnp.dot(att, wo_ref[...], preferred_element_type=F32)


def _cross_attention_block(x, gain, wq, wo, layer, kv, q_gain, k_gain, *, tm=512,
                           cast_weights=()):
    b, s, dm = x.shape
    mlen = kv.shape[1]
    hd = dm // XA_HEADS
    ni = s // tm
    resident = pl.Buffered(buffer_count=1)
    casts = _Casts(cast_weights, b * ni, lambda bi, i: bi * ni + i)
    blocks = 2 * _nbytes((tm, dm), F32) + _nbytes((mlen, 2 * dm), BF16) + casts.block_bytes
    scratch = 2 * _nbytes((dm, dm), BF16) + _nbytes((mlen, dm), BF16)
    out, *cast_out = pl.pallas_call(
        functools.partial(_xa_kernel, heads=XA_HEADS, n_cast=casts.n),
        grid=(b, ni),
        in_specs=[
            pl.BlockSpec((1, tm, dm), lambda bi, i: (bi, i, 0)),
            pl.BlockSpec((1, dm), lambda bi, i: (0, 0)),
            pl.BlockSpec((None, dm, dm), lambda bi, i: (layer, 0, 0), pipeline_mode=resident),
            pl.BlockSpec((1, mlen, 2 * dm), lambda bi, i: (bi, 0, 0)),
            pl.BlockSpec((1, hd), lambda bi, i: (0, 0)),
            pl.BlockSpec((1, hd), lambda bi, i: (0, 0)),
            pl.BlockSpec((None, dm, dm), lambda bi, i: (layer, 0, 0), pipeline_mode=resident),
            *casts.in_specs,
        ],
        out_specs=[pl.BlockSpec((1, tm, dm), lambda bi, i: (bi, i, 0)), *casts.out_specs],
        out_shape=[jax.ShapeDtypeStruct((b, s, dm), F32), *casts.out_shapes],
        scratch_shapes=[pltpu.VMEM((mlen, dm), BF16)],
        compiler_params=pltpu.CompilerParams(
            dimension_semantics=("parallel", "arbitrary"),
            vmem_limit_bytes=_vmem_limit(blocks, scratch)),
        name="cross_attention_block",
    )(x, gain.reshape(1, dm), wq, kv, q_gain.reshape(1, hd), k_gain.reshape(1, hd), wo,
      *casts.args)
    return (out, *casts.finish(cast_out)) if casts.n else out


def kernel(x, mem, positions, norm_mix, norm_xa, norm_mem, norm_ffn, ev_w_in, ev_ret_gain,
           ev_sg_norm, ev_sg_w, ev_sg_b, ev_w_out, od_w_qkv, od_q_gain, od_k_gain, od_lam_q1,
           od_lam_k1, od_lam_q2, od_lam_k2, od_sub_gain, od_w_o, xa_w_q, xa_w_kv, xa_q_gain,
           xa_k_gain, xa_w_o, ffn_w1, ffn_w2):
    b, s, dm = x.shape
    mlen = mem.shape[1]
    depth = norm_mix.shape[0]
    m = b * s
    assert depth == 2, "the weight-cast schedule below is laid out for one even + one odd layer"
    cos_r, sin_r, cos_d, sin_d, ev_w_in = _rope_tables(positions, cast_weights=(ev_w_in,))
    xf = x.reshape(m, dm)
    mem_f = mem.reshape(b * mlen, dm)

    proj, ev_w_out, xa_w_q, xa_w_o = _norm_mm(
        xf, norm_mix[0], ev_w_in, 0, cast_weights=(ev_w_out, xa_w_q, xa_w_o),
        name="even_in_proj")
    proj = proj.reshape(b, s, -1)
    out_a = _retention(proj, cos_r, sin_r, ev_ret_gain[0])
    out_b = _spatial_gate(proj, ev_sg_norm[0], ev_sg_w[0], ev_sg_b[0])
    xf, ffn_w1 = _mm2_res(out_a.reshape(m, -1), out_b.reshape(m, -1), ev_w_out, 0, xf,
                          cast_weights=(ffn_w1,), name="even_out_proj")

    for li in range(depth):
        if li == 1:
            lambda_init = 0.8 - 0.6 * math.exp(-0.3 * li)
            proj = _norm_mm(xf, norm_mix[li], od_w_qkv, 0, name="diff_qkv_proj")
            lam_params = jnp.stack([od_lam_q1[0], od_lam_k1[0], od_lam_q2[0], od_lam_k2[0]])
            att = _diff_attention(proj.reshape(b, s, -1), cos_d, sin_d, od_q_gain[0],
                                  od_k_gain[0], od_sub_gain[0], lam_params, lambda_init)
            xf = _mm_res_full(att.reshape(m, -1), od_w_o, 0, xf, name="diff_out_proj")

        kv = _norm_mm(mem_f, norm_mem[li], xa_w_kv, li, tn=1024, name="xa_kv_proj")
        xa_out = _cross_attention_block(
            xf.reshape(b, s, dm), norm_xa[li], xa_w_q, xa_w_o, li, kv.reshape(b, mlen, 2 * dm),
            xa_q_gain[li], xa_k_gain[li], cast_weights=(od_w_qkv, od_w_o) if li == 0 else ())
        if li == 0:
            xa_out, od_w_qkv, od_w_o = xa_out
        xf = xa_out.reshape(m, dm)

        hid = _norm_mm(xf, norm_ffn[li], ffn_w1, li, act="relu2",
                       cast_weights=(ffn_w2,) if li == 0 else (), name="ffn_up")
        if li == 0:
            hid, ffn_w2 = hid
        xf = _mm_res(hid, ffn_w2, li, xf, name="ffn_down")

    return xf.reshape(b, s, dm)
```

```python
import functools
import math

import jax
import jax.numpy as jnp
from jax import lax
from jax.experimental import pallas as pl
from jax.experimental.pallas import tpu as pltpu

F32 = jnp.float32
BF16 = jnp.bfloat16

CHUNK = 64
CHUNK_SHIFT = CHUNK.bit_length() - 1
ROPE_THETA = 10000.0
NORM_EPS = 1e-6
RET_HEADS = 4
RET_QK_DIM = 256
RET_V_DIM = 512
SG_GROUPS = 4
SG_DIM = 256
SG_BLOCK = 128
DIFF_HEADS = 8
DIFF_HEAD_DIM = 128
XA_HEADS = 4
MASK_VALUE = -1e30

V7X_VMEM_BYTES = 64 * 1024 * 1024
VMEM_REQUEST_CAP = V7X_VMEM_BYTES - 8 * 1024 * 1024
LANES = 128


def _vmem_limit(block_bytes, scratch_bytes=0):
    need = 2 * block_bytes + scratch_bytes
    return int(min(VMEM_REQUEST_CAP, max(32 * 1024 * 1024, 2 * need)))


def _nbytes(shape, dtype):
    return math.prod(shape) * jnp.dtype(dtype).itemsize


def _rms(x, gain):
    ms = jnp.mean(x * x, axis=-1, keepdims=True)
    return x * lax.rsqrt(ms + NORM_EPS) * gain


def _gelu_tanh(x):
    c = math.sqrt(2.0 / math.pi)
    return 0.5 * x * (1.0 + jnp.tanh(c * (x + 0.044715 * (x * x * x))))


class _Casts:
    def __init__(self, weights, steps, step_of):
        self.weights = weights
        self.n = len(weights)
        self.args, self.in_specs, self.out_specs, self.out_shapes = [], [], [], []
        self.block_bytes = 0
        for w in weights:
            flat = w.reshape(-1, w.shape[-1])
            rows = flat.shape[0] // steps
            assert rows * steps == flat.shape[0] and rows % 16 == 0
            block = (rows, flat.shape[1])
            index = lambda *grid_ids: (step_of(*grid_ids), 0)
            self.args.append(flat)
            self.in_specs.append(pl.BlockSpec(block, index))
            self.out_specs.append(pl.BlockSpec(block, index))
            self.out_shapes.append(jax.ShapeDtypeStruct(flat.shape, BF16))
            self.block_bytes += _nbytes(block, F32) + _nbytes(block, BF16)

    @staticmethod
    def run(src_refs, dst_refs):
        for src, dst in zip(src_refs, dst_refs):
            dst[...] = src[...].astype(dst.dtype)

    def finish(self, outs):
        return [o.reshape(w.shape) for o, w in zip(outs, self.weights)]


def _rope_kernel(*refs, n_cast, rows):
    first_ref, inv_r_ref, inv_d_ref = refs[:3]
    out_refs = refs[3 + n_cast:7 + n_cast]
    base_ref = refs[-1]
    _Casts.run(refs[3:3 + n_cast], refs[7 + n_cast:7 + 2 * n_cast])
    bi = pl.program_id(0)
    blk = pl.program_id(1)
    invs = (inv_r_ref[...], inv_d_ref[...])

    @pl.when(bi == 0)
    def _():
        t = (blk * rows + lax.broadcasted_iota(jnp.int32, (rows, LANES), 0)).astype(F32)
        for a, inv in enumerate(invs):
            base_ref[2 * a, blk] = jnp.cos(t * inv)
            base_ref[2 * a + 1, blk] = jnp.sin(t * inv)

    first = jnp.broadcast_to(first_ref[0], (8, LANES)).astype(F32)
    lane = lax.broadcasted_iota(jnp.int32, (rows, LANES), 1)
    for a, inv in enumerate(invs):
        c0 = jnp.cos(first * inv)[0:1]
        s0 = jnp.sin(first * inv)[0:1]
        ct = base_ref[2 * a, blk]
        st = base_ref[2 * a + 1, blk]
        out_refs[2 * a][...] = c0 * ct - s0 * st
        sin_sum = s0 * ct + c0 * st
        if a == 1:
            sin_sum = jnp.where(lane < DIFF_HEAD_DIM // 2, -sin_sum, sin_sum)
        out_refs[2 * a + 1][...] = sin_sum


def _rope_tables(positions, cast_weights=()):
    b, s = positions.shape
    rows = 1024
    nblk = s // rows
    n = b * s
    inv_r = ROPE_THETA ** (-jnp.arange(0, RET_QK_DIM, 2, dtype=F32) / RET_QK_DIM)
    inv_d = ROPE_THETA ** (-jnp.arange(0, DIFF_HEAD_DIM, 2, dtype=F32) / DIFF_HEAD_DIM)
    inv_d = jnp.concatenate([inv_d, inv_d])
    out = jax.ShapeDtypeStruct((n, LANES), F32)
    tab_spec = pl.BlockSpec((rows, LANES), lambda bi, blk: (bi * nblk + blk, 0))
    vec_spec = pl.BlockSpec((1, LANES), lambda bi, blk: (0, 0))
    casts = _Casts(cast_weights, b * nblk, lambda bi, blk: bi * nblk + blk)
    base_bytes = _nbytes((4, nblk, rows, LANES), F32)
    outs = pl.pallas_call(
        functools.partial(_rope_kernel, n_cast=casts.n, rows=rows),
        grid=(b, nblk),
        in_specs=[pl.BlockSpec((1, 1, 1), lambda bi, blk: (bi, 0, 0)), vec_spec, vec_spec,
                  *casts.in_specs],
        out_specs=[tab_spec] * 4 + casts.out_specs,
        out_shape=[out] * 4 + casts.out_shapes,
        scratch_shapes=[pltpu.VMEM((4, nblk, rows, LANES), F32)],
        compiler_params=pltpu.CompilerParams(
            dimension_semantics=("arbitrary", "arbitrary"),
            vmem_limit_bytes=_vmem_limit(casts.block_bytes + 4 * _nbytes((rows, LANES), F32),
                                         base_bytes)),
        name="rope_tables",
    )(positions[:, :1].reshape(b, 1, 1), inv_r.reshape(1, LANES), inv_d.reshape(1, LANES),
      *casts.args)
    return [t.reshape(b, s, LANES) for t in outs[:4]] + casts.finish(outs[4:])


def _norm_mm_kernel(*refs, act, n_cast):
    x_ref, g_ref, w_ref = refs[:3]
    cast_src = refs[3:3 + n_cast]
    o_ref = refs[3 + n_cast]
    cast_dst = refs[4 + n_cast:4 + 2 * n_cast]
    xg_ref, r_ref = refs[4 + 2 * n_cast:]

    def finish(y):
        r = r_ref[...]
        for c in range(y.shape[1] // LANES):
            lanes = slice(c * LANES, (c + 1) * LANES)
            z = y[:, lanes] * r
            if act == "relu2":
                z = jnp.maximum(z, 0.0)
                z = z * z
            o_ref[:, lanes] = z.astype(o_ref.dtype)

    @pl.when(pl.program_id(1) == 0)
    def _():
        x = x_ref[...]
        xg = (x * g_ref[...]).astype(BF16)
        xg_ref[...] = xg
        ms = jnp.mean(x * x, axis=-1, keepdims=True)
        r_ref[...] = jnp.broadcast_to(lax.rsqrt(ms + NORM_EPS), r_ref.shape)
        _Casts.run(cast_src, cast_dst)
        finish(jnp.dot(xg, w_ref[...].astype(BF16), preferred_element_type=F32))

    @pl.when(pl.program_id(1) != 0)
    def _():
        _Casts.run(cast_src, cast_dst)
        finish(jnp.dot(xg_ref[...], w_ref[...].astype(BF16), preferred_element_type=F32))


def _norm_mm(x, gain, w, layer, *, act=None, tm=1024, tn=2048, cast_weights=(), name):
    m, k = x.shape
    n = w.shape[2]
    tm = min(tm, m)
    assert m % tm == 0 and n % tn == 0
    nj = n // tn
    casts = _Casts(cast_weights, (m // tm) * nj, lambda i, j: i * nj + j)
    blocks = (_nbytes((tm, k), F32) + _nbytes((k, tn), w.dtype) + _nbytes((tm, tn), BF16)
              + _nbytes((1, k), F32) + casts.block_bytes)
    out, *cast_out = pl.pallas_call(
        functools.partial(_norm_mm_kernel, act=act, n_cast=casts.n),
        grid=(m // tm, nj),
        in_specs=[
            pl.BlockSpec((tm, k), lambda i, j: (i, 0)),
            pl.BlockSpec((1, k), lambda i, j: (0, 0)),
            pl.BlockSpec((None, k, tn), lambda i, j: (layer, 0, j)),
            *casts.in_specs,
        ],
        out_specs=[pl.BlockSpec((tm, tn), lambda i, j: (i, j)), *casts.out_specs],
        out_shape=[jax.ShapeDtypeStruct((m, n), BF16), *casts.out_shapes],
        scratch_shapes=[pltpu.VMEM((tm, k), BF16), pltpu.VMEM((tm, LANES), F32)],
        compiler_params=pltpu.CompilerParams(
            dimension_semantics=("parallel", "arbitrary"),
            vmem_limit_bytes=_vmem_limit(blocks, _nbytes((tm, k), BF16))),
        name=name,
    )(x, gain.reshape(1, k), w, *casts.args)
    return (out, *casts.finish(cast_out)) if casts.n else out


def _mm_res_kernel(a_ref, w_ref, r_ref, o_ref):
    @pl.when(pl.program_id(2) == 0)
    def _():
        o_ref[...] = r_ref[...] + jnp.dot(a_ref[...], w_ref[...], preferred_element_type=F32)

    @pl.when(pl.program_id(2) != 0)
    def _():
        o_ref[...] += jnp.dot(a_ref[...], w_ref[...], preferred_element_type=F32)


def _mm_res_full_kernel(a_ref, w_ref, r_ref, o_ref):
    o_ref[...] = r_ref[...] + jnp.dot(a_ref[...], w_ref[...], preferred_element_type=F32)


def _mm_res_full(a, w, layer, res, *, tm=1024, name):
    m, kdim = a.shape
    n = w.shape[2]
    assert m % tm == 0
    blocks = _nbytes((tm, kdim), BF16) + 2 * _nbytes((tm, n), F32)
    return pl.pallas_call(
        _mm_res_full_kernel,
        grid=(m // tm,),
        in_specs=[
            pl.BlockSpec((tm, kdim), lambda i: (i, 0)),
            pl.BlockSpec((None, kdim, n), lambda i: (layer, 0, 0),
                         pipeline_mode=pl.Buffered(buffer_count=1)),
            pl.BlockSpec((tm, n), lambda i: (i, 0)),
        ],
        out_specs=pl.BlockSpec((tm, n), lambda i: (i, 0)),
        out_shape=jax.ShapeDtypeStruct((m, n), F32),
        compiler_params=pltpu.CompilerParams(
            dimension_semantics=("parallel",),
            vmem_limit_bytes=_vmem_limit(blocks, _nbytes((kdim, n), BF16))),
        name=name,
    )(a, w, res)


def _mm_res(a, w, layer, res, *, tm=1024, tn=1024, tk=4096, name):
    m, kdim = a.shape
    n = w.shape[2]
    tk = min(tk, kdim)
    assert m % tm == 0 and n % tn == 0 and kdim % tk == 0
    nk = kdim // tk
    blocks = (_nbytes((tm, tk), BF16) + _nbytes((tk, tn), BF16) + 2 * _nbytes((tm, tn), F32))
    return pl.pallas_call(
        _mm_res_kernel,
        grid=(m // tm, n // tn, nk),
        in_specs=[
            pl.BlockSpec((tm, tk), lambda i, j, k: (i, k)),
            pl.BlockSpec((None, tk, tn), lambda i, j, k: (layer, k, j)),
            pl.BlockSpec((tm, tn), lambda i, j, k: (i, j)),
        ],
        out_specs=pl.BlockSpec((tm, tn), lambda i, j, k: (i, j)),
        out_shape=jax.ShapeDtypeStruct((m, n), F32),
        compiler_params=pltpu.CompilerParams(
            dimension_semantics=("parallel", "parallel", "arbitrary"),
            vmem_limit_bytes=_vmem_limit(blocks)),
        name=name,
    )(a, w, res)


def _mm2_res_kernel(*refs, n_cast):
    a1_ref, a2_ref, w1_ref, w2_ref, r_ref = refs[:5]
    o_ref = refs[5 + n_cast]
    _Casts.run(refs[5:5 + n_cast], refs[6 + n_cast:])
    acc = r_ref[...] + jnp.dot(a1_ref[...], w1_ref[...], preferred_element_type=F32)
    o_ref[...] = acc + jnp.dot(a2_ref[...], w2_ref[...], preferred_element_type=F32)


def _mm2_res(a1, a2, w, layer, res, *, tm=512, cast_weights=(), name):
    m, k1 = a1.shape
    k2 = a2.shape[1]
    n = w.shape[2]
    assert w.shape[1] == k1 + k2 and k1 % k2 == 0
    assert m % tm == 0
    resident = pl.Buffered(buffer_count=1)
    casts = _Casts(cast_weights, m // tm, lambda i: i)
    blocks = _nbytes((tm, k1 + k2), BF16) + 2 * _nbytes((tm, n), F32) + casts.block_bytes
    out, *cast_out = pl.pallas_call(
        functools.partial(_mm2_res_kernel, n_cast=casts.n),
        grid=(m // tm,),
        in_specs=[
            pl.BlockSpec((tm, k1), lambda i: (i, 0)),
            pl.BlockSpec((tm, k2), lambda i: (i, 0)),
            pl.BlockSpec((None, k1, n), lambda i: (layer, 0, 0), pipeline_mode=resident),
            pl.BlockSpec((None, k2, n), lambda i: (layer, k1 // k2, 0), pipeline_mode=resident),
            pl.BlockSpec((tm, n), lambda i: (i, 0)),
            *casts.in_specs,
        ],
        out_specs=[pl.BlockSpec((tm, n), lambda i: (i, 0)), *casts.out_specs],
        out_shape=[jax.ShapeDtypeStruct((m, n), F32), *casts.out_shapes],
        compiler_params=pltpu.CompilerParams(
            dimension_semantics=("parallel",),
            vmem_limit_bytes=_vmem_limit(blocks, _nbytes((k1 + k2, n), BF16))),
        name=name,
    )(a1, a2, w, w, res, *casts.args)
    return (out, *casts.finish(cast_out)) if casts.n else out


def _ret_kernel(q_ref, k_ref, v_ref, g_ref, cos_ref, sin_ref, gain_ref, lg_ref, o_ref,
                state_ref, dmat_ref, qd_ref, kd_ref, *, blk):
    lg = lg_ref[0][:, 0:1]

    state_ref[...] = jnp.zeros_like(state_ref)
    i = lax.broadcasted_iota(jnp.int32, (blk, blk), 0)
    j = lax.broadcasted_iota(jnp.int32, (blk, blk), 1)
    ci = i >> CHUNK_SHIFT
    cj = j >> CHUNK_SHIFT
    d = (i - j).astype(F32)
    expo = jnp.where(ci == cj, jnp.abs(d), d)
    dmat_ref[...] = jnp.where(ci >= cj, jnp.exp(lg * expo), 0.0)
    r = lax.broadcasted_iota(jnp.int32, (blk, RET_QK_DIM), 0).astype(F32)
    qd_ref[...] = jnp.exp(lg * r)
    kd_ref[...] = jnp.exp(lg * (blk - r))

    half = RET_QK_DIM // 2

    def block(n, carry):
        rows = pl.ds(pl.multiple_of(n * blk, blk), blk)
        c = cos_ref[0, rows, :]
        s = sin_ref[0, rows, :]

        def rope(x):
            x1 = x[:, :half]
            x2 = x[:, half:]
            return jnp.concatenate([x1 * c - x2 * s, x2 * c + x1 * s], axis=-1)

        q = rope(q_ref[0, rows, :].astype(F32))
        k = rope(k_ref[0, rows, :].astype(F32)) * (RET_QK_DIM ** -0.5)
        v = v_ref[0, rows, :]
        state = state_ref[...]

        scores = lax.dot_general(q.astype(BF16), k.astype(BF16), (((1,), (1,)), ((), ())),
                                 preferred_element_type=F32) * dmat_ref[...]
        out = jnp.dot(scores.astype(BF16), v, preferred_element_type=F32)
        out = out + jnp.dot((q * qd_ref[...]).astype(BF16), state.astype(BF16),
                            preferred_element_type=F32)
        kv = lax.dot_general((k * kd_ref[...]).astype(BF16), v, (((0,), (0,)), ((), ())),
                             preferred_element_type=F32)
        state_ref[...] = state * jnp.exp(lg * blk) + kv

        y = _rms(out, gain_ref[0])
        g = g_ref[0, rows, :].astype(F32)
        o_ref[0, rows, :] = (g / (1.0 + jnp.exp(-g)) * y).astype(o_ref.dtype)
        return carry

    lax.fori_loop(0, q_ref.shape[1] // blk, block, 0)


def _retention(proj, cos_r, sin_r, ret_gain, *, blk=512):
    b, s, _ = proj.shape
    h = RET_HEADS
    dk, dv = RET_QK_DIM, RET_V_DIM
    v_off = 2 * h * dk // dv
    g_off = v_off + h
    assert s % blk == 0
    log_g = jnp.log(1.0 - 2.0 ** (-5.0 - jnp.arange(h, dtype=F32)))
    log_g = jnp.broadcast_to(log_g[:, None, None], (h, 1, LANES))
    blocks = (2 * _nbytes((s, dk), BF16) + 3 * _nbytes((s, dv), BF16)
              + 2 * _nbytes((s, LANES), F32))
    scratch = (_nbytes((dk, dv), F32) + _nbytes((blk, blk), F32) + 2 * _nbytes((blk, dk), F32))
    return pl.pallas_call(
        functools.partial(_ret_kernel, blk=blk),
        grid=(b, h),
        in_specs=[
            pl.BlockSpec((1, s, dk), lambda bi, hi: (bi, 0, hi)),
            pl.BlockSpec((1, s, dk), lambda bi, hi: (bi, 0, h + hi)),
            pl.BlockSpec((1, s, dv), lambda bi, hi: (bi, 0, v_off + hi)),
            pl.BlockSpec((1, s, dv), lambda bi, hi: (bi, 0, g_off + hi)),
            pl.BlockSpec((1, s, LANES), lambda bi, hi: (bi, 0, 0)),
            pl.BlockSpec((1, s, LANES), lambda bi, hi: (bi, 0, 0)),
            pl.BlockSpec((1, 1, dv), lambda bi, hi: (hi, 0, 0)),
            pl.BlockSpec((1, 1, LANES), lambda bi, hi: (hi, 0, 0)),
        ],
        out_specs=pl.BlockSpec((1, s, dv), lambda bi, hi: (bi, 0, hi)),
        out_shape=jax.ShapeDtypeStruct((b, s, h * dv), BF16),
        scratch_shapes=[
            pltpu.VMEM((dk, dv), F32),
            pltpu.VMEM((blk, blk), F32),
            pltpu.VMEM((blk, dk), F32),
            pltpu.VMEM((blk, dk), F32),
        ],
        compiler_params=pltpu.CompilerParams(
            dimension_semantics=("parallel", "parallel"),
            vmem_limit_bytes=_vmem_limit(blocks, scratch)),
        name="retention",
    )(proj, proj, proj, proj, cos_r, sin_r, ret_gain.reshape(h, 1, dv), log_g)


def _sg_kernel(u_ref, v_ref, gain_ref, w_ref, b_ref, o_ref, *, rows):
    i = lax.broadcasted_iota(jnp.int32, (SG_BLOCK, SG_BLOCK), 0)
    j = lax.broadcasted_iota(jnp.int32, (SG_BLOCK, SG_BLOCK), 1)
    mask = (j >> CHUNK_SHIFT) <= (i >> CHUNK_SHIFT)
    for g in range(SG_GROUPS):
        cols = slice(g * SG_DIM, (g + 1) * SG_DIM)
        w = jnp.where(mask, w_ref[g], 0.0).astype(BF16)
        bias = b_ref[g]
        u = _gelu_tanh(u_ref[0, :, cols].astype(F32))
        v = _rms(_gelu_tanh(v_ref[0, :, cols].astype(F32)), gain_ref[:, cols]).astype(BF16)
        for n in range(rows // SG_BLOCK):
            r = slice(n * SG_BLOCK, (n + 1) * SG_BLOCK)
            mixed = jnp.dot(w, v[r], preferred_element_type=F32) + bias
            o_ref[0, r, cols] = (u[r] * mixed).astype(o_ref.dtype)


def _spatial_gate(proj, sg_norm, sg_w, sg_b, *, rows=512):
    b, s, width = proj.shape
    sgw = SG_GROUPS * SG_DIM
    u_blk = (width - 2 * sgw) // sgw
    blocks = 3 * _nbytes((rows, sgw), BF16) + _nbytes(sg_w.shape, F32)
    return pl.pallas_call(
        functools.partial(_sg_kernel, rows=rows),
        grid=(b, s // rows),
        in_specs=[
            pl.BlockSpec((1, rows, sgw), lambda bi, n: (bi, n, u_blk)),
            pl.BlockSpec((1, rows, sgw), lambda bi, n: (bi, n, u_blk + 1)),
            pl.BlockSpec((1, sgw), lambda bi, n: (0, 0)),
            pl.BlockSpec((SG_GROUPS, SG_BLOCK, SG_BLOCK), lambda bi, n: (0, 0, 0)),
            pl.BlockSpec((SG_GROUPS, SG_BLOCK, 1), lambda bi, n: (0, 0, 0)),
        ],
        out_specs=pl.BlockSpec((1, rows, sgw), lambda bi, n: (bi, n, 0)),
        out_shape=jax.ShapeDtypeStruct((b, s, sgw), BF16),
        compiler_params=pltpu.CompilerParams(
            dimension_semantics=("parallel", "parallel"),
            vmem_limit_bytes=_vmem_limit(blocks)),
        name="spatial_gate",
    )(proj, proj, sg_norm.reshape(1, sgw), sg_w, sg_b.reshape(SG_GROUPS, SG_BLOCK, 1))


def _diff_attn_kernel(q0_ref, k0_ref, cos0_ref, sin0_ref, q1_ref, k1_ref, cos1_ref, sin1_ref,
                      q2_ref, k2_ref, cos2_ref, sin2_ref,
                      v_ref, qg_ref, kg_ref, sg_ref, lam_ref, o_ref,
                      qp_ref, kn_ref, bias_ref, sa_ref, sb_ref, m_ref, l_ref, acc_ref,
                      *, tq, nq, lambda_init):
    qi = pl.program_id(2)
    d = DIFF_HEAD_DIM
    n_lane_tiles = tq // LANES
    prep_rows = min(tq, 256)
    q_scale = d ** -0.5 * math.log2(math.e)

    def prepare(q_ref, k_ref, cos_ref, sin_ref, tile, qbuf):
        ii = lax.broadcasted_iota(jnp.int32, (2 * d, 2 * d), 0)
        jj = lax.broadcasted_iota(jnp.int32, (2 * d, 2 * d), 1)
        swap_halves = jnp.where((ii ^ (d // 2)) == jj, 1.0, 0.0).astype(BF16)

        def norm_rope(x, gain_cos, gain_sin):
            xf = x.astype(F32)
            rot = jnp.dot(x, swap_halves, preferred_element_type=F32)
            out = []
            for mi in range(2):
                lanes = slice(mi * d, (mi + 1) * d)
                xm = xf[:, lanes]
                r = lax.rsqrt(jnp.mean(xm * xm, axis=-1, keepdims=True) + NORM_EPS)
                out.append(r * (xm * gain_cos + rot[:, lanes] * gain_sin))
            return jnp.concatenate(out, axis=-1).astype(BF16)

        def rolled(g):
            return pltpu.roll(jnp.broadcast_to(g, (8, d)), d // 2, 1)[0:1]

        qg = qg_ref[...] * q_scale
        kg = kg_ref[...]
        qg_rot = rolled(qg)
        kg_rot = rolled(kg)
        base = pl.multiple_of(tile * tq, tq)
        for r0 in range(0, tq, prep_rows):
            rows = slice(r0, r0 + prep_rows)
            c = cos_ref[0, rows, :]
            s = sin_ref[0, rows, :]
            qp_ref[qbuf, rows, :] = norm_rope(q_ref[0, rows, :], qg * c, qg_rot * s)
            kn_ref[pl.ds(base + r0, prep_rows), :] = norm_rope(k_ref[0, rows, :], kg * c,
                                                               kg_rot * s)

    def scores(s_ref, slot, kv, width, masked=False):
        k0 = pl.multiple_of(kv * tq, tq)
        kt = kn_ref[pl.ds(k0, width * tq), :]
        for mi in range(2):
            lanes = slice(mi * d, (mi + 1) * d)
            s = lax.dot_general(qp_ref[slot, :, lanes], kt[:, lanes], (((1,), (1,)), ((), ())),
                                preferred_element_type=F32)
            if masked:
                s = s + bias_ref[...]
            for w in range(width):
                s_ref[mi, kv + w] = s[:, w * tq:(w + 1) * tq]
            m = m_ref[slot, mi]
            for c in range(width * n_lane_tiles):
                m = jnp.maximum(m, s[:, c * LANES:(c + 1) * LANES])
            m_ref[slot, mi] = m

    def finish_max(slot):
        for mi in range(2):
            m_row = jnp.max(m_ref[slot, mi], axis=-1, keepdims=True)
            m_ref[slot, mi] = jnp.broadcast_to(m_row, (tq, LANES))

    def weighted_values(s_ref, slot, kv, width):
        k0 = pl.multiple_of(kv * tq, tq)
        vt = v_ref[0, pl.ds(k0, width * tq), :]
        for mi in range(2):
            m = m_ref[slot, mi]
            l = l_ref[mi]
            ps = []
            for w in range(width):
                for c in range(n_lane_tiles):
                    p = jnp.exp2(s_ref[mi, kv + w, :, c * LANES:(c + 1) * LANES] - m)
                    l = l + p
                    ps.append(p.astype(BF16))
            l_ref[mi] = l
            acc_ref[mi] += jnp.dot(jnp.concatenate(ps, axis=-1), vt,
                                   preferred_element_type=F32)

    def run(cur_ref, nxt_ref, slot):
        nslot = 1 - slot
        odd_tile_count = slot == 0
        n_pairs = (qi + 1) >> 1
        l_ref[...] = jnp.zeros_like(l_ref)
        acc_ref[...] = jnp.zeros_like(acc_ref)

        @pl.when(qi < nq - 1)
        def _():
            m_ref[nslot] = jnp.full(m_ref.shape[1:], MASK_VALUE, F32)

            def body(pair, carry):
                weighted_values(cur_ref, slot, 2 * pair, 2)
                scores(nxt_ref, nslot, 2 * pair, 2)
                return carry

            lax.fori_loop(0, n_pairs, body, 0)

        def finish_next_tile(prepare_ahead):
            if odd_tile_count:
                weighted_values(cur_ref, slot, qi, 1)
                scores(nxt_ref, nslot, qi, 1)
            if prepare_ahead:
                prepare(q2_ref, k2_ref, cos2_ref, sin2_ref, qi + 2, slot)
            scores(nxt_ref, nslot, qi + 1, 1, masked=True)
            finish_max(nslot)

        @pl.when(qi < nq - 2)
        def _():
            finish_next_tile(True)

        if (nq - 2) % 2 == slot:
            @pl.when(qi == nq - 2)
            def _():
                finish_next_tile(False)

        if (nq - 1) % 2 == slot:
            @pl.when(qi == nq - 1)
            def _():
                def body(pair, carry):
                    weighted_values(cur_ref, slot, 2 * pair, 2)
                    return carry

                lax.fori_loop(0, n_pairs, body, 0)
                if odd_tile_count:
                    weighted_values(cur_ref, slot, qi, 1)

    @pl.when(qi == 0)
    def _():
        row = lax.broadcasted_iota(jnp.int32, (tq, tq), 0)
        col = lax.broadcasted_iota(jnp.int32, (tq, tq), 1)
        bias_ref[...] = jnp.where((col >> CHUNK_SHIFT) <= (row >> CHUNK_SHIFT), 0.0, MASK_VALUE)
        prepare(q0_ref, k0_ref, cos0_ref, sin0_ref, 0, 0)
        prepare(q1_ref, k1_ref, cos1_ref, sin1_ref, 1, 1)
        m_ref[0] = jnp.full(m_ref.shape[1:], MASK_VALUE, F32)
        scores(sa_ref, 0, 0, 1, masked=True)
        finish_max(0)

    @pl.when((qi & 1) == 0)
    def _():
        run(sa_ref, sb_ref, 0)

    @pl.when((qi & 1) == 1)
    def _():
        run(sb_ref, sa_ref, 1)

    lam_p = lam_ref[...]
    lam = (jnp.exp(jnp.sum(lam_p[0:1] * lam_p[1:2], axis=-1, keepdims=True))
           - jnp.exp(jnp.sum(lam_p[2:3] * lam_p[3:4], axis=-1, keepdims=True))
           + lambda_init)
    l0 = jnp.sum(l_ref[0], axis=-1, keepdims=True)
    l1 = jnp.sum(l_ref[1], axis=-1, keepdims=True)
    out = acc_ref[0] / l0 - lam * (acc_ref[1] / l1)
    o_ref[0] = (_rms(out, sg_ref[...]) * (1.0 - lambda_init)).astype(o_ref.dtype)


def _diff_attention(proj, cos_d, sin_d, q_gain, k_gain, sub_gain, lam_params, lambda_init,
                    *, tq=512):
    b, s, _ = proj.shape
    h = DIFF_HEADS
    d = DIFF_HEAD_DIM
    hw = 2 * d
    nq = s // tq
    assert nq >= 2
    blocks = (7 * _nbytes((tq, hw), BF16) + 6 * _nbytes((tq, d), F32) + _nbytes((s, hw), BF16))
    scratch = (_nbytes((2, tq, hw), BF16) + _nbytes((s, hw), BF16) + 2 * _nbytes((2, tq, s), F32)
               + _nbytes((2, 2, tq, LANES), F32) + _nbytes((2, tq, LANES), F32)
               + _nbytes((2, tq, hw), F32))
    fixed = lambda tile, col: (lambda bi, hi, qi: (bi, tile, col(hi)))
    ahead = lambda col: (lambda bi, hi, qi: (bi, jnp.minimum(qi + 2, nq - 1), col(hi)))
    q_col = lambda hi: hi
    k_col = lambda hi: h + hi
    rope_col = lambda hi: 0
    tile_specs = lambda index: [
        pl.BlockSpec((1, tq, hw), index(q_col)),
        pl.BlockSpec((1, tq, hw), index(k_col)),
        pl.BlockSpec((1, tq, d), index(rope_col)),
        pl.BlockSpec((1, tq, d), index(rope_col)),
    ]
    tile_args = (proj, proj, cos_d, sin_d)
    return pl.pallas_call(
        functools.partial(_diff_attn_kernel, tq=tq, nq=nq, lambda_init=lambda_init),
        grid=(b, h, nq),
        in_specs=[
            *tile_specs(functools.partial(fixed, 0)),
            *tile_specs(functools.partial(fixed, 1)),
            *tile_specs(ahead),
            pl.BlockSpec((1, s, hw), lambda bi, hi, qi: (bi, 0, 2 * h + hi)),
            pl.BlockSpec((1, d), lambda bi, hi, qi: (0, 0)),
            pl.BlockSpec((1, d), lambda bi, hi, qi: (0, 0)),
            pl.BlockSpec((1, hw), lambda bi, hi, qi: (0, 0)),
            pl.BlockSpec((4, d), lambda bi, hi, qi: (0, 0)),
        ],
        out_specs=pl.BlockSpec((1, tq, hw), lambda bi, hi, qi: (bi, qi, hi)),
        out_shape=jax.ShapeDtypeStruct((b, s, h * hw), BF16),
        scratch_shapes=[
            pltpu.VMEM((2, tq, hw), BF16),
            pltpu.VMEM((s, hw), BF16),
            pltpu.VMEM((tq, tq), F32),
            pltpu.VMEM((2, nq, tq, tq), F32),
            pltpu.VMEM((2, nq, tq, tq), F32),
            pltpu.VMEM((2, 2, tq, LANES), F32),
            pltpu.VMEM((2, tq, LANES), F32),
            pltpu.VMEM((2, tq, hw), F32),
        ],
        compiler_params=pltpu.CompilerParams(
            dimension_semantics=("parallel", "parallel", "arbitrary"),
            vmem_limit_bytes=_vmem_limit(blocks, scratch)),
        name="diff_attention",
    )(*tile_args, *tile_args, *tile_args, proj,
      q_gain.reshape(1, d), k_gain.reshape(1, d), sub_gain.reshape(1, hw), lam_params)


def _xa_kernel(*refs, heads, n_cast):
    x_ref, g_ref, wq_ref, kv_ref, qg_ref, kg_ref, wo_ref = refs[:7]
    o_ref = refs[7 + n_cast]
    kn_ref = refs[-1]
    dm = x_ref.shape[2]
    hd = dm // heads

    @pl.when(pl.program_id(1) == 0)
    def _():
        for h in range(heads):
            cols = slice(h * hd, (h + 1) * hd)
            kn_ref[:, cols] = _rms(kv_ref[0, :, cols].astype(F32), kg_ref[...]).astype(BF16)

    _Casts.run(refs[7:7 + n_cast], refs[8 + n_cast:8 + 2 * n_cast])
    x = x_ref[0]
    r = lax.rsqrt(jnp.mean(x * x, axis=-1, keepdims=True) + NORM_EPS)
    q = jnp.dot((x * g_ref[...]).astype(BF16), wq_ref[...], preferred_element_type=F32)
    scale = hd ** -0.5
    heads_out = []
    for h in range(heads):
        cols = slice(h * hd, (h + 1) * hd)
        qn = (_rms(q[:, cols] * r, qg_ref[...]) * scale).astype(BF16)
        s = lax.dot_general(qn, kn_ref[:, cols], (((1,), (1,)), ((), ())),
                            preferred_element_type=F32)
        e = jnp.exp(s - jnp.max(s, axis=-1, keepdims=True))
        p = e / jnp.sum(e, axis=-1, keepdims=True)
        v = kv_ref[0, :, dm + h * hd: dm + (h + 1) * hd]
        heads_out.append(jnp.dot(p.astype(BF16), v, preferred_element_type=F32).astype(BF16))
    att = jnp.concatenate(heads_out, axis=-1)
    o_ref[0] = x + jnp.dot(att, wo_ref[...], preferred_element_type=F32)


def _cross_attention_block(x, gain, wq, wo, layer, kv, q_gain, k_gain, *, tm=512,
                           cast_weights=()):
    b, s, dm = x.shape
    mlen = kv.shape[1]
    hd = dm // XA_HEADS
    ni = s // tm
    resident = pl.Buffered(buffer_count=1)
    casts = _Casts(cast_weights, b * ni, lambda bi, i: bi * ni + i)
    blocks = 2 * _nbytes((tm, dm), F32) + _nbytes((mlen, 2 * dm), BF16) + casts.block_bytes
    scratch = 2 * _nbytes((dm, dm), BF16) + _nbytes((mlen, dm), BF16)
    out, *cast_out = pl.pallas_call(
        functools.partial(_xa_kernel, heads=XA_HEADS, n_cast=casts.n),
        grid=(b, ni),
        in_specs=[
            pl.BlockSpec((1, tm, dm), lambda bi, i: (bi, i, 0)),
            pl.BlockSpec((1, dm), lambda bi, i: (0, 0)),
            pl.BlockSpec((None, dm, dm), lambda bi, i: (layer, 0, 0), pipeline_mode=resident),
            pl.BlockSpec((1, mlen, 2 * dm), lambda bi, i: (bi, 0, 0)),
            pl.BlockSpec((1, hd), lambda bi, i: (0, 0)),
            pl.BlockSpec((1, hd), lambda bi, i: (0, 0)),
            pl.BlockSpec((None, dm, dm), lambda bi, i: (layer, 0, 0), pipeline_mode=resident),
            *casts.in_specs,
        ],
        out_specs=[pl.BlockSpec((1, tm, dm), lambda bi, i: (bi, i, 0)), *casts.out_specs],
        out_shape=[jax.ShapeDtypeStruct((b, s, dm), F32), *casts.out_shapes],
        scratch_shapes=[pltpu.VMEM((mlen, dm), BF16)],
        compiler_params=pltpu.CompilerParams(
            dimension_semantics=("parallel", "arbitrary"),
            vmem_limit_bytes=_vmem_limit(blocks, scratch)),
        name="cross_attention_block",
    )(x, gain.reshape(1, dm), wq, kv, q_gain.reshape(1, hd), k_gain.reshape(1, hd), wo,
      *casts.args)
    return (out, *casts.finish(cast_out)) if casts.n else out


def kernel(x, mem, positions, norm_mix, norm_xa, norm_mem, norm_ffn, ev_w_in, ev_ret_gain,
           ev_sg_norm, ev_sg_w, ev_sg_b, ev_w_out, od_w_qkv, od_q_gain, od_k_gain, od_lam_q1,
           od_lam_k1, od_lam_q2, od_lam_k2, od_sub_gain, od_w_o, xa_w_q, xa_w_kv, xa_q_gain,
           xa_k_gain, xa_w_o, ffn_w1, ffn_w2):
    b, s, dm = x.shape
    mlen = mem.shape[1]
    depth = norm_mix.shape[0]
    m = b * s
    assert depth == 2, "the weight-cast schedule below is laid out for one even + one odd layer"
    cos_r, sin_r, cos_d, sin_d, ev_w_in = _rope_tables(positions, cast_weights=(ev_w_in,))
    xf = x.reshape(m, dm)
    mem_f = mem.reshape(b * mlen, dm)

    proj, ev_w_out, xa_w_q, xa_w_o = _norm_mm(
        xf, norm_mix[0], ev_w_in, 0, cast_weights=(ev_w_out, xa_w_q, xa_w_o),
        name="even_in_proj")
    proj = proj.reshape(b, s, -1)
    out_a = _retention(proj, cos_r, sin_r, ev_ret_gain[0])
    out_b = _spatial_gate(proj, ev_sg_norm[0], ev_sg_w[0], ev_sg_b[0])
    xf, ffn_w1 = _mm2_res(out_a.reshape(m, -1), out_b.reshape(m, -1), ev_w_out, 0, xf,
                          cast_weights=(ffn_w1,), name="even_out_proj")

    for li in range(depth):
        if li == 1:
            lambda_init = 0.8 - 0.6 * math.exp(-0.3 * li)
            proj = _norm_mm(xf, norm_mix[li], od_w_qkv, 0, name="diff_qkv_proj")
            lam_params = jnp.stack([od_lam_q1[0], od_lam_k1[0], od_lam_q2[0], od_lam_k2[0]])
            att = _diff_attention(proj.reshape(b, s, -1), cos_d, sin_d, od_q_gain[0],
                                  od_k_gain[0], od_sub_gain[0], lam_params, lambda_init)
            xf = _mm_res_full(att.reshape(m, -1), od_w_o, 0, xf, name="diff_out_proj")

        kv = _norm_mm(mem_f, norm_mem[li], xa_w_kv, li, tn=1024, name="xa_kv_proj")
        xa_out = _cross_attention_block(
            xf.reshape(b, s, dm), norm_xa[li], xa_w_q, xa_w_o, li, kv.reshape(b, mlen, 2 * dm),
            xa_q_gain[li], xa_k_gain[li], cast_weights=(od_w_qkv, od_w_o) if li == 0 else ())
        if li == 0:
            xa_out, od_w_qkv, od_w_o = xa_out
        xf = xa_out.reshape(m, dm)

        hid = _norm_mm(xf, norm_ffn[li], ffn_w1, li, act="relu2",
                       cast_weights=(ffn_w2,) if li == 0 else (), name="ffn_up")
        if li == 0:
            hid, ffn_w2 = hid
        xf = _mm_res(hid, ffn_w2, li, xf, name="ffn_down")

    return xf.reshape(b, s, dm)
```

```python
import functools
import math

import jax
import jax.numpy as jnp
from jax import lax
from jax.experimental import pallas as pl
from jax.experimental.pallas import tpu as pltpu

F32 = jnp.float32
BF16 = jnp.bfloat16

CHUNK = 64
CHUNK_SHIFT = CHUNK.bit_length() - 1
ROPE_THETA = 10000.0
NORM_EPS = 1e-6
RET_HEADS = 4
RET_QK_DIM = 256
RET_V_DIM = 512
SG_GROUPS = 4
SG_DIM = 256
SG_BLOCK = 128
DIFF_HEADS = 8
DIFF_HEAD_DIM = 128
XA_HEADS = 4
MASK_VALUE = -1e30

V7X_VMEM_BYTES = 64 * 1024 * 1024
VMEM_REQUEST_CAP = V7X_VMEM_BYTES - 8 * 1024 * 1024
LANES = 128


def _vmem_limit(block_bytes, scratch_bytes=0):
    need = 2 * block_bytes + scratch_bytes
    return int(min(VMEM_REQUEST_CAP, max(32 * 1024 * 1024, 2 * need)))


def _nbytes(shape, dtype):
    return math.prod(shape) * jnp.dtype(dtype).itemsize


def _rms(x, gain):
    ms = jnp.mean(x * x, axis=-1, keepdims=True)
    return x * lax.rsqrt(ms + NORM_EPS) * gain


def _gelu_tanh(x):
    c = math.sqrt(2.0 / math.pi)
    return 0.5 * x * (1.0 + jnp.tanh(c * (x + 0.044715 * (x * x * x))))


class _Casts:
    def __init__(self, weights, steps, step_of):
        self.weights = weights
        self.n = len(weights)
        self.args, self.in_specs, self.out_specs, self.out_shapes = [], [], [], []
        self.block_bytes = 0
        for w in weights:
            flat = w.reshape(-1, w.shape[-1])
            rows = flat.shape[0] // steps
            assert rows * steps == flat.shape[0] and rows % 16 == 0
            block = (rows, flat.shape[1])
            index = lambda *grid_ids: (step_of(*grid_ids), 0)
            self.args.append(flat)
            self.in_specs.append(pl.BlockSpec(block, index))
            self.out_specs.append(pl.BlockSpec(block, index))
            self.out_shapes.append(jax.ShapeDtypeStruct(flat.shape, BF16))
            self.block_bytes += _nbytes(block, F32) + _nbytes(block, BF16)

    @staticmethod
    def run(src_refs, dst_refs):
        for src, dst in zip(src_refs, dst_refs):
            dst[...] = src[...].astype(dst.dtype)

    def finish(self, outs):
        return [o.reshape(w.shape) for o, w in zip(outs, self.weights)]


def _rope_kernel(*refs, n_cast, rows):
    first_ref, inv_r_ref, inv_d_ref = refs[:3]
    out_refs = refs[3 + n_cast:7 + n_cast]
    base_ref = refs[-1]
    _Casts.run(refs[3:3 + n_cast], refs[7 + n_cast:7 + 2 * n_cast])
    bi = pl.program_id(0)
    blk = pl.program_id(1)
    invs = (inv_r_ref[...], inv_d_ref[...])

    @pl.when(bi == 0)
    def _():
        t = (blk * rows + lax.broadcasted_iota(jnp.int32, (rows, LANES), 0)).astype(F32)
        for a, inv in enumerate(invs):
            base_ref[2 * a, blk] = jnp.cos(t * inv)
            base_ref[2 * a + 1, blk] = jnp.sin(t * inv)

    first = jnp.broadcast_to(first_ref[0], (8, LANES)).astype(F32)
    lane = lax.broadcasted_iota(jnp.int32, (rows, LANES), 1)
    for a, inv in enumerate(invs):
        c0 = jnp.cos(first * inv)[0:1]
        s0 = jnp.sin(first * inv)[0:1]
        ct = base_ref[2 * a, blk]
        st = base_ref[2 * a + 1, blk]
        out_refs[2 * a][...] = c0 * ct - s0 * st
        sin_sum = s0 * ct + c0 * st
        if a == 1:
            sin_sum = jnp.where(lane < DIFF_HEAD_DIM // 2, -sin_sum, sin_sum)
        out_refs[2 * a + 1][...] = sin_sum


def _rope_tables(positions, cast_weights=()):
    b, s = positions.shape
    rows = 1024
    nblk = s // rows
    n = b * s
    inv_r = ROPE_THETA ** (-jnp.arange(0, RET_QK_DIM, 2, dtype=F32) / RET_QK_DIM)
    inv_d = ROPE_THETA ** (-jnp.arange(0, DIFF_HEAD_DIM, 2, dtype=F32) / DIFF_HEAD_DIM)
    inv_d = jnp.concatenate([inv_d, inv_d])
    out = jax.ShapeDtypeStruct((n, LANES), F32)
    tab_spec = pl.BlockSpec((rows, LANES), lambda bi, blk: (bi * nblk + blk, 0))
    vec_spec = pl.BlockSpec((1, LANES), lambda bi, blk: (0, 0))
    casts = _Casts(cast_weights, b * nblk, lambda bi, blk: bi * nblk + blk)
    base_bytes = _nbytes((4, nblk, rows, LANES), F32)
    outs = pl.pallas_call(
        functools.partial(_rope_kernel, n_cast=casts.n, rows=rows),
        grid=(b, nblk),
        in_specs=[pl.BlockSpec((1, 1, 1), lambda bi, blk: (bi, 0, 0)), vec_spec, vec_spec,
                  *casts.in_specs],
        out_specs=[tab_spec] * 4 + casts.out_specs,
        out_shape=[out] * 4 + casts.out_shapes,
        scratch_shapes=[pltpu.VMEM((4, nblk, rows, LANES), F32)],
        compiler_params=pltpu.CompilerParams(
            dimension_semantics=("arbitrary", "arbitrary"),
            vmem_limit_bytes=_vmem_limit(casts.block_bytes + 4 * _nbytes((rows, LANES), F32),
                                         base_bytes)),
        name="rope_tables",
    )(positions[:, :1].reshape(b, 1, 1), inv_r.reshape(1, LANES), inv_d.reshape(1, LANES),
      *casts.args)
    return [t.reshape(b, s, LANES) for t in outs[:4]] + casts.finish(outs[4:])


def _norm_mm_kernel(*refs, act, n_cast):
    x_ref, g_ref, w_ref = refs[:3]
    cast_src = refs[3:3 + n_cast]
    o_ref = refs[3 + n_cast]
    cast_dst = refs[4 + n_cast:4 + 2 * n_cast]
    xg_ref, r_ref = refs[4 + 2 * n_cast:]

    def finish(y):
        r = r_ref[...]
        for c in range(y.shape[1] // LANES):
            lanes = slice(c * LANES, (c + 1) * LANES)
            z = y[:, lanes] * r
            if act == "relu2":
                z = jnp.maximum(z, 0.0)
                z = z * z
            o_ref[:, lanes] = z.astype(o_ref.dtype)

    @pl.when(pl.program_id(1) == 0)
    def _():
        x = x_ref[...]
        xg = (x * g_ref[...]).astype(BF16)
        xg_ref[...] = xg
        ms = jnp.mean(x * x, axis=-1, keepdims=True)
        r_ref[...] = jnp.broadcast_to(lax.rsqrt(ms + NORM_EPS), r_ref.shape)
        _Casts.run(cast_src, cast_dst)
        finish(jnp.dot(xg, w_ref[...].astype(BF16), preferred_element_type=F32))

    @pl.when(pl.program_id(1) != 0)
    def _():
        _Casts.run(cast_src, cast_dst)
        finish(jnp.dot(xg_ref[...], w_ref[...].astype(BF16), preferred_element_type=F32))


def _norm_mm(x, gain, w, layer, *, act=None, tm=1024, tn=2048, cast_weights=(), name):
    m, k = x.shape
    n = w.shape[2]
    tm = min(tm, m)
    assert m % tm == 0 and n % tn == 0
    nj = n // tn
    casts = _Casts(cast_weights, (m // tm) * nj, lambda i, j: i * nj + j)
    blocks = (_nbytes((tm, k), F32) + _nbytes((k, tn), w.dtype) + _nbytes((tm, tn), BF16)
              + _nbytes((1, k), F32) + casts.block_bytes)
    out, *cast_out = pl.pallas_call(
        functools.partial(_norm_mm_kernel, act=act, n_cast=casts.n),
        grid=(m // tm, nj),
        in_specs=[
            pl.BlockSpec((tm, k), lambda i, j: (i, 0)),
            pl.BlockSpec((1, k), lambda i, j: (0, 0)),
            pl.BlockSpec((None, k, tn), lambda i, j: (layer, 0, j)),
            *casts.in_specs,
        ],
        out_specs=[pl.BlockSpec((tm, tn), lambda i, j: (i, j)), *casts.out_specs],
        out_shape=[jax.ShapeDtypeStruct((m, n), BF16), *casts.out_shapes],
        scratch_shapes=[pltpu.VMEM((tm, k), BF16), pltpu.VMEM((tm, LANES), F32)],
        compiler_params=pltpu.CompilerParams(
            dimension_semantics=("parallel", "arbitrary"),
            vmem_limit_bytes=_vmem_limit(blocks, _nbytes((tm, k), BF16))),
        name=name,
    )(x, gain.reshape(1, k), w, *casts.args)
    return (out, *casts.finish(cast_out)) if casts.n else out


def _mm_res_kernel(a_ref, w_ref, r_ref, o_ref):
    @pl.when(pl.program_id(2) == 0)
    def _():
        o_ref[...] = r_ref[...] + jnp.dot(a_ref[...], w_ref[...], preferred_element_type=F32)

    @pl.when(pl.program_id(2) != 0)
    def _():
        o_ref[...] += jnp.dot(a_ref[...], w_ref[...], preferred_element_type=F32)


def _mm_res_full_kernel(a_ref, w_ref, r_ref, o_ref):
    o_ref[...] = r_ref[...] + jnp.dot(a_ref[...], w_ref[...], preferred_element_type=F32)


def _mm_res_full(a, w, layer, res, *, tm=1024, name):
    m, kdim = a.shape
    n = w.shape[2]
    assert m % tm == 0
    blocks = _nbytes((tm, kdim), BF16) + 2 * _nbytes((tm, n), F32)
    return pl.pallas_call(
        _mm_res_full_kernel,
        grid=(m // tm,),
        in_specs=[
            pl.BlockSpec((tm, kdim), lambda i: (i, 0)),
            pl.BlockSpec((None, kdim, n), lambda i: (layer, 0, 0),
                         pipeline_mode=pl.Buffered(buffer_count=1)),
            pl.BlockSpec((tm, n), lambda i: (i, 0)),
        ],
        out_specs=pl.BlockSpec((tm, n), lambda i: (i, 0)),
        out_shape=jax.ShapeDtypeStruct((m, n), F32),
        compiler_params=pltpu.CompilerParams(
            dimension_semantics=("parallel",),
            vmem_limit_bytes=_vmem_limit(blocks, _nbytes((kdim, n), BF16))),
        name=name,
    )(a, w, res)


def _mm_res(a, w, layer, res, *, tm=1024, tn=1024, tk=4096, name):
    m, kdim = a.shape
    n = w.shape[2]
    tk = min(tk, kdim)
    assert m % tm == 0 and n % tn == 0 and kdim % tk == 0
    nk = kdim // tk
    blocks = (_nbytes((tm, tk), BF16) + _nbytes((tk, tn), BF16) + 2 * _nbytes((tm, tn), F32))
    return pl.pallas_call(
        _mm_res_kernel,
        grid=(m // tm, n // tn, nk),
        in_specs=[
            pl.BlockSpec((tm, tk), lambda i, j, k: (i, k)),
            pl.BlockSpec((None, tk, tn), lambda i, j, k: (layer, k, j)),
            pl.BlockSpec((tm, tn), lambda i, j, k: (i, j)),
        ],
        out_specs=pl.BlockSpec((tm, tn), lambda i, j, k: (i, j)),
        out_shape=jax.ShapeDtypeStruct((m, n), F32),
        compiler_params=pltpu.CompilerParams(
            dimension_semantics=("parallel", "parallel", "arbitrary"),
            vmem_limit_bytes=_vmem_limit(blocks)),
        name=name,
    )(a, w, res)


def _mm2_res_kernel(*refs, n_cast):
    a1_ref, a2_ref, w1_ref, w2_ref, r_ref = refs[:5]
    o_ref = refs[5 + n_cast]
    _Casts.run(refs[5:5 + n_cast], refs[6 + n_cast:])
    acc = r_ref[...] + jnp.dot(a1_ref[...], w1_ref[...], preferred_element_type=F32)
    o_ref[...] = acc + jnp.dot(a2_ref[...], w2_ref[...], preferred_element_type=F32)


def _mm2_res(a1, a2, w, layer, res, *, tm=512, cast_weights=(), name):
    m, k1 = a1.shape
    k2 = a2.shape[1]
    n = w.shape[2]
    assert w.shape[1] == k1 + k2 and k1 % k2 == 0
    assert m % tm == 0
    resident = pl.Buffered(buffer_count=1)
    casts = _Casts(cast_weights, m // tm, lambda i: i)
    blocks = _nbytes((tm, k1 + k2), BF16) + 2 * _nbytes((tm, n), F32) + casts.block_bytes
    out, *cast_out = pl.pallas_call(
        functools.partial(_mm2_res_kernel, n_cast=casts.n),
        grid=(m // tm,),
        in_specs=[
            pl.BlockSpec((tm, k1), lambda i: (i, 0)),
            pl.BlockSpec((tm, k2), lambda i: (i, 0)),
            pl.BlockSpec((None, k1, n), lambda i: (layer, 0, 0), pipeline_mode=resident),
            pl.BlockSpec((None, k2, n), lambda i: (layer, k1 // k2, 0), pipeline_mode=resident),
            pl.BlockSpec((tm, n), lambda i: (i, 0)),
            *casts.in_specs,
        ],
        out_specs=[pl.BlockSpec((tm, n), lambda i: (i, 0)), *casts.out_specs],
        out_shape=[jax.ShapeDtypeStruct((m, n), F32), *casts.out_shapes],
        compiler_params=pltpu.CompilerParams(
            dimension_semantics=("parallel",),
            vmem_limit_bytes=_vmem_limit(blocks, _nbytes((k1 + k2, n), BF16))),
        name=name,
    )(a1, a2, w, w, res, *casts.args)
    return (out, *casts.finish(cast_out)) if casts.n else out


def _ret_kernel(q_ref, k_ref, v_ref, g_ref, cos_ref, sin_ref, gain_ref, lg_ref, o_ref,
                state_ref, dmat_ref, qd_ref, kd_ref, *, blk):
    lg = lg_ref[0][:, 0:1]

    state_ref[...] = jnp.zeros_like(state_ref)
    i = lax.broadcasted_iota(jnp.int32, (blk, blk), 0)
    j = lax.broadcasted_iota(jnp.int32, (blk, blk), 1)
    ci = i >> CHUNK_SHIFT
    cj = j >> CHUNK_SHIFT
    d = (i - j).astype(F32)
    expo = jnp.where(ci == cj, jnp.abs(d), d)
    dmat_ref[...] = jnp.where(ci >= cj, jnp.exp(lg * expo), 0.0)
    r = lax.broadcasted_iota(jnp.int32, (blk, RET_QK_DIM), 0).astype(F32)
    qd_ref[...] = jnp.exp(lg * r)
    kd_ref[...] = jnp.exp(lg * (blk - r))

    half = RET_QK_DIM // 2

    def block(n, carry):
        rows = pl.ds(pl.multiple_of(n * blk, blk), blk)
        c = cos_ref[0, rows, :]
        s = sin_ref[0, rows, :]

        def rope(x):
            x1 = x[:, :half]
            x2 = x[:, half:]
            return jnp.concatenate([x1 * c - x2 * s, x2 * c + x1 * s], axis=-1)

        q = rope(q_ref[0, rows, :].astype(F32))
        k = rope(k_ref[0, rows, :].astype(F32)) * (RET_QK_DIM ** -0.5)
        v = v_ref[0, rows, :]
        state = state_ref[...]

        scores = lax.dot_general(q.astype(BF16), k.astype(BF16), (((1,), (1,)), ((), ())),
                                 preferred_element_type=F32) * dmat_ref[...]
        out = jnp.dot(scores.astype(BF16), v, preferred_element_type=F32)
        out = out + jnp.dot((q * qd_ref[...]).astype(BF16), state.astype(BF16),
                            preferred_element_type=F32)
        kv = lax.dot_general((k * kd_ref[...]).astype(BF16), v, (((0,), (0,)), ((), ())),
                             preferred_element_type=F32)
        state_ref[...] = state * jnp.exp(lg * blk) + kv

        y = _rms(out, gain_ref[0])
        g = g_ref[0, rows, :].astype(F32)
        o_ref[0, rows, :] = (g / (1.0 + jnp.exp(-g)) * y).astype(o_ref.dtype)
        return carry

    lax.fori_loop(0, q_ref.shape[1] // blk, block, 0)


def _retention(proj, cos_r, sin_r, ret_gain, *, blk=512):
    b, s, _ = proj.shape
    h = RET_HEADS
    dk, dv = RET_QK_DIM, RET_V_DIM
    v_off = 2 * h * dk // dv
    g_off = v_off + h
    assert s % blk == 0
    log_g = jnp.log(1.0 - 2.0 ** (-5.0 - jnp.arange(h, dtype=F32)))
    log_g = jnp.broadcast_to(log_g[:, None, None], (h, 1, LANES))
    blocks = (2 * _nbytes((s, dk), BF16) + 3 * _nbytes((s, dv), BF16)
              + 2 * _nbytes((s, LANES), F32))
    scratch = (_nbytes((dk, dv), F32) + _nbytes((blk, blk), F32) + 2 * _nbytes((blk, dk), F32))
    return pl.pallas_call(
        functools.partial(_ret_kernel, blk=blk),
        grid=(b, h),
        in_specs=[
            pl.BlockSpec((1, s, dk), lambda bi, hi: (bi, 0, hi)),
            pl.BlockSpec((1, s, dk), lambda bi, hi: (bi, 0, h + hi)),
            pl.BlockSpec((1, s, dv), lambda bi, hi: (bi, 0, v_off + hi)),
            pl.BlockSpec((1, s, dv), lambda bi, hi: (bi, 0, g_off + hi)),
            pl.BlockSpec((1, s, LANES), lambda bi, hi: (bi, 0, 0)),
            pl.BlockSpec((1, s, LANES), lambda bi, hi: (bi, 0, 0)),
            pl.BlockSpec((1, 1, dv), lambda bi, hi: (hi, 0, 0)),
            pl.BlockSpec((1, 1, LANES), lambda bi, hi: (hi, 0, 0)),
        ],
        out_specs=pl.BlockSpec((1, s, dv), lambda bi, hi: (bi, 0, hi)),
        out_shape=jax.ShapeDtypeStruct((b, s, h * dv), BF16),
        scratch_shapes=[
            pltpu.VMEM((dk, dv), F32),
            pltpu.VMEM((blk, blk), F32),
            pltpu.VMEM((blk, dk), F32),
            pltpu.VMEM((blk, dk), F32),
        ],
        compiler_params=pltpu.CompilerParams(
            dimension_semantics=("parallel", "parallel"),
            vmem_limit_bytes=_vmem_limit(blocks, scratch)),
        name="retention",
    )(proj, proj, proj, proj, cos_r, sin_r, ret_gain.reshape(h, 1, dv), log_g)


def _sg_kernel(u_ref, v_ref, gain_ref, w_ref, b_ref, o_ref, *, rows):
    i = lax.broadcasted_iota(jnp.int32, (SG_BLOCK, SG_BLOCK), 0)
    j = lax.broadcasted_iota(jnp.int32, (SG_BLOCK, SG_BLOCK), 1)
    mask = (j >> CHUNK_SHIFT) <= (i >> CHUNK_SHIFT)
    for g in range(SG_GROUPS):
        cols = slice(g * SG_DIM, (g + 1) * SG_DIM)
        w = jnp.where(mask, w_ref[g], 0.0).astype(BF16)
        bias = b_ref[g]
        u = _gelu_tanh(u_ref[0, :, cols].astype(F32))
        v = _rms(_gelu_tanh(v_ref[0, :, cols].astype(F32)), gain_ref[:, cols]).astype(BF16)
        for n in range(rows // SG_BLOCK):
            r = slice(n * SG_BLOCK, (n + 1) * SG_BLOCK)
            mixed = jnp.dot(w, v[r], preferred_element_type=F32) + bias
            o_ref[0, r, cols] = (u[r] * mixed).astype(o_ref.dtype)


def _spatial_gate(proj, sg_norm, sg_w, sg_b, *, rows=1024):
    b, s, width = proj.shape
    sgw = SG_GROUPS * SG_DIM
    u_blk = (width - 2 * sgw) // sgw
    blocks = 3 * _nbytes((rows, sgw), BF16) + _nbytes(sg_w.shape, F32)
    return pl.pallas_call(
        functools.partial(_sg_kernel, rows=rows),
        grid=(b, s // rows),
        in_specs=[
            pl.BlockSpec((1, rows, sgw), lambda bi, n: (bi, n, u_blk)),
            pl.BlockSpec((1, rows, sgw), lambda bi, n: (bi, n, u_blk + 1)),
            pl.BlockSpec((1, sgw), lambda bi, n: (0, 0)),
            pl.BlockSpec((SG_GROUPS, SG_BLOCK, SG_BLOCK), lambda bi, n: (0, 0, 0)),
            pl.BlockSpec((SG_GROUPS, SG_BLOCK, 1), lambda bi, n: (0, 0, 0)),
        ],
        out_specs=pl.BlockSpec((1, rows, sgw), lambda bi, n: (bi, n, 0)),
        out_shape=jax.ShapeDtypeStruct((b, s, sgw), BF16),
        compiler_params=pltpu.CompilerParams(
            dimension_semantics=("parallel", "parallel"),
            vmem_limit_bytes=_vmem_limit(blocks)),
        name="spatial_gate",
    )(proj, proj, sg_norm.reshape(1, sgw), sg_w, sg_b.reshape(SG_GROUPS, SG_BLOCK, 1))


def _diff_attn_kernel(q0_ref, k0_ref, cos0_ref, sin0_ref, q1_ref, k1_ref, cos1_ref, sin1_ref,
                      q2_ref, k2_ref, cos2_ref, sin2_ref,
                      v_ref, qg_ref, kg_ref, sg_ref, lam_ref, o_ref,
                      qp_ref, kn_ref, bias_ref, sa_ref, sb_ref, m_ref, l_ref, acc_ref,
                      *, tq, nq, lambda_init):
    qi = pl.program_id(2)
    d = DIFF_HEAD_DIM
    n_lane_tiles = tq // LANES
    prep_rows = min(tq, 256)
    q_scale = d ** -0.5 * math.log2(math.e)

    def prepare(q_ref, k_ref, cos_ref, sin_ref, tile, qbuf):
        ii = lax.broadcasted_iota(jnp.int32, (2 * d, 2 * d), 0)
        jj = lax.broadcasted_iota(jnp.int32, (2 * d, 2 * d), 1)
        swap_halves = jnp.where((ii ^ (d // 2)) == jj, 1.0, 0.0).astype(BF16)

        def norm_rope(x, gain_cos, gain_sin):
            xf = x.astype(F32)
            rot = jnp.dot(x, swap_halves, preferred_element_type=F32)
            out = []
            for mi in range(2):
                lanes = slice(mi * d, (mi + 1) * d)
                xm = xf[:, lanes]
                r = lax.rsqrt(jnp.mean(xm * xm, axis=-1, keepdims=True) + NORM_EPS)
                out.append(r * (xm * gain_cos + rot[:, lanes] * gain_sin))
            return jnp.concatenate(out, axis=-1).astype(BF16)

        def rolled(g):
            return pltpu.roll(jnp.broadcast_to(g, (8, d)), d // 2, 1)[0:1]

        qg = qg_ref[...] * q_scale
        kg = kg_ref[...]
        qg_rot = rolled(qg)
        kg_rot = rolled(kg)
        base = pl.multiple_of(tile * tq, tq)
        for r0 in range(0, tq, prep_rows):
            rows = slice(r0, r0 + prep_rows)
            c = cos_ref[0, rows, :]
            s = sin_ref[0, rows, :]
            qp_ref[qbuf, rows, :] = norm_rope(q_ref[0, rows, :], qg * c, qg_rot * s)
            kn_ref[pl.ds(base + r0, prep_rows), :] = norm_rope(k_ref[0, rows, :], kg * c,
                                                               kg_rot * s)

    def scores(s_ref, slot, kv, width, masked=False):
        k0 = pl.multiple_of(kv * tq, tq)
        kt = kn_ref[pl.ds(k0, width * tq), :]
        for mi in range(2):
            lanes = slice(mi * d, (mi + 1) * d)
            s = lax.dot_general(qp_ref[slot, :, lanes], kt[:, lanes], (((1,), (1,)), ((), ())),
                                preferred_element_type=F32)
            if masked:
                s = s + bias_ref[...]
            for w in range(width):
                s_ref[mi, kv + w] = s[:, w * tq:(w + 1) * tq]
            m = m_ref[slot, mi]
            for c in range(width * n_lane_tiles):
                m = jnp.maximum(m, s[:, c * LANES:(c + 1) * LANES])
            m_ref[slot, mi] = m

    def finish_max(slot):
        for mi in range(2):
            m_row = jnp.max(m_ref[slot, mi], axis=-1, keepdims=True)
            m_ref[slot, mi] = jnp.broadcast_to(m_row, (tq, LANES))

    def weighted_values(s_ref, slot, kv, width):
        k0 = pl.multiple_of(kv * tq, tq)
        vt = v_ref[0, pl.ds(k0, width * tq), :]
        for mi in range(2):
            m = m_ref[slot, mi]
            l = l_ref[mi]
            ps = []
            for w in range(width):
                for c in range(n_lane_tiles):
                    p = jnp.exp2(s_ref[mi, kv + w, :, c * LANES:(c + 1) * LANES] - m)
                    l = l + p
                    ps.append(p.astype(BF16))
            l_ref[mi] = l
            acc_ref[mi] += jnp.dot(jnp.concatenate(ps, axis=-1), vt,
                                   preferred_element_type=F32)

    def run(cur_ref, nxt_ref, slot):
        nslot = 1 - slot
        odd_tile_count = slot == 0
        n_pairs = (qi + 1) >> 1
        l_ref[...] = jnp.zeros_like(l_ref)
        acc_ref[...] = jnp.zeros_like(acc_ref)

        @pl.when(qi < nq - 1)
        def _():
            m_ref[nslot] = jnp.full(m_ref.shape[1:], MASK_VALUE, F32)

            def body(pair, carry):
                weighted_values(cur_ref, slot, 2 * pair, 2)
                scores(nxt_ref, nslot, 2 * pair, 2)
                return carry

            lax.fori_loop(0, n_pairs, body, 0)

        def finish_next_tile(prepare_ahead):
            if odd_tile_count:
                weighted_values(cur_ref, slot, qi, 1)
                scores(nxt_ref, nslot, qi, 1)
            if prepare_ahead:
                prepare(q2_ref, k2_ref, cos2_ref, sin2_ref, qi + 2, slot)
            scores(nxt_ref, nslot, qi + 1, 1, masked=True)
            finish_max(nslot)

        @pl.when(qi < nq - 2)
        def _():
            finish_next_tile(True)

        if (nq - 2) % 2 == slot:
            @pl.when(qi == nq - 2)
            def _():
                finish_next_tile(False)

        if (nq - 1) % 2 == slot:
            @pl.when(qi == nq - 1)
            def _():
                def body(pair, carry):
                    weighted_values(cur_ref, slot, 2 * pair, 2)
                    return carry

                lax.fori_loop(0, n_pairs, body, 0)
                if odd_tile_count:
                    weighted_values(cur_ref, slot, qi, 1)

    @pl.when(qi == 0)
    def _():
        row = lax.broadcasted_iota(jnp.int32, (tq, tq), 0)
        col = lax.broadcasted_iota(jnp.int32, (tq, tq), 1)
        bias_ref[...] = jnp.where((col >> CHUNK_SHIFT) <= (row >> CHUNK_SHIFT), 0.0, MASK_VALUE)
        prepare(q0_ref, k0_ref, cos0_ref, sin0_ref, 0, 0)
        prepare(q1_ref, k1_ref, cos1_ref, sin1_ref, 1, 1)
        m_ref[0] = jnp.full(m_ref.shape[1:], MASK_VALUE, F32)
        scores(sa_ref, 0, 0, 1, masked=True)
        finish_max(0)

    @pl.when((qi & 1) == 0)
    def _():
        run(sa_ref, sb_ref, 0)

    @pl.when((qi & 1) == 1)
    def _():
        run(sb_ref, sa_ref, 1)

    lam_p = lam_ref[...]
    lam = (jnp.exp(jnp.sum(lam_p[0:1] * lam_p[1:2], axis=-1, keepdims=True))
           - jnp.exp(jnp.sum(lam_p[2:3] * lam_p[3:4], axis=-1, keepdims=True))
           + lambda_init)
    l0 = jnp.sum(l_ref[0], axis=-1, keepdims=True)
    l1 = jnp.sum(l_ref[1], axis=-1, keepdims=True)
    out = acc_ref[0] / l0 - lam * (acc_ref[1] / l1)
    o_ref[0] = (_rms(out, sg_ref[...]) * (1.0 - lambda_init)).astype(o_ref.dtype)


def _diff_attention(proj, cos_d, sin_d, q_gain, k_gain, sub_gain, lam_params, lambda_init,
                    *, tq=512):
    b, s, _ = proj.shape
    h = DIFF_HEADS
    d = DIFF_HEAD_DIM
    hw = 2 * d
    nq = s // tq
    assert nq >= 2
    blocks = (7 * _nbytes((tq, hw), BF16) + 6 * _nbytes((tq, d), F32) + _nbytes((s, hw), BF16))
    scratch = (_nbytes((2, tq, hw), BF16) + _nbytes((s, hw), BF16) + 2 * _nbytes((2, tq, s), F32)
               + _nbytes((2, 2, tq, LANES), F32) + _nbytes((2, tq, LANES), F32)
               + _nbytes((2, tq, hw), F32))
    fixed = lambda tile, col: (lambda bi, hi, qi: (bi, tile, col(hi)))
    ahead = lambda col: (lambda bi, hi, qi: (bi, jnp.minimum(qi + 2, nq - 1), col(hi)))
    q_col = lambda hi: hi
    k_col = lambda hi: h + hi
    rope_col = lambda hi: 0
    tile_specs = lambda index: [
        pl.BlockSpec((1, tq, hw), index(q_col)),
        pl.BlockSpec((1, tq, hw), index(k_col)),
        pl.BlockSpec((1, tq, d), index(rope_col)),
        pl.BlockSpec((1, tq, d), index(rope_col)),
    ]
    tile_args = (proj, proj, cos_d, sin_d)
    return pl.pallas_call(
        functools.partial(_diff_attn_kernel, tq=tq, nq=nq, lambda_init=lambda_init),
        grid=(b, h, nq),
        in_specs=[
            *tile_specs(functools.partial(fixed, 0)),
            *tile_specs(functools.partial(fixed, 1)),
            *tile_specs(ahead),
            pl.BlockSpec((1, s, hw), lambda bi, hi, qi: (bi, 0, 2 * h + hi)),
            pl.BlockSpec((1, d), lambda bi, hi, qi: (0, 0)),
            pl.BlockSpec((1, d), lambda bi, hi, qi: (0, 0)),
            pl.BlockSpec((1, hw), lambda bi, hi, qi: (0, 0)),
            pl.BlockSpec((4, d), lambda bi, hi, qi: (0, 0)),
        ],
        out_specs=pl.BlockSpec((1, tq, hw), lambda bi, hi, qi: (bi, qi, hi)),
        out_shape=jax.ShapeDtypeStruct((b, s, h * hw), BF16),
        scratch_shapes=[
            pltpu.VMEM((2, tq, hw), BF16),
            pltpu.VMEM((s, hw), BF16),
            pltpu.VMEM((tq, tq), F32),
            pltpu.VMEM((2, nq, tq, tq), F32),
            pltpu.VMEM((2, nq, tq, tq), F32),
            pltpu.VMEM((2, 2, tq, LANES), F32),
            pltpu.VMEM((2, tq, LANES), F32),
            pltpu.VMEM((2, tq, hw), F32),
        ],
        compiler_params=pltpu.CompilerParams(
            dimension_semantics=("parallel", "parallel", "arbitrary"),
            vmem_limit_bytes=_vmem_limit(blocks, scratch)),
        name="diff_attention",
    )(*tile_args, *tile_args, *tile_args, proj,
      q_gain.reshape(1, d), k_gain.reshape(1, d), sub_gain.reshape(1, hw), lam_params)


def _xa_kernel(*refs, heads, n_cast):
    x_ref, g_ref, wq_ref, kv_ref, qg_ref, kg_ref, wo_ref = refs[:7]
    o_ref = refs[7 + n_cast]
    kn_ref = refs[-1]
    dm = x_ref.shape[2]
    hd = dm // heads

    @pl.when(pl.program_id(1) == 0)
    def _():
        for h in range(heads):
            cols = slice(h * hd, (h + 1) * hd)
            kn_ref[:, cols] = _rms(kv_ref[0, :, cols].astype(F32), kg_ref[...]).astype(BF16)

    _Casts.run(refs[7:7 + n_cast], refs[8 + n_cast:8 + 2 * n_cast])
    x = x_ref[0]
    r = lax.rsqrt(jnp.mean(x * x, axis=-1, keepdims=True) + NORM_EPS)
    q = jnp.dot((x * g_ref[...]).astype(BF16), wq_ref[...], preferred_element_type=F32)
    scale = hd ** -0.5
    heads_out = []
    for h in range(heads):
        cols = slice(h * hd, (h + 1) * hd)
        qn = (_rms(q[:, cols] * r, qg_ref[...]) * scale).astype(BF16)
        s = lax.dot_general(qn, kn_ref[:, cols], (((1,), (1,)), ((), ())),
                            preferred_element_type=F32)
        e = jnp.exp(s - jnp.max(s, axis=-1, keepdims=True))
        p = e / jnp.sum(e, axis=-1, keepdims=True)
        v = kv_ref[0, :, dm + h * hd: dm + (h + 1) * hd]
        heads_out.append(jnp.dot(p.astype(BF16), v, preferred_element_type=F32).astype(BF16))
    att = jnp.concatenate(heads_out, axis=-1)
    o_ref[0] = x + jnp.dot(att, wo_ref[...], preferred_element_type=F32)


def _cross_attention_block(x, gain, wq, wo, layer, kv, q_gain, k_gain, *, tm=512,
                           cast_weights=()):
    b, s, dm = x.shape
    mlen = kv.shape[1]
    hd = dm // XA_HEADS
    ni = s // tm
    resident = pl.Buffered(buffer_count=1)
    casts = _Casts(cast_weights, b * ni, lambda bi, i: bi * ni + i)
    blocks = 2 * _nbytes((tm, dm), F32) + _nbytes((mlen, 2 * dm), BF16) + casts.block_bytes
    scratch = 2 * _nbytes((dm, dm), BF16) + _nbytes((mlen, dm), BF16)
    out, *cast_out = pl.pallas_call(
        functools.partial(_xa_kernel, heads=XA_HEADS, n_cast=casts.n),
        grid=(b, ni),
        in_specs=[
            pl.BlockSpec((1, tm, dm), lambda bi, i: (bi, i, 0)),
            pl.BlockSpec((1, dm), lambda bi, i: (0, 0)),
            pl.BlockSpec((None, dm, dm), lambda bi, i: (layer, 0, 0), pipeline_mode=resident),
            pl.BlockSpec((1, mlen, 2 * dm), lambda bi, i: (bi, 0, 0)),
            pl.BlockSpec((1, hd), lambda bi, i: (0, 0)),
            pl.BlockSpec((1, hd), lambda bi, i: (0, 0)),
            pl.BlockSpec((None, dm, dm), lambda bi, i: (layer, 0, 0), pipeline_mode=resident),
            *casts.in_specs,
        ],
        out_specs=[pl.BlockSpec((1, tm, dm), lambda bi, i: (bi, i, 0)), *casts.out_specs],
        out_shape=[jax.ShapeDtypeStruct((b, s, dm), F32), *casts.out_shapes],
        scratch_shapes=[pltpu.VMEM((mlen, dm), BF16)],
        compiler_params=pltpu.CompilerParams(
            dimension_semantics=("parallel", "arbitrary"),
            vmem_limit_bytes=_vmem_limit(blocks, scratch)),
        name="cross_attention_block",
    )(x, gain.reshape(1, dm), wq, kv, q_gain.reshape(1, hd), k_gain.reshape(1, hd), wo,
      *casts.args)
    return (out, *casts.finish(cast_out)) if casts.n else out


def kernel(x, mem, positions, norm_mix, norm_xa, norm_mem, norm_ffn, ev_w_in, ev_ret_gain,
           ev_sg_norm, ev_sg_w, ev_sg_b, ev_w_out, od_w_qkv, od_q_gain, od_k_gain, od_lam_q1,
           od_lam_k1, od_lam_q2, od_lam_k2, od_sub_gain, od_w_o, xa_w_q, xa_w_kv, xa_q_gain,
           xa_k_gain, xa_w_o, ffn_w1, ffn_w2):
    b, s, dm = x.shape
    mlen = mem.shape[1]
    depth = norm_mix.shape[0]
    m = b * s
    assert depth == 2, "the weight-cast schedule below is laid out for one even + one odd layer"
    cos_r, sin_r, cos_d, sin_d, ev_w_in = _rope_tables(positions, cast_weights=(ev_w_in,))
    xf = x.reshape(m, dm)
    mem_f = mem.reshape(b * mlen, dm)

    proj, ev_w_out, xa_w_q, xa_w_o = _norm_mm(
        xf, norm_mix[0], ev_w_in, 0, cast_weights=(ev_w_out, xa_w_q, xa_w_o),
        name="even_in_proj")
    proj = proj.reshape(b, s, -1)
    out_a = _retention(proj, cos_r, sin_r, ev_ret_gain[0])
    out_b = _spatial_gate(proj, ev_sg_norm[0], ev_sg_w[0], ev_sg_b[0])
    xf, ffn_w1 = _mm2_res(out_a.reshape(m, -1), out_b.reshape(m, -1), ev_w_out, 0, xf,
                          cast_weights=(ffn_w1,), name="even_out_proj")

    for li in range(depth):
        if li == 1:
            lambda_init = 0.8 - 0.6 * math.exp(-0.3 * li)
            proj = _norm_mm(xf, norm_mix[li], od_w_qkv, 0, name="diff_qkv_proj")
            lam_params = jnp.stack([od_lam_q1[0], od_lam_k1[0], od_lam_q2[0], od_lam_k2[0]])
            att = _diff_attention(proj.reshape(b, s, -1), cos_d, sin_d, od_q_gain[0],
                                  od_k_gain[0], od_sub_gain[0], lam_params, lambda_init)
            xf = _mm_res_full(att.reshape(m, -1), od_w_o, 0, xf, name="diff_out_proj")

        kv = _norm_mm(mem_f, norm_mem[li], xa_w_kv, li, tn=1024, name="xa_kv_proj")
        xa_out = _cross_attention_block(
            xf.reshape(b, s, dm), norm_xa[li], xa_w_q, xa_w_o, li, kv.reshape(b, mlen, 2 * dm),
            xa_q_gain[li], xa_k_gain[li], cast_weights=(od_w_qkv, od_w_o) if li == 0 else ())
        if li == 0:
            xa_out, od_w_qkv, od_w_o = xa_out
        xf = xa_out.reshape(m, dm)

        hid = _norm_mm(xf, norm_ffn[li], ffn_w1, li, act="relu2",
                       cast_weights=(ffn_w2,) if li == 0 else (), name="ffn_up")
        if li == 0:
            hid, ffn_w2 = hid
        xf = _mm_res(hid, ffn_w2, li, xf, name="ffn_down")

    return xf.reshape(b, s, dm)
```

```python
import functools
import math

import jax
import jax.numpy as jnp
from jax import lax
from jax.experimental import pallas as pl
from jax.experimental.pallas import tpu as pltpu

F32 = jnp.float32
BF16 = jnp.bfloat16

CHUNK = 64
CHUNK_SHIFT = CHUNK.bit_length() - 1
ROPE_THETA = 10000.0
NORM_EPS = 1e-6
RET_HEADS = 4
RET_QK_DIM = 256
RET_V_DIM = 512
SG_GROUPS = 4
SG_DIM = 256
SG_BLOCK = 128
DIFF_HEADS = 8
DIFF_HEAD_DIM = 128
XA_HEADS = 4
MASK_VALUE = -1e30

V7X_VMEM_BYTES = 64 * 1024 * 1024
VMEM_REQUEST_CAP = V7X_VMEM_BYTES - 8 * 1024 * 1024
LANES = 128


def _vmem_limit(block_bytes, scratch_bytes=0):
    need = 2 * block_bytes + scratch_bytes
    return int(min(VMEM_REQUEST_CAP, max(32 * 1024 * 1024, 2 * need)))


def _nbytes(shape, dtype):
    return math.prod(shape) * jnp.dtype(dtype).itemsize


def _rms(x, gain):
    ms = jnp.mean(x * x, axis=-1, keepdims=True)
    return x * lax.rsqrt(ms + NORM_EPS) * gain


def _gelu_tanh(x):
    c = math.sqrt(2.0 / math.pi)
    return 0.5 * x * (1.0 + jnp.tanh(c * (x + 0.044715 * (x * x * x))))


class _Casts:
    def __init__(self, weights, steps, step_of):
        self.weights = weights
        self.n = len(weights)
        self.args, self.in_specs, self.out_specs, self.out_shapes = [], [], [], []
        self.block_bytes = 0
        for w in weights:
            flat = w.reshape(-1, w.shape[-1])
            rows = flat.shape[0] // steps
            assert rows * steps == flat.shape[0] and rows % 16 == 0
            block = (rows, flat.shape[1])
            index = lambda *grid_ids: (step_of(*grid_ids), 0)
            self.args.append(flat)
            self.in_specs.append(pl.BlockSpec(block, index))
            self.out_specs.append(pl.BlockSpec(block, index))
            self.out_shapes.append(jax.ShapeDtypeStruct(flat.shape, BF16))
            self.block_bytes += _nbytes(block, F32) + _nbytes(block, BF16)

    @staticmethod
    def run(src_refs, dst_refs):
        for src, dst in zip(src_refs, dst_refs):
            dst[...] = src[...].astype(dst.dtype)

    def finish(self, outs):
        return [o.reshape(w.shape) for o, w in zip(outs, self.weights)]


def _rope_kernel(*refs, n_cast, rows):
    first_ref, inv_r_ref, inv_d_ref = refs[:3]
    out_refs = refs[3 + n_cast:7 + n_cast]
    base_ref = refs[-1]
    _Casts.run(refs[3:3 + n_cast], refs[7 + n_cast:7 + 2 * n_cast])
    bi = pl.program_id(0)
    blk = pl.program_id(1)
    invs = (inv_r_ref[...], inv_d_ref[...])

    @pl.when(bi == 0)
    def _():
        t = (blk * rows + lax.broadcasted_iota(jnp.int32, (rows, LANES), 0)).astype(F32)
        for a, inv in enumerate(invs):
            base_ref[2 * a, blk] = jnp.cos(t * inv)
            base_ref[2 * a + 1, blk] = jnp.sin(t * inv)

    first = jnp.broadcast_to(first_ref[0], (8, LANES)).astype(F32)
    lane = lax.broadcasted_iota(jnp.int32, (rows, LANES), 1)
    for a, inv in enumerate(invs):
        c0 = jnp.cos(first * inv)[0:1]
        s0 = jnp.sin(first * inv)[0:1]
        ct = base_ref[2 * a, blk]
        st = base_ref[2 * a + 1, blk]
        out_refs[2 * a][...] = c0 * ct - s0 * st
        sin_sum = s0 * ct + c0 * st
        if a == 1:
            sin_sum = jnp.where(lane < DIFF_HEAD_DIM // 2, -sin_sum, sin_sum)
        out_refs[2 * a + 1][...] = sin_sum


def _rope_tables(positions, cast_weights=()):
    b, s = positions.shape
    rows = 1024
    nblk = s // rows
    n = b * s
    inv_r = ROPE_THETA ** (-jnp.arange(0, RET_QK_DIM, 2, dtype=F32) / RET_QK_DIM)
    inv_d = ROPE_THETA ** (-jnp.arange(0, DIFF_HEAD_DIM, 2, dtype=F32) / DIFF_HEAD_DIM)
    inv_d = jnp.concatenate([inv_d, inv_d])
    out = jax.ShapeDtypeStruct((n, LANES), F32)
    tab_spec = pl.BlockSpec((rows, LANES), lambda bi, blk: (bi * nblk + blk, 0))
    vec_spec = pl.BlockSpec((1, LANES), lambda bi, blk: (0, 0))
    casts = _Casts(cast_weights, b * nblk, lambda bi, blk: bi * nblk + blk)
    base_bytes = _nbytes((4, nblk, rows, LANES), F32)
    outs = pl.pallas_call(
        functools.partial(_rope_kernel, n_cast=casts.n, rows=rows),
        grid=(b, nblk),
        in_specs=[pl.BlockSpec((1, 1, 1), lambda bi, blk: (bi, 0, 0)), vec_spec, vec_spec,
                  *casts.in_specs],
        out_specs=[tab_spec] * 4 + casts.out_specs,
        out_shape=[out] * 4 + casts.out_shapes,
        scratch_shapes=[pltpu.VMEM((4, nblk, rows, LANES), F32)],
        compiler_params=pltpu.CompilerParams(
            dimension_semantics=("arbitrary", "arbitrary"),
            vmem_limit_bytes=_vmem_limit(casts.block_bytes + 4 * _nbytes((rows, LANES), F32),
                                         base_bytes)),
        name="rope_tables",
    )(positions[:, :1].reshape(b, 1, 1), inv_r.reshape(1, LANES), inv_d.reshape(1, LANES),
      *casts.args)
    return [t.reshape(b, s, LANES) for t in outs[:4]] + casts.finish(outs[4:])


def _norm_mm_kernel(*refs, act, n_cast):
    x_ref, g_ref, w_ref = refs[:3]
    cast_src = refs[3:3 + n_cast]
    o_ref = refs[3 + n_cast]
    cast_dst = refs[4 + n_cast:4 + 2 * n_cast]
    xg_ref, r_ref = refs[4 + 2 * n_cast:]

    def finish(y):
        r = r_ref[...]
        for c in range(y.shape[1] // LANES):
            lanes = slice(c * LANES, (c + 1) * LANES)
            z = y[:, lanes] * r
            if act == "relu2":
                z = jnp.maximum(z, 0.0)
                z = z * z
            o_ref[:, lanes] = z.astype(o_ref.dtype)

    @pl.when(pl.program_id(1) == 0)
    def _():
        x = x_ref[...]
        xg = (x * g_ref[...]).astype(BF16)
        xg_ref[...] = xg
        ms = jnp.mean(x * x, axis=-1, keepdims=True)
        r_ref[...] = jnp.broadcast_to(lax.rsqrt(ms + NORM_EPS), r_ref.shape)
        _Casts.run(cast_src, cast_dst)
        finish(jnp.dot(xg, w_ref[...].astype(BF16), preferred_element_type=F32))

    @pl.when(pl.program_id(1) != 0)
    def _():
        _Casts.run(cast_src, cast_dst)
        finish(jnp.dot(xg_ref[...], w_ref[...].astype(BF16), preferred_element_type=F32))


def _norm_mm(x, gain, w, layer, *, act=None, tm=1024, tn=2048, cast_weights=(), name):
    m, k = x.shape
    n = w.shape[2]
    tm = min(tm, m)
    assert m % tm == 0 and n % tn == 0
    nj = n // tn
    casts = _Casts(cast_weights, (m // tm) * nj, lambda i, j: i * nj + j)
    blocks = (_nbytes((tm, k), F32) + _nbytes((k, tn), w.dtype) + _nbytes((tm, tn), BF16)
              + _nbytes((1, k), F32) + casts.block_bytes)
    out, *cast_out = pl.pallas_call(
        functools.partial(_norm_mm_kernel, act=act, n_cast=casts.n),
        grid=(m // tm, nj),
        in_specs=[
            pl.BlockSpec((tm, k), lambda i, j: (i, 0)),
            pl.BlockSpec((1, k), lambda i, j: (0, 0)),
            pl.BlockSpec((None, k, tn), lambda i, j: (layer, 0, j)),
            *casts.in_specs,
        ],
        out_specs=[pl.BlockSpec((tm, tn), lambda i, j: (i, j)), *casts.out_specs],
        out_shape=[jax.ShapeDtypeStruct((m, n), BF16), *casts.out_shapes],
        scratch_shapes=[pltpu.VMEM((tm, k), BF16), pltpu.VMEM((tm, LANES), F32)],
        compiler_params=pltpu.CompilerParams(
            dimension_semantics=("parallel", "arbitrary"),
            vmem_limit_bytes=_vmem_limit(blocks, _nbytes((tm, k), BF16))),
        name=name,
    )(x, gain.reshape(1, k), w, *casts.args)
    return (out, *casts.finish(cast_out)) if casts.n else out


def _norm_mm_all_layers(x, gains, w, *, tn=1024, name):
    m, k = x.shape
    layers, _, n = w.shape
    assert n % tn == 0
    blocks = (_nbytes((m, k), F32) + _nbytes((k, tn), w.dtype) + _nbytes((m, tn), BF16)
              + _nbytes((1, k), F32))
    return pl.pallas_call(
        functools.partial(_norm_mm_kernel, act=None, n_cast=0),
        grid=(layers, n // tn),
        in_specs=[
            pl.BlockSpec((m, k), lambda l, j: (0, 0)),
            pl.BlockSpec((None, 1, k), lambda l, j: (l, 0, 0)),
            pl.BlockSpec((None, k, tn), lambda l, j: (l, 0, j)),
        ],
        out_specs=pl.BlockSpec((None, m, tn), lambda l, j: (l, 0, j)),
        out_shape=jax.ShapeDtypeStruct((layers, m, n), BF16),
        scratch_shapes=[pltpu.VMEM((m, k), BF16), pltpu.VMEM((m, LANES), F32)],
        compiler_params=pltpu.CompilerParams(
            dimension_semantics=("arbitrary", "arbitrary"),
            vmem_limit_bytes=_vmem_limit(blocks, _nbytes((m, k), BF16))),
        name=name,
    )(x, gains.reshape(layers, 1, k), w)


def _mm_res_kernel(a_ref, w_ref, r_ref, o_ref):
    @pl.when(pl.program_id(2) == 0)
    def _():
        o_ref[...] = r_ref[...] + jnp.dot(a_ref[...], w_ref[...], preferred_element_type=F32)

    @pl.when(pl.program_id(2) != 0)
    def _():
        o_ref[...] += jnp.dot(a_ref[...], w_ref[...], preferred_element_type=F32)


def _mm_res_full_kernel(a_ref, w_ref, r_ref, o_ref):
    o_ref[...] = r_ref[...] + jnp.dot(a_ref[...], w_ref[...], preferred_element_type=F32)


def _mm_res_full(a, w, layer, res, *, tm=1024, name):
    m, kdim = a.shape
    n = w.shape[2]
    assert m % tm == 0
    blocks = _nbytes((tm, kdim), BF16) + 2 * _nbytes((tm, n), F32)
    return pl.pallas_call(
        _mm_res_full_kernel,
        grid=(m // tm,),
        in_specs=[
            pl.BlockSpec((tm, kdim), lambda i: (i, 0)),
            pl.BlockSpec((None, kdim, n), lambda i: (layer, 0, 0),
                         pipeline_mode=pl.Buffered(buffer_count=1)),
            pl.BlockSpec((tm, n), lambda i: (i, 0)),
        ],
        out_specs=pl.BlockSpec((tm, n), lambda i: (i, 0)),
        out_shape=jax.ShapeDtypeStruct((m, n), F32),
        compiler_params=pltpu.CompilerParams(
            dimension_semantics=("parallel",),
            vmem_limit_bytes=_vmem_limit(blocks, _nbytes((kdim, n), BF16))),
        name=name,
    )(a, w, res)


def _mm_res(a, w, layer, res, *, tm=1024, tn=1024, tk=4096, name):
    m, kdim = a.shape
    n = w.shape[2]
    tk = min(tk, kdim)
    assert m % tm == 0 and n % tn == 0 and kdim % tk == 0
    nk = kdim // tk
    blocks = (_nbytes((tm, tk), BF16) + _nbytes((tk, tn), BF16) + 2 * _nbytes((tm, tn), F32))
    return pl.pallas_call(
        _mm_res_kernel,
        grid=(m // tm, n // tn, nk),
        in_specs=[
            pl.BlockSpec((tm, tk), lambda i, j, k: (i, k)),
            pl.BlockSpec((None, tk, tn), lambda i, j, k: (layer, k, j)),
            pl.BlockSpec((tm, tn), lambda i, j, k: (i, j)),
        ],
        out_specs=pl.BlockSpec((tm, tn), lambda i, j, k: (i, j)),
        out_shape=jax.ShapeDtypeStruct((m, n), F32),
        compiler_params=pltpu.CompilerParams(
            dimension_semantics=("parallel", "parallel", "arbitrary"),
            vmem_limit_bytes=_vmem_limit(blocks)),
        name=name,
    )(a, w, res)


def _mm2_res_kernel(*refs, n_cast):
    a1_ref, a2_ref, w1_ref, w2_ref, r_ref = refs[:5]
    o_ref = refs[5 + n_cast]
    _Casts.run(refs[5:5 + n_cast], refs[6 + n_cast:])
    acc = r_ref[...] + jnp.dot(a1_ref[...], w1_ref[...], preferred_element_type=F32)
    o_ref[...] = acc + jnp.dot(a2_ref[...], w2_ref[...], preferred_element_type=F32)


def _mm2_res(a1, a2, w, layer, res, *, tm=512, cast_weights=(), name):
    m, k1 = a1.shape
    k2 = a2.shape[1]
    n = w.shape[2]
    assert w.shape[1] == k1 + k2 and k1 % k2 == 0
    assert m % tm == 0
    resident = pl.Buffered(buffer_count=1)
    casts = _Casts(cast_weights, m // tm, lambda i: i)
    blocks = _nbytes((tm, k1 + k2), BF16) + 2 * _nbytes((tm, n), F32) + casts.block_bytes
    out, *cast_out = pl.pallas_call(
        functools.partial(_mm2_res_kernel, n_cast=casts.n),
        grid=(m // tm,),
        in_specs=[
            pl.BlockSpec((tm, k1), lambda i: (i, 0)),
            pl.BlockSpec((tm, k2), lambda i: (i, 0)),
            pl.BlockSpec((None, k1, n), lambda i: (layer, 0, 0), pipeline_mode=resident),
            pl.BlockSpec((None, k2, n), lambda i: (layer, k1 // k2, 0), pipeline_mode=resident),
            pl.BlockSpec((tm, n), lambda i: (i, 0)),
            *casts.in_specs,
        ],
        out_specs=[pl.BlockSpec((tm, n), lambda i: (i, 0)), *casts.out_specs],
        out_shape=[jax.ShapeDtypeStruct((m, n), F32), *casts.out_shapes],
        compiler_params=pltpu.CompilerParams(
            dimension_semantics=("parallel",),
            vmem_limit_bytes=_vmem_limit(blocks, _nbytes((k1 + k2, n), BF16))),
        name=name,
    )(a1, a2, w, w, res, *casts.args)
    return (out, *casts.finish(cast_out)) if casts.n else out


def _ret_kernel(q_ref, k_ref, v_ref, g_ref, cos_ref, sin_ref, gain_ref, lg_ref, o_ref,
                state_ref, dmat_ref, qd_ref, kd_ref, *, blk):
    lg = lg_ref[0][:, 0:1]

    state_ref[...] = jnp.zeros_like(state_ref)
    i = lax.broadcasted_iota(jnp.int32, (blk, blk), 0)
    j = lax.broadcasted_iota(jnp.int32, (blk, blk), 1)
    ci = i >> CHUNK_SHIFT
    cj = j >> CHUNK_SHIFT
    d = (i - j).astype(F32)
    expo = jnp.where(ci == cj, jnp.abs(d), d)
    dmat_ref[...] = jnp.where(ci >= cj, jnp.exp(lg * expo), 0.0)
    r = lax.broadcasted_iota(jnp.int32, (blk, RET_QK_DIM), 0).astype(F32)
    qd_ref[...] = jnp.exp(lg * r)
    kd_ref[...] = jnp.exp(lg * (blk - r))

    half = RET_QK_DIM // 2

    def block(n, carry):
        rows = pl.ds(pl.multiple_of(n * blk, blk), blk)
        c = cos_ref[0, rows, :]
        s = sin_ref[0, rows, :]

        def rope(x):
            x1 = x[:, :half]
            x2 = x[:, half:]
            return jnp.concatenate([x1 * c - x2 * s, x2 * c + x1 * s], axis=-1)

        q = rope(q_ref[0, rows, :].astype(F32))
        k = rope(k_ref[0, rows, :].astype(F32)) * (RET_QK_DIM ** -0.5)
        v = v_ref[0, rows, :]
        state = state_ref[...]

        scores = lax.dot_general(q.astype(BF16), k.astype(BF16), (((1,), (1,)), ((), ())),
                                 preferred_element_type=F32) * dmat_ref[...]
        out = jnp.dot(scores.astype(BF16), v, preferred_element_type=F32)
        out = out + jnp.dot((q * qd_ref[...]).astype(BF16), state.astype(BF16),
                            preferred_element_type=F32)
        kv = lax.dot_general((k * kd_ref[...]).astype(BF16), v, (((0,), (0,)), ((), ())),
                             preferred_element_type=F32)
        state_ref[...] = state * jnp.exp(lg * blk) + kv

        y = _rms(out, gain_ref[0])
        g = g_ref[0, rows, :].astype(F32)
        o_ref[0, rows, :] = (g / (1.0 + jnp.exp(-g)) * y).astype(o_ref.dtype)
        return carry

    lax.fori_loop(0, q_ref.shape[1] // blk, block, 0)


def _retention(proj, cos_r, sin_r, ret_gain, *, blk=512):
    b, s, _ = proj.shape
    h = RET_HEADS
    dk, dv = RET_QK_DIM, RET_V_DIM
    v_off = 2 * h * dk // dv
    g_off = v_off + h
    assert s % blk == 0
    log_g = jnp.log(1.0 - 2.0 ** (-5.0 - jnp.arange(h, dtype=F32)))
    log_g = jnp.broadcast_to(log_g[:, None, None], (h, 1, LANES))
    blocks = (2 * _nbytes((s, dk), BF16) + 3 * _nbytes((s, dv), BF16)
              + 2 * _nbytes((s, LANES), F32))
    scratch = (_nbytes((dk, dv), F32) + _nbytes((blk, blk), F32) + 2 * _nbytes((blk, dk), F32))
    return pl.pallas_call(
        functools.partial(_ret_kernel, blk=blk),
        grid=(b, h),
        in_specs=[
            pl.BlockSpec((1, s, dk), lambda bi, hi: (bi, 0, hi)),
            pl.BlockSpec((1, s, dk), lambda bi, hi: (bi, 0, h + hi)),
            pl.BlockSpec((1, s, dv), lambda bi, hi: (bi, 0, v_off + hi)),
            pl.BlockSpec((1, s, dv), lambda bi, hi: (bi, 0, g_off + hi)),
            pl.BlockSpec((1, s, LANES), lambda bi, hi: (bi, 0, 0)),
            pl.BlockSpec((1, s, LANES), lambda bi, hi: (bi, 0, 0)),
            pl.BlockSpec((1, 1, dv), lambda bi, hi: (hi, 0, 0)),
            pl.BlockSpec((1, 1, LANES), lambda bi, hi: (hi, 0, 0)),
        ],
        out_specs=pl.BlockSpec((1, s, dv), lambda bi, hi: (bi, 0, hi)),
        out_shape=jax.ShapeDtypeStruct((b, s, h * dv), BF16),
        scratch_shapes=[
            pltpu.VMEM((dk, dv), F32),
            pltpu.VMEM((blk, blk), F32),
            pltpu.VMEM((blk, dk), F32),
            pltpu.VMEM((blk, dk), F32),
        ],
        compiler_params=pltpu.CompilerParams(
            dimension_semantics=("parallel", "parallel"),
            vmem_limit_bytes=_vmem_limit(blocks, scratch)),
        name="retention",
    )(proj, proj, proj, proj, cos_r, sin_r, ret_gain.reshape(h, 1, dv), log_g)


def _sg_kernel(u_ref, v_ref, gain_ref, w_ref, b_ref, o_ref, *, rows):
    i = lax.broadcasted_iota(jnp.int32, (SG_BLOCK, SG_BLOCK), 0)
    j = lax.broadcasted_iota(jnp.int32, (SG_BLOCK, SG_BLOCK), 1)
    mask = (j >> CHUNK_SHIFT) <= (i >> CHUNK_SHIFT)
    for g in range(SG_GROUPS):
        cols = slice(g * SG_DIM, (g + 1) * SG_DIM)
        w = jnp.where(mask, w_ref[g], 0.0).astype(BF16)
        bias = b_ref[g]
        u = _gelu_tanh(u_ref[0, :, cols].astype(F32))
        v = _rms(_gelu_tanh(v_ref[0, :, cols].astype(F32)), gain_ref[:, cols]).astype(BF16)
        for n in range(rows // SG_BLOCK):
            r = slice(n * SG_BLOCK, (n + 1) * SG_BLOCK)
            mixed = jnp.dot(w, v[r], preferred_element_type=F32) + bias
            o_ref[0, r, cols] = (u[r] * mixed).astype(o_ref.dtype)


def _spatial_gate(proj, sg_norm, sg_w, sg_b, *, rows=1024):
    b, s, width = proj.shape
    sgw = SG_GROUPS * SG_DIM
    u_blk = (width - 2 * sgw) // sgw
    blocks = 3 * _nbytes((rows, sgw), BF16) + _nbytes(sg_w.shape, F32)
    return pl.pallas_call(
        functools.partial(_sg_kernel, rows=rows),
        grid=(b, s // rows),
        in_specs=[
            pl.BlockSpec((1, rows, sgw), lambda bi, n: (bi, n, u_blk)),
            pl.BlockSpec((1, rows, sgw), lambda bi, n: (bi, n, u_blk + 1)),
            pl.BlockSpec((1, sgw), lambda bi, n: (0, 0)),
            pl.BlockSpec((SG_GROUPS, SG_BLOCK, SG_BLOCK), lambda bi, n: (0, 0, 0)),
            pl.BlockSpec((SG_GROUPS, SG_BLOCK, 1), lambda bi, n: (0, 0, 0)),
        ],
        out_specs=pl.BlockSpec((1, rows, sgw), lambda bi, n: (bi, n, 0)),
        out_shape=jax.ShapeDtypeStruct((b, s, sgw), BF16),
        compiler_params=pltpu.CompilerParams(
            dimension_semantics=("parallel", "parallel"),
            vmem_limit_bytes=_vmem_limit(blocks)),
        name="spatial_gate",
    )(proj, proj, sg_norm.reshape(1, sgw), sg_w, sg_b.reshape(SG_GROUPS, SG_BLOCK, 1))


def _diff_attn_kernel(q0_ref, k0_ref, cos0_ref, sin0_ref, q1_ref, k1_ref, cos1_ref, sin1_ref,
                      q2_ref, k2_ref, cos2_ref, sin2_ref,
                      v_ref, qg_ref, kg_ref, sg_ref, lam_ref, o_ref,
                      qp_ref, kn_ref, bias_ref, sa_ref, sb_ref, m_ref, l_ref, acc_ref,
                      *, tq, nq, lambda_init):
    qi = pl.program_id(2)
    d = DIFF_HEAD_DIM
    n_lane_tiles = tq // LANES
    prep_rows = min(tq, 256)
    q_scale = d ** -0.5 * math.log2(math.e)

    def prepare(q_ref, k_ref, cos_ref, sin_ref, tile, qbuf):
        ii = lax.broadcasted_iota(jnp.int32, (2 * d, 2 * d), 0)
        jj = lax.broadcasted_iota(jnp.int32, (2 * d, 2 * d), 1)
        swap_halves = jnp.where((ii ^ (d // 2)) == jj, 1.0, 0.0).astype(BF16)

        def norm_rope(x, gain_cos, gain_sin):
            xf = x.astype(F32)
            rot = jnp.dot(x, swap_halves, preferred_element_type=F32)
            out = []
            for mi in range(2):
                lanes = slice(mi * d, (mi + 1) * d)
                xm = xf[:, lanes]
                r = lax.rsqrt(jnp.mean(xm * xm, axis=-1, keepdims=True) + NORM_EPS)
                out.append(r * (xm * gain_cos + rot[:, lanes] * gain_sin))
            return jnp.concatenate(out, axis=-1).astype(BF16)

        def rolled(g):
            return pltpu.roll(jnp.broadcast_to(g, (8, d)), d // 2, 1)[0:1]

        qg = qg_ref[...] * q_scale
        kg = kg_ref[...]
        qg_rot = rolled(qg)
        kg_rot = rolled(kg)
        base = pl.multiple_of(tile * tq, tq)
        for r0 in range(0, tq, prep_rows):
            rows = slice(r0, r0 + prep_rows)
            c = cos_ref[0, rows, :]
            s = sin_ref[0, rows, :]
            qp_ref[qbuf, rows, :] = norm_rope(q_ref[0, rows, :], qg * c, qg_rot * s)
            kn_ref[pl.ds(base + r0, prep_rows), :] = norm_rope(k_ref[0, rows, :], kg * c,
                                                               kg_rot * s)

    def scores(s_ref, slot, kv, width, masked=False):
        k0 = pl.multiple_of(kv * tq, tq)
        kt = kn_ref[pl.ds(k0, width * tq), :]
        for mi in range(2):
            lanes = slice(mi * d, (mi + 1) * d)
            s = lax.dot_general(qp_ref[slot, :, lanes], kt[:, lanes], (((1,), (1,)), ((), ())),
                                preferred_element_type=F32)
            if masked:
                s = s + bias_ref[...]
            for w in range(width):
                s_ref[mi, kv + w] = s[:, w * tq:(w + 1) * tq]
            m = m_ref[slot, mi]
            for c in range(width * n_lane_tiles):
                m = jnp.maximum(m, s[:, c * LANES:(c + 1) * LANES])
            m_ref[slot, mi] = m

    def finish_max(slot):
        for mi in range(2):
            m_row = jnp.max(m_ref[slot, mi], axis=-1, keepdims=True)
            m_ref[slot, mi] = jnp.broadcast_to(m_row, (tq, LANES))

    def weighted_values(s_ref, slot, kv, width):
        k0 = pl.multiple_of(kv * tq, tq)
        vt = v_ref[0, pl.ds(k0, width * tq), :]
        for mi in range(2):
            m = m_ref[slot, mi]
            l = l_ref[mi]
            ps = []
            for w in range(width):
                for c in range(n_lane_tiles):
                    p = jnp.exp2(s_ref[mi, kv + w, :, c * LANES:(c + 1) * LANES] - m)
                    l = l + p
                    ps.append(p.astype(BF16))
            l_ref[mi] = l
            acc_ref[mi] += jnp.dot(jnp.concatenate(ps, axis=-1), vt,
                                   preferred_element_type=F32)

    def run(cur_ref, nxt_ref, slot):
        nslot = 1 - slot
        odd_tile_count = slot == 0
        n_pairs = (qi + 1) >> 1
        l_ref[...] = jnp.zeros_like(l_ref)
        acc_ref[...] = jnp.zeros_like(acc_ref)

        @pl.when(qi < nq - 1)
        def _():
            m_ref[nslot] = jnp.full(m_ref.shape[1:], MASK_VALUE, F32)

            def body(pair, carry):
                weighted_values(cur_ref, slot, 2 * pair, 2)
                scores(nxt_ref, nslot, 2 * pair, 2)
                return carry

            lax.fori_loop(0, n_pairs, body, 0)

        def finish_next_tile(prepare_ahead):
            if odd_tile_count:
                weighted_values(cur_ref, slot, qi, 1)
                scores(nxt_ref, nslot, qi, 1)
            if prepare_ahead:
                prepare(q2_ref, k2_ref, cos2_ref, sin2_ref, qi + 2, slot)
            scores(nxt_ref, nslot, qi + 1, 1, masked=True)
            finish_max(nslot)

        @pl.when(qi < nq - 2)
        def _():
            finish_next_tile(True)

        if (nq - 2) % 2 == slot:
            @pl.when(qi == nq - 2)
            def _():
                finish_next_tile(False)

        if (nq - 1) % 2 == slot:
            @pl.when(qi == nq - 1)
            def _():
                def body(pair, carry):
                    weighted_values(cur_ref, slot, 2 * pair, 2)
                    return carry

                lax.fori_loop(0, n_pairs, body, 0)
                if odd_tile_count:
                    weighted_values(cur_ref, slot, qi, 1)

    @pl.when(qi == 0)
    def _():
        row = lax.broadcasted_iota(jnp.int32, (tq, tq), 0)
        col = lax.broadcasted_iota(jnp.int32, (tq, tq), 1)
        bias_ref[...] = jnp.where((col >> CHUNK_SHIFT) <= (row >> CHUNK_SHIFT), 0.0, MASK_VALUE)
        prepare(q0_ref, k0_ref, cos0_ref, sin0_ref, 0, 0)
        prepare(q1_ref, k1_ref, cos1_ref, sin1_ref, 1, 1)
        m_ref[0] = jnp.full(m_ref.shape[1:], MASK_VALUE, F32)
        scores(sa_ref, 0, 0, 1, masked=True)
        finish_max(0)

    @pl.when((qi & 1) == 0)
    def _():
        run(sa_ref, sb_ref, 0)

    @pl.when((qi & 1) == 1)
    def _():
        run(sb_ref, sa_ref, 1)

    lam_p = lam_ref[...]
    lam = (jnp.exp(jnp.sum(lam_p[0:1] * lam_p[1:2], axis=-1, keepdims=True))
           - jnp.exp(jnp.sum(lam_p[2:3] * lam_p[3:4], axis=-1, keepdims=True))
           + lambda_init)
    l0 = jnp.sum(l_ref[0], axis=-1, keepdims=True)
    l1 = jnp.sum(l_ref[1], axis=-1, keepdims=True)
    out = acc_ref[0] / l0 - lam * (acc_ref[1] / l1)
    o_ref[0] = (_rms(out, sg_ref[...]) * (1.0 - lambda_init)).astype(o_ref.dtype)


def _diff_attention(proj, cos_d, sin_d, q_gain, k_gain, sub_gain, lam_params, lambda_init,
                    *, tq=512):
    b, s, _ = proj.shape
    h = DIFF_HEADS
    d = DIFF_HEAD_DIM
    hw = 2 * d
    nq = s // tq
    assert nq >= 2
    blocks = (7 * _nbytes((tq, hw), BF16) + 6 * _nbytes((tq, d), F32) + _nbytes((s, hw), BF16))
    scratch = (_nbytes((2, tq, hw), BF16) + _nbytes((s, hw), BF16) + 2 * _nbytes((2, tq, s), F32)
               + _nbytes((2, 2, tq, LANES), F32) + _nbytes((2, tq, LANES), F32)
               + _nbytes((2, tq, hw), F32))
    fixed = lambda tile, col: (lambda bi, hi, qi: (bi, tile, col(hi)))
    ahead = lambda col: (lambda bi, hi, qi: (bi, jnp.minimum(qi + 2, nq - 1), col(hi)))
    q_col = lambda hi: hi
    k_col = lambda hi: h + hi
    rope_col = lambda hi: 0
    tile_specs = lambda index: [
        pl.BlockSpec((1, tq, hw), index(q_col)),
        pl.BlockSpec((1, tq, hw), index(k_col)),
        pl.BlockSpec((1, tq, d), index(rope_col)),
        pl.BlockSpec((1, tq, d), index(rope_col)),
    ]
    tile_args = (proj, proj, cos_d, sin_d)
    return pl.pallas_call(
        functools.partial(_diff_attn_kernel, tq=tq, nq=nq, lambda_init=lambda_init),
        grid=(b, h, nq),
        in_specs=[
            *tile_specs(functools.partial(fixed, 0)),
            *tile_specs(functools.partial(fixed, 1)),
            *tile_specs(ahead),
            pl.BlockSpec((1, s, hw), lambda bi, hi, qi: (bi, 0, 2 * h + hi)),
            pl.BlockSpec((1, d), lambda bi, hi, qi: (0, 0)),
            pl.BlockSpec((1, d), lambda bi, hi, qi: (0, 0)),
            pl.BlockSpec((1, hw), lambda bi, hi, qi: (0, 0)),
            pl.BlockSpec((4, d), lambda bi, hi, qi: (0, 0)),
        ],
        out_specs=pl.BlockSpec((1, tq, hw), lambda bi, hi, qi: (bi, qi, hi)),
        out_shape=jax.ShapeDtypeStruct((b, s, h * hw), BF16),
        scratch_shapes=[
            pltpu.VMEM((2, tq, hw), BF16),
            pltpu.VMEM((s, hw), BF16),
            pltpu.VMEM((tq, tq), F32),
            pltpu.VMEM((2, nq, tq, tq), F32),
            pltpu.VMEM((2, nq, tq, tq), F32),
            pltpu.VMEM((2, 2, tq, LANES), F32),
            pltpu.VMEM((2, tq, LANES), F32),
            pltpu.VMEM((2, tq, hw), F32),
        ],
        compiler_params=pltpu.CompilerParams(
            dimension_semantics=("parallel", "parallel", "arbitrary"),
            vmem_limit_bytes=_vmem_limit(blocks, scratch)),
        name="diff_attention",
    )(*tile_args, *tile_args, *tile_args, proj,
      q_gain.reshape(1, d), k_gain.reshape(1, d), sub_gain.reshape(1, hw), lam_params)


def _xa_kernel(*refs, heads, n_cast):
    x_ref, g_ref, wq_ref, kv_ref, qg_ref, kg_ref, wo_ref = refs[:7]
    o_ref = refs[7 + n_cast]
    kn_ref = refs[-1]
    dm = x_ref.shape[2]
    hd = dm // heads

    @pl.when(pl.program_id(1) == 0)
    def _():
        for h in range(heads):
            cols = slice(h * hd, (h + 1) * hd)
            kn_ref[:, cols] = _rms(kv_ref[0, :, cols].astype(F32), kg_ref[...]).astype(BF16)

    _Casts.run(refs[7:7 + n_cast], refs[8 + n_cast:8 + 2 * n_cast])
    x = x_ref[0]
    r = lax.rsqrt(jnp.mean(x * x, axis=-1, keepdims=True) + NORM_EPS)
    q = jnp.dot((x * g_ref[...]).astype(BF16), wq_ref[...], preferred_element_type=F32)
    scale = hd ** -0.5
    heads_out = []
    for h in range(heads):
        cols = slice(h * hd, (h + 1) * hd)
        qn = (_rms(q[:, cols] * r, qg_ref[...]) * scale).astype(BF16)
        s = lax.dot_general(qn, kn_ref[:, cols], (((1,), (1,)), ((), ())),
                            preferred_element_type=F32)
        e = jnp.exp(s - jnp.max(s, axis=-1, keepdims=True))
        p = e / jnp.sum(e, axis=-1, keepdims=True)
        v = kv_ref[0, :, dm + h * hd: dm + (h + 1) * hd]
        heads_out.append(jnp.dot(p.astype(BF16), v, preferred_element_type=F32).astype(BF16))
    att = jnp.concatenate(heads_out, axis=-1)
    o_ref[0] = x + jnp.dot(att, wo_ref[...], preferred_element_type=F32)


def _cross_attention_block(x, gain, wq, wo, layer, kv, q_gain, k_gain, *, tm=512,
                           cast_weights=()):
    b, s, dm = x.shape
    mlen = kv.shape[1]
    hd = dm // XA_HEADS
    ni = s // tm
    resident = pl.Buffered(buffer_count=1)
    casts = _Casts(cast_weights, b * ni, lambda bi, i: bi * ni + i)
    blocks = 2 * _nbytes((tm, dm), F32) + _nbytes((mlen, 2 * dm), BF16) + casts.block_bytes
    scratch = 2 * _nbytes((dm, dm), BF16) + _nbytes((mlen, dm), BF16)
    out, *cast_out = pl.pallas_call(
        functools.partial(_xa_kernel, heads=XA_HEADS, n_cast=casts.n),
        grid=(b, ni),
        in_specs=[
            pl.BlockSpec((1, tm, dm), lambda bi, i: (bi, i, 0)),
            pl.BlockSpec((1, dm), lambda bi, i: (0, 0)),
            pl.BlockSpec((None, dm, dm), lambda bi, i: (layer, 0, 0), pipeline_mode=resident),
            pl.BlockSpec((1, mlen, 2 * dm), lambda bi, i: (layer * b + bi, 0, 0)),
            pl.BlockSpec((1, hd), lambda bi, i: (0, 0)),
            pl.BlockSpec((1, hd), lambda bi, i: (0, 0)),
            pl.BlockSpec((None, dm, dm), lambda bi, i: (layer, 0, 0), pipeline_mode=resident),
            *casts.in_specs,
        ],
        out_specs=[pl.BlockSpec((1, tm, dm), lambda bi, i: (bi, i, 0)), *casts.out_specs],
        out_shape=[jax.ShapeDtypeStruct((b, s, dm), F32), *casts.out_shapes],
        scratch_shapes=[pltpu.VMEM((mlen, dm), BF16)],
        compiler_params=pltpu.CompilerParams(
            dimension_semantics=("parallel", "arbitrary"),
            vmem_limit_bytes=_vmem_limit(blocks, scratch)),
        name="cross_attention_block",
    )(x, gain.reshape(1, dm), wq, kv, q_gain.reshape(1, hd), k_gain.reshape(1, hd), wo,
      *casts.args)
    return (out, *casts.finish(cast_out)) if casts.n else out


def kernel(x, mem, positions, norm_mix, norm_xa, norm_mem, norm_ffn, ev_w_in, ev_ret_gain,
           ev_sg_norm, ev_sg_w, ev_sg_b, ev_w_out, od_w_qkv, od_q_gain, od_k_gain, od_lam_q1,
           od_lam_k1, od_lam_q2, od_lam_k2, od_sub_gain, od_w_o, xa_w_q, xa_w_kv, xa_q_gain,
           xa_k_gain, xa_w_o, ffn_w1, ffn_w2):
    b, s, dm = x.shape
    mlen = mem.shape[1]
    depth = norm_mix.shape[0]
    m = b * s
    assert depth == 2, "the weight-cast schedule below is laid out for one even + one odd layer"
    cos_r, sin_r, cos_d, sin_d, ev_w_in = _rope_tables(positions, cast_weights=(ev_w_in,))
    xf = x.reshape(m, dm)
    mem_f = mem.reshape(b * mlen, dm)

    proj, ev_w_out, xa_w_q, xa_w_o = _norm_mm(
        xf, norm_mix[0], ev_w_in, 0, cast_weights=(ev_w_out, xa_w_q, xa_w_o),
        name="even_in_proj")
    proj = proj.reshape(b, s, -1)
    out_a = _retention(proj, cos_r, sin_r, ev_ret_gain[0])
    out_b = _spatial_gate(proj, ev_sg_norm[0], ev_sg_w[0], ev_sg_b[0])
    xf, ffn_w1 = _mm2_res(out_a.reshape(m, -1), out_b.reshape(m, -1), ev_w_out, 0, xf,
                          cast_weights=(ffn_w1,), name="even_out_proj")

    kv = _norm_mm_all_layers(mem_f, norm_mem, xa_w_kv, name="xa_kv_proj")
    kv = kv.reshape(depth * b, mlen, 2 * dm)

    for li in range(depth):
        if li == 1:
            lambda_init = 0.8 - 0.6 * math.exp(-0.3 * li)
            proj = _norm_mm(xf, norm_mix[li], od_w_qkv, 0, name="diff_qkv_proj")
            lam_params = jnp.stack([od_lam_q1[0], od_lam_k1[0], od_lam_q2[0], od_lam_k2[0]])
            att = _diff_attention(proj.reshape(b, s, -1), cos_d, sin_d, od_q_gain[0],
                                  od_k_gain[0], od_sub_gain[0], lam_params, lambda_init)
            xf = _mm_res_full(att.reshape(m, -1), od_w_o, 0, xf, name="diff_out_proj")

        xa_out = _cross_attention_block(
            xf.reshape(b, s, dm), norm_xa[li], xa_w_q, xa_w_o, li, kv,
            xa_q_gain[li], xa_k_gain[li], cast_weights=(od_w_qkv, od_w_o) if li == 0 else ())
        if li == 0:
            xa_out, od_w_qkv, od_w_o = xa_out
        xf = xa_out.reshape(m, dm)

        hid = _norm_mm(xf, norm_ffn[li], ffn_w1, li, act="relu2",
                       cast_weights=(ffn_w2,) if li == 0 else (), name="ffn_up")
        if li == 0:
            hid, ffn_w2 = hid
        xf = _mm_res(hid, ffn_w2, li, xf, name="ffn_down")

    return xf.reshape(b, s, dm)
```
